```python
import jax
import jax.numpy as jnp
from jax import lax
import numpy as np


D_MODEL = 2048
BATCH = 16
SEQ = 2048
DEPTH = 2

HEAD_DIM = 128
DIL_CONFIGS = ((128, 1), (512, 4), (2048, 16))
N_GROUPS_A = len(DIL_CONFIGS)
HEADS_PER_GROUP_A = 8
WIDTH_A = HEADS_PER_GROUP_A * HEAD_DIM
N_HEADS_B = D_MODEL // HEAD_DIM
WIDTH_B = N_HEADS_B * HEAD_DIM
ROT_DIM = HEAD_DIM // 4
ROPE_THETA = 500000.0
D_FF = ((8 * D_MODEL + 3 * 256 - 1) // (3 * 256)) * 256
BLOCK = 128
N_A_LAYERS = DEPTH // 2
N_B_LAYERS = DEPTH - N_A_LAYERS
EPS = 1e-6
NEG_INF = -1e30

kernel_name = 'yoco_dilated_fox_hybrid'


def rmsnorm(x, g):
    xf = x.astype(jnp.float32)
    y = xf * lax.rsqrt(jnp.mean(xf * xf, axis=-1, keepdims=True) + EPS)
    return (y * g.astype(jnp.float32)).astype(x.dtype)


def modulate(h, shift, scale):
    return h * (1.0 + scale[:, None, :]) + shift[:, None, :]


def swiglu(h, w_in, w_out):
    g, u = jnp.split(h @ w_in, 2, axis=-1)
    return (jax.nn.silu(g) * u) @ w_out


def rope_tables(seq):
    inv = ROPE_THETA ** (-jnp.arange(0, ROT_DIM, 2, dtype=jnp.float32) / ROT_DIM)
    ang = jnp.arange(seq, dtype=jnp.float32)[:, None] * inv[None, :]
    return jnp.cos(ang), jnp.sin(ang)


def partial_rope(x, cos, sin):
    half = ROT_DIM // 2
    shape = (1, x.shape[1]) + (1,) * (x.ndim - 3) + (half,)
    cos = cos.reshape(shape).astype(x.dtype)
    sin = sin.reshape(shape).astype(x.dtype)
    x1 = x[..., :half]
    x2 = x[..., half:ROT_DIM]
    return jnp.concatenate([x1 * cos - x2 * sin, x2 * cos + x1 * sin, x[..., ROT_DIM:]], axis=-1)


def dilated_group_attention(q, k, v, window, dilation):
    b, s, h, dh = q.shape
    n_back = window // dilation
    sub_len = s // dilation
    nb = -(-sub_len // BLOCK)
    pad = nb * BLOCK - sub_len

    def to_blocks(t):
        t = t.reshape(b, sub_len, dilation, h, dh)
        t = jnp.pad(t, ((0, 0), (0, pad), (0, 0), (0, 0), (0, 0)))
        return t.reshape(b, nb, BLOCK, dilation, h, dh)

    def with_prev(t):
        prev = jnp.pad(t[:, :-1], ((0, 0), (1, 0), (0, 0), (0, 0), (0, 0), (0, 0)))
        return jnp.concatenate([prev, t], axis=2)

    qb = to_blocks(q)
    kb = with_prev(to_blocks(k))
    vb = with_prev(to_blocks(v))
    scores = jnp.einsum('bnqrhd,bnkrhd->bnrhqk', qb, kb,
                        preferred_element_type=jnp.float32) * (dh ** -0.5)
    qi = jnp.arange(BLOCK)[:, None]
    kj = jnp.arange(2 * BLOCK)[None, :]
    dist = qi + BLOCK - kj
    band = (dist >= 0) & (dist <= n_back)
    exists = (jnp.arange(nb)[:, None, None] > 0) | (kj >= BLOCK)[None]
    mask = band[None] & exists
    scores = jnp.where(mask[None, :, None, None], scores, NEG_INF)
    m = jnp.max(scores, axis=-1, keepdims=True)
    p = jnp.exp(scores - m)
    den = jnp.sum(p, axis=-1)
    o = jnp.einsum('bnrhqk,bnkrhd->bnqrhd', p.astype(v.dtype), vb,
                   preferred_element_type=jnp.float32)
    o = o / jnp.transpose(den, (0, 1, 4, 2, 3))[..., None]
    lse = jnp.transpose(m[..., 0] + jnp.log(den), (0, 1, 4, 2, 3))
    o = o.reshape(b, nb * BLOCK, dilation, h, dh)[:, :sub_len].reshape(b, s, h, dh)
    lse = lse.reshape(b, nb * BLOCK, dilation, h)[:, :sub_len].reshape(b, s, h)
    return o, lse


def dilated_mixture_attention(h, w_qkv, g_q, g_k, w_o, cos, sin):
    b, s, _ = h.shape
    qkv = (h @ w_qkv).reshape(b, s, 3, N_GROUPS_A, HEADS_PER_GROUP_A, HEAD_DIM)
    q = partial_rope(rmsnorm(qkv[:, :, 0], g_q[:, None, :]), cos, sin)
    k = partial_rope(rmsnorm(qkv[:, :, 1], g_k[:, None, :]), cos, sin)
    v = qkv[:, :, 2]
    outs, lses = [], []
    for gi, (window, dilation) in enumerate(DIL_CONFIGS):
        o, l = dilated_group_attention(q[:, :, gi], k[:, :, gi], v[:, :, gi], window, dilation)
        outs.append(o)
        lses.append(l)
    alpha = jax.nn.softmax(jnp.stack(lses), axis=0)
    o = jnp.sum(alpha[..., None] * jnp.stack(outs), axis=0)
    return o.astype(h.dtype).reshape(b, s, WIDTH_A) @ w_o


def shared_kv(x, mod_kv, g_norm_kv, w_kv, g_k, w_f, b_f):
    b, s, _ = x.shape
    shift, scale = jnp.split(mod_kv, 2, axis=-1)
    h = modulate(rmsnorm(x, g_norm_kv), shift, scale)
    kv = (h @ w_kv).reshape(b, s, 2, N_HEADS_B, HEAD_DIM)
    k = rmsnorm(kv[:, :, 0], g_k)
    v = kv[:, :, 1]
    log_f = jax.nn.log_sigmoid((h @ w_f + b_f).astype(jnp.float32))
    cum_log_f = jnp.cumsum(log_f, axis=1)
    return k, v, cum_log_f


def forgetting_attention(h, w_q, g_q, w_o, k, v, cum_log_f):
    b, s, _ = h.shape
    nb = s // BLOCK
    q = rmsnorm((h @ w_q).reshape(b, s, N_HEADS_B, HEAD_DIM), g_q)
    q_blocks = jnp.moveaxis(q.reshape(b, nb, BLOCK, N_HEADS_B, HEAD_DIM), 1, 0)
    fq_blocks = jnp.moveaxis(cum_log_f.reshape(b, nb, BLOCK, N_HEADS_B), 1, 0)
    fk = jnp.transpose(cum_log_f, (0, 2, 1))
    key_pos = jnp.arange(s)
    scale = HEAD_DIM ** -0.5

    def attend_block(args):
        q_blk, fq, blk = args
        logits = jnp.einsum('bqhd,bkhd->bhqk', q_blk, k,
                            preferred_element_type=jnp.float32) * scale
        logits = logits + jnp.transpose(fq, (0, 2, 1))[..., None] - fk[:, :, None, :]
        q_pos = blk * BLOCK + jnp.arange(BLOCK)
        logits = jnp.where(key_pos[None, :] <= q_pos[:, None], logits, NEG_INF)
        p = jax.nn.softmax(logits, axis=-1)
        return jnp.einsum('bhqk,bkhd->bqhd', p.astype(v.dtype), v)

    o = lax.map(attend_block, (q_blocks, fq_blocks, jnp.arange(nb)))
    o = jnp.moveaxis(o, 0, 1).reshape(b, s, WIDTH_B)
    return o @ w_o


def setup_inputs(seed: int = 0) -> dict:
    key = jax.random.key(seed)
    ks = jax.random.split(key, 24)

    def nrm(k, shape, scale):
        return jax.random.normal(k, shape, jnp.float32) * scale

    def gain(k, shape):
        return 1.0 + 0.02 * jax.random.normal(k, shape, jnp.float32)

    d = D_MODEL
    return {
        'x': nrm(ks[0], (BATCH, SEQ, d), 1.0),
        'c': nrm(ks[1], (BATCH, d), 1.0),
        'w_ada': nrm(ks[2], (DEPTH, d, 6 * d), 0.5 * d ** -0.5),
        'b_ada': nrm(ks[3], (DEPTH, 6 * d), 0.02),
        'g_norm_attn': gain(ks[4], (DEPTH, d)),
        'g_norm_ffn': gain(ks[5], (DEPTH, d)),
        'w_qkv_a': nrm(ks[6], (N_A_LAYERS, d, 3 * N_GROUPS_A * WIDTH_A), d ** -0.5),
        'g_qk_a': gain(ks[7], (N_A_LAYERS, 2, N_GROUPS_A, HEAD_DIM)),
        'w_o_a': nrm(ks[8], (N_A_LAYERS, WIDTH_A, d), WIDTH_A ** -0.5),
        'w_ada_kv': nrm(ks[9], (d, 2 * d), 0.5 * d ** -0.5),
        'b_ada_kv': nrm(ks[10], (2 * d,), 0.02),
        'g_norm_kv': gain(ks[11], (d,)),
        'w_kv': nrm(ks[12], (d, 2 * WIDTH_B), d ** -0.5),
        'g_k_b': gain(ks[13], (HEAD_DIM,)),
        'w_f': nrm(ks[14], (d, N_HEADS_B), 0.5 * d ** -0.5),
        'b_f': jax.random.uniform(ks[15], (N_HEADS_B,), jnp.float32, 1.0, 5.0),
        'w_q_b': nrm(ks[16], (N_B_LAYERS, d, WIDTH_B), d ** -0.5),
        'g_q_b': gain(ks[17], (N_B_LAYERS, HEAD_DIM)),
        'w_o_b': nrm(ks[18], (N_B_LAYERS, WIDTH_B, d), WIDTH_B ** -0.5),
        'w_ffn_in': nrm(ks[19], (DEPTH, d, 2 * D_FF), d ** -0.5),
        'w_ffn_out': nrm(ks[20], (DEPTH, D_FF, d), D_FF ** -0.5),
    }


def reference(x, c, w_ada, b_ada, g_norm_attn, g_norm_ffn, w_qkv_a, g_qk_a, w_o_a,
              w_ada_kv, b_ada_kv, g_norm_kv, w_kv, g_k_b, w_f, b_f, w_q_b, g_q_b, w_o_b,
              w_ffn_in, w_ffn_out):
    s = x.shape[1]
    cos, sin = rope_tables(s)
    c_act = jax.nn.silu(c)
    k_sh = None
    v_sh = None
    f_sh = None
    for layer in range(DEPTH):
        mods = c_act @ w_ada[layer] + b_ada[layer]
        sh_a, sc_a, gt_a, sh_f, sc_f, gt_f = jnp.split(mods, 6, axis=-1)
        h = modulate(rmsnorm(x, g_norm_attn[layer]), sh_a, sc_a)
        if layer < N_A_LAYERS:
            i = layer
            y = dilated_mixture_attention(h, w_qkv_a[i], g_qk_a[i, 0], g_qk_a[i, 1],
                                          w_o_a[i], cos, sin)
        else:
            i = layer - N_A_LAYERS
            y = forgetting_attention(h, w_q_b[i], g_q_b[i], w_o_b[i], k_sh, v_sh, f_sh)
        x = x + gt_a[:, None, :] * y
        h = modulate(rmsnorm(x, g_norm_ffn[layer]), sh_f, sc_f)
        x = x + gt_f[:, None, :] * swiglu(h, w_ffn_in[layer], w_ffn_out[layer])
        if layer == N_A_LAYERS - 1:
            mod_kv = c_act @ w_ada_kv + b_ada_kv
            k_sh, v_sh, f_sh = shared_kv(x, mod_kv, g_norm_kv, w_kv, g_k_b, w_f, b_f)
    return x
```

```python
import functools
import math

import jax
import jax.numpy as jnp
from jax import lax
from jax.experimental import pallas as pl
from jax.experimental.pallas import tpu as pltpu

HEAD_DIM = 128
DIL_CONFIGS = ((128, 1), (512, 4), (2048, 16))
ROT_DIM = HEAD_DIM // 4
ROPE_THETA = 500000.0
ATT_BLOCK = 128
EPS = 1e-6
NEG_INF = -1e30

LANES = 128
VMEM_LIMIT_BYTES = 56 * 1024 * 1024

F32 = jnp.float32
BF16 = jnp.bfloat16


def _compiler_params(semantics):
    return pltpu.CompilerParams(dimension_semantics=semantics, vmem_limit_bytes=VMEM_LIMIT_BYTES)


def _pick_tile(n, candidates):
    for t in candidates:
        if n % t == 0:
            return t
    raise ValueError(f"no tile in {candidates} divides {n}")


def _head_rmsnorm(a, gain):
    return a * lax.rsqrt(jnp.mean(a * a, axis=-1, keepdims=True) + EPS) * gain


def _silu(v):
    return v * jax.nn.sigmoid(v)


def _mods_kernel(c_ref, w_ref, b_ref, o_ref):
    c_act = _silu(c_ref[...])
    o_ref[...] = jnp.dot(c_act.astype(BF16), w_ref[...].astype(BF16),
                         preferred_element_type=F32) + b_ref[...]


def _mods(c, w, b, layer):
    bsz, d = c.shape
    n = w.shape[-1]
    tn = _pick_tile(n, (512, 256, 128))
    return pl.pallas_call(
        _mods_kernel,
        grid=(n // tn,),
        in_specs=[
            pl.BlockSpec((bsz, d), lambda j: (0, 0)),
            pl.BlockSpec((None, d, tn), lambda j: (layer, 0, j)),
            pl.BlockSpec((None, 1, tn), lambda j: (layer, 0, j)),
        ],
        out_specs=pl.BlockSpec((bsz, tn), lambda j: (0, j)),
        out_shape=jax.ShapeDtypeStruct((bsz, n), F32),
        compiler_params=_compiler_params(("parallel",)),
    )(c, w, b.reshape(b.shape[0], 1, n))


def _fill_lhs(x_ref, gn_ref, sh_ref, sc_ref, h_scr, row_chunk):
    tm = x_ref.shape[0]
    gain = gn_ref[...]
    mul = 1.0 + sc_ref[...]
    shift = sh_ref[...]

    def body(r, carry):
        rows = pl.ds(pl.multiple_of(r * row_chunk, row_chunk), row_chunk)
        xf = x_ref[rows, :]
        y = xf * lax.rsqrt(jnp.mean(xf * xf, axis=-1, keepdims=True) + EPS) * gain
        h_scr[rows, :] = (y * mul + shift).astype(BF16)
        return carry

    lax.fori_loop(0, tm // row_chunk, body, 0)


def _proj_kernel(*refs, n_norm_tiles, rope, forget, tiles_per_batch, row_chunk):
    x_ref, gn_ref, sh_ref, sc_ref, w_ref, gain_ref = refs[:6]
    pos = 6
    if rope:
        cos_ref, s1_ref, s2_ref = refs[pos:pos + 3]
        pos += 3
    if forget:
        wf_ref, bf_ref, tri_ref = refs[pos:pos + 3]
        pos += 3
    o_ref = refs[pos]
    pos += 1
    if forget:
        f_ref, ft_ref = refs[pos:pos + 2]
        pos += 2
    h_scr, acc_scr = refs[pos:pos + 2]
    pos += 2
    if forget:
        carry_scr = refs[pos]

    i = pl.program_id(0)
    j = pl.program_id(1)
    tm, tn = acc_scr.shape

    @pl.when(j == 0)
    def _():
        _fill_lhs(x_ref, gn_ref, sh_ref, sc_ref, h_scr, row_chunk)
        if forget:
            z = jnp.dot(h_scr[...], wf_ref[...], preferred_element_type=F32) + bf_ref[...]
            log_f = jnp.minimum(z, 0.0) - jnp.log1p(jnp.exp(-jnp.abs(z)))
            hi = log_f.astype(BF16)
            rem = log_f - hi.astype(F32)
            mid = rem.astype(BF16)
            lo = (rem - mid.astype(F32)).astype(BF16)
            tri = tri_ref[...]
            cum = (jnp.dot(tri, hi, preferred_element_type=F32)
                   + jnp.dot(tri, mid, preferred_element_type=F32)
                   + jnp.dot(tri, lo, preferred_element_type=F32))

            @pl.when(i % tiles_per_batch == 0)
            def _():
                carry_scr[...] = jnp.zeros_like(carry_scr)

            cum = cum + carry_scr[...]
            carry_scr[...] = cum[tm - 1:tm, :]
            f_ref[...] = cum
            ft_ref[...] = cum.T

    acc_scr[...] = jnp.dot(h_scr[...], w_ref[...], preferred_element_type=F32)

    @pl.when(j < n_norm_tiles)
    def _():
        gain = gain_ref[...]

        def body(r, carry):
            rows = pl.ds(pl.multiple_of(r * row_chunk, row_chunk), row_chunk)
            if rope:
                cos = cos_ref[rows, :]
                s1 = s1_ref[rows, :]
                s2 = s2_ref[rows, :]
            for h in range(tn // HEAD_DIM):
                cols = slice(h * HEAD_DIM, (h + 1) * HEAD_DIM)
                y = _head_rmsnorm(acc_scr[rows, cols], gain)
                if rope:
                    y = (y * cos + pltpu.roll(y, HEAD_DIM - ROT_DIM // 2, 1) * s1
                         + pltpu.roll(y, ROT_DIM // 2, 1) * s2)
                o_ref[rows, cols] = y.astype(o_ref.dtype)
            return carry

        lax.fori_loop(0, tm // row_chunk, body, 0)

    @pl.when(j >= n_norm_tiles)
    def _():
        o_ref[...] = acc_scr[...].astype(o_ref.dtype)


def _proj(x2, seq, gnorm, shift, scale, w, layer, gains, n_norm_tiles, rope_tables=None,
          forget=None):
    m, d = x2.shape
    n = w.shape[-1]
    tm = _pick_tile(seq, (1024, 512, 256, 128))
    tn = 1024
    assert n % tn == 0
    tpb = seq // tm
    n_gain = gains.shape[0]
    rope = rope_tables is not None
    with_forget = forget is not None

    in_specs = [
        pl.BlockSpec((tm, d), lambda i, j: (i, 0)),
        pl.BlockSpec((1, d), lambda i, j: (0, 0)),
        pl.BlockSpec((None, 1, d), lambda i, j: (i // tpb, 0, 0)),
        pl.BlockSpec((None, 1, d), lambda i, j: (i // tpb, 0, 0)),
        pl.BlockSpec((None, d, tn), lambda i, j: (layer, 0, j)),
        pl.BlockSpec((None, 1, HEAD_DIM), lambda i, j: (jnp.minimum(j, n_gain - 1), 0, 0)),
    ]
    args = [x2, gnorm.reshape(1, d), shift, scale, w, gains]
    if rope:
        in_specs += [pl.BlockSpec((tm, HEAD_DIM), lambda i, j: (i % tpb, 0))] * 3
        args += list(rope_tables)
    out_specs = [pl.BlockSpec((tm, tn), lambda i, j: (i, j))]
    out_shape = [jax.ShapeDtypeStruct((m, n), BF16)]
    scratch = [pltpu.VMEM((tm, d), BF16), pltpu.VMEM((tm, tn), F32)]
    if with_forget:
        wf_pad, bf_pad = forget
        tri = (lax.broadcasted_iota(jnp.int32, (tm, tm), 0)
               >= lax.broadcasted_iota(jnp.int32, (tm, tm), 1)).astype(BF16)
        in_specs += [
            pl.BlockSpec((d, LANES), lambda i, j: (0, 0)),
            pl.BlockSpec((1, LANES), lambda i, j: (0, 0)),
            pl.BlockSpec((tm, tm), lambda i, j: (0, 0)),
        ]
        args += [wf_pad, bf_pad, tri]
        out_specs += [
            pl.BlockSpec((tm, LANES), lambda i, j: (i, 0)),
            pl.BlockSpec((None, LANES, tm), lambda i, j: (i // tpb, 0, i % tpb)),
        ]
        out_shape += [
            jax.ShapeDtypeStruct((m, LANES), F32),
            jax.ShapeDtypeStruct((m // seq, LANES, seq), F32),
        ]
        scratch.append(pltpu.VMEM((1, LANES), F32))

    kern = functools.partial(_proj_kernel, n_norm_tiles=n_norm_tiles, rope=rope,
                             forget=with_forget, tiles_per_batch=tpb, row_chunk=128)
    outs = pl.pallas_call(
        kern,
        grid=(m // tm, n // tn),
        in_specs=in_specs,
        out_specs=out_specs,
        out_shape=out_shape,
        scratch_shapes=scratch,
        compiler_params=_compiler_params(("arbitrary", "arbitrary")),
    )(*args)
    return outs if with_forget else outs[0]


def _swiglu_kernel(x_ref, gn_ref, sh_ref, sc_ref, wg_ref, wu_ref, o_ref, h_scr, *, row_chunk):
    @pl.when(pl.program_id(1) == 0)
    def _():
        _fill_lhs(x_ref, gn_ref, sh_ref, sc_ref, h_scr, row_chunk)

    h = h_scr[...]
    g = jnp.dot(h, wg_ref[...], preferred_element_type=F32)
    u = jnp.dot(h, wu_ref[...], preferred_element_type=F32)
    o_ref[...] = (_silu(g) * u).astype(o_ref.dtype)


def _swiglu_in(x2, seq, gnorm, shift, scale, w, layer):
    m, d = x2.shape
    f = w.shape[-1] // 2
    tm = _pick_tile(seq, (1024, 512, 256, 128))
    tf = _pick_tile(f, (512, 256, 128))
    tpb = seq // tm
    nf = f // tf
    return pl.pallas_call(
        functools.partial(_swiglu_kernel, row_chunk=128),
        grid=(m // tm, nf),
        in_specs=[
            pl.BlockSpec((tm, d), lambda i, j: (i, 0)),
            pl.BlockSpec((1, d), lambda i, j: (0, 0)),
            pl.BlockSpec((None, 1, d), lambda i, j: (i // tpb, 0, 0)),
            pl.BlockSpec((None, 1, d), lambda i, j: (i // tpb, 0, 0)),
            pl.BlockSpec((None, d, tf), lambda i, j: (layer, 0, j)),
            pl.BlockSpec((None, d, tf), lambda i, j: (layer, 0, nf + j)),
        ],
        out_specs=pl.BlockSpec((tm, tf), lambda i, j: (i, j)),
        out_shape=jax.ShapeDtypeStruct((m, f), BF16),
        scratch_shapes=[pltpu.VMEM((tm, d), BF16)],
        compiler_params=_compiler_params(("parallel", "arbitrary")),
    )(x2, gnorm.reshape(1, d), shift, scale, w, w)


def _mmres_kernel(a_ref, w_ref, x_ref, gt_ref, o_ref):
    y = jnp.dot(a_ref[...], w_ref[...], preferred_element_type=F32)
    o_ref[...] = x_ref[...] + gt_ref[...] * y


def _mm_residual(a, w, layer, x2, seq, gate):
    m, k = a.shape
    n = w.shape[-1]
    tm = _pick_tile(seq, (1024, 512, 256, 128))
    tn = _pick_tile(n, (512, 256, 128))
    tpb = seq // tm
    return pl.pallas_call(
        _mmres_kernel,
        grid=(m // tm, n // tn),
        in_specs=[
            pl.BlockSpec((tm, k), lambda i, j: (i, 0)),
            pl.BlockSpec((None, k, tn), lambda i, j: (layer, 0, j)),
            pl.BlockSpec((tm, tn), lambda i, j: (i, j)),
            pl.BlockSpec((None, 1, tn), lambda i, j: (i // tpb, 0, j)),
        ],
        out_specs=pl.BlockSpec((tm, tn), lambda i, j: (i, j)),
        out_shape=jax.ShapeDtypeStruct((m, n), F32),
        compiler_params=_compiler_params(("parallel", "parallel")),
    )(a, w, x2, gate)


def _dilated_kernel(q_ref, kc_ref, kp_ref, vc_ref, vp_ref, o_ref, lse_ref, *, n_back, n_heads):
    blk = pl.program_id(2)
    qi = lax.broadcasted_iota(jnp.int32, (ATT_BLOCK, ATT_BLOCK), 0)
    kj = lax.broadcasted_iota(jnp.int32, (ATT_BLOCK, ATT_BLOCK), 1)
    dist_cur = qi - kj
    dist_prev = dist_cur + ATT_BLOCK
    mask_cur = (dist_cur >= 0) & (dist_cur <= n_back)
    mask_prev = (dist_prev >= 0) & (dist_prev <= n_back) & (blk > 0)
    lane = lax.broadcasted_iota(jnp.int32, (ATT_BLOCK, LANES), 1)
    scale = HEAD_DIM ** -0.5
    contract_last = (((1,), (1,)), ((), ()))

    lse_tile = jnp.zeros((ATT_BLOCK, LANES), F32)
    for h in range(n_heads):
        cols = slice(h * HEAD_DIM, (h + 1) * HEAD_DIM)
        q = q_ref[:, cols]
        s_cur = lax.dot_general(q, kc_ref[:, cols], contract_last,
                                preferred_element_type=F32) * scale
        s_prev = lax.dot_general(q, kp_ref[:, cols], contract_last,
                                 preferred_element_type=F32) * scale
        s_cur = jnp.where(mask_cur, s_cur, NEG_INF)
        s_prev = jnp.where(mask_prev, s_prev, NEG_INF)
        mx = jnp.maximum(jnp.max(s_cur, axis=-1, keepdims=True),
                         jnp.max(s_prev, axis=-1, keepdims=True))
        p_cur = jnp.exp(s_cur - mx)
        p_prev = jnp.exp(s_prev - mx)
        den = jnp.sum(p_cur, axis=-1, keepdims=True) + jnp.sum(p_prev, axis=-1, keepdims=True)
        o = (jnp.dot(p_cur.astype(BF16), vc_ref[:, cols], preferred_element_type=F32)
             + jnp.dot(p_prev.astype(BF16), vp_ref[:, cols], preferred_element_type=F32))
        o_ref[:, cols] = (o / den).astype(o_ref.dtype)
        lse_tile = jnp.where(lane == h, mx + jnp.log(den), lse_tile)
    lse_ref[...] = lse_tile


def _dilated_group(qkv, bsz, seq, group, n_groups, width, window, dilation):
    n_back = window // dilation
    sub_len = seq // dilation
    assert sub_len % ATT_BLOCK == 0
    nb = sub_len // ATT_BLOCK
    n_seg = 3 * n_groups
    view = qkv.reshape(bsz, sub_len, dilation * n_seg * width)

    def spec(part, prev):
        def index(b, r, n):
            row = jnp.maximum(n - 1, 0) if prev else n
            return (b, row, r * n_seg + part * n_groups + group)
        return pl.BlockSpec((None, ATT_BLOCK, width), index)

    o, lse = pl.pallas_call(
        functools.partial(_dilated_kernel, n_back=n_back, n_heads=width // HEAD_DIM),
        grid=(bsz, dilation, nb),
        in_specs=[spec(0, False), spec(1, False), spec(1, True), spec(2, False), spec(2, True)],
        out_specs=[
            pl.BlockSpec((None, ATT_BLOCK, width), lambda b, r, n: (b, n, r)),
            pl.BlockSpec((None, ATT_BLOCK, LANES), lambda b, r, n: (b, n, r)),
        ],
        out_shape=[
            jax.ShapeDtypeStruct((bsz, sub_len, dilation * width), BF16),
            jax.ShapeDtypeStruct((bsz, sub_len, dilation * LANES), F32),
        ],
        compiler_params=_compiler_params(("parallel", "parallel", "arbitrary")),
    )(view, view, view, view, view)
    return o.reshape(bsz * seq, width), lse.reshape(bsz * seq, LANES)


def _mix_out_kernel(*refs, n_groups, n_heads, row_chunk):
    o_refs = refs[:n_groups]
    lse_refs = refs[n_groups:2 * n_groups]
    w_ref, x_ref, gt_ref, out_ref, lhs_scr = refs[2 * n_groups:]
    tm = x_ref.shape[0]

    def body(r, carry):
        rows = pl.ds(pl.multiple_of(r * row_chunk, row_chunk), row_chunk)
        lses = [ref[rows, :] for ref in lse_refs]
        mx = functools.reduce(jnp.maximum, lses)
        es = [jnp.exp(l - mx) for l in lses]
        inv = 1.0 / functools.reduce(lambda a, b: a + b, es)
        alphas = [e * inv for e in es]
        for h in range(n_heads):
            cols = slice(h * HEAD_DIM, (h + 1) * HEAD_DIM)
            mixed = None
            for g in range(n_groups):
                term = alphas[g][:, h:h + 1] * o_refs[g][rows, cols].astype(F32)
                mixed = term if mixed is None else mixed + term
            lhs_scr[rows, cols] = mixed.astype(BF16)
        return carry

    lax.fori_loop(0, tm // row_chunk, body, 0)
    y = jnp.dot(lhs_scr[...], w_ref[...], preferred_element_type=F32)
    out_ref[...] = x_ref[...] + gt_ref[...] * y


def _mix_out(o_list, lse_list, w, layer, x2, seq, gate):
    m, width = o_list[0].shape
    n = w.shape[-1]
    n_groups = len(o_list)
    tm = _pick_tile(seq, (512, 256, 128))
    tpb = seq // tm
    return pl.pallas_call(
        functools.partial(_mix_out_kernel, n_groups=n_groups, n_heads=width // HEAD_DIM,
                          row_chunk=128),
        grid=(m // tm,),
        in_specs=(
            [pl.BlockSpec((tm, width), lambda i: (i, 0))] * n_groups
            + [pl.BlockSpec((tm, LANES), lambda i: (i, 0))] * n_groups
            + [
                pl.BlockSpec((None, width, n), lambda i: (layer, 0, 0)),
                pl.BlockSpec((tm, n), lambda i: (i, 0)),
                pl.BlockSpec((None, 1, n), lambda i: (i // tpb, 0, 0)),
            ]
        ),
        out_specs=pl.BlockSpec((tm, n), lambda i: (i, 0)),
        out_shape=jax.ShapeDtypeStruct((m, n), F32),
        scratch_shapes=[pltpu.VMEM((tm, width), BF16)],
        compiler_params=_compiler_params(("parallel",)),
    )(*o_list, *lse_list, w, x2, gate)


def _fox_kernel(q_ref, k_ref, v_ref, fq_ref, fk_ref, o_ref, *, blk):
    head = pl.program_id(1)
    qt = pl.program_id(2)
    q = q_ref[...]
    lane = lax.broadcasted_iota(jnp.int32, fq_ref.shape, 1)
    fq = jnp.sum(jnp.where(lane == head, fq_ref[...], 0.0), axis=-1, keepdims=True)
    scale = HEAD_DIM ** -0.5
    contract_last = (((1,), (1,)), ((), ()))

    def step(kb, carry, masked):
        m_run, l_run, acc = carry
        rows = pl.ds(pl.multiple_of(kb * blk, blk), blk)
        s = lax.dot_general(q, k_ref[rows, :], contract_last,
                            preferred_element_type=F32) * scale
        s = s + fq - fk_ref[kb]
        if masked:
            qi = lax.broadcasted_iota(jnp.int32, (blk, blk), 0)
            kj = lax.broadcasted_iota(jnp.int32, (blk, blk), 1)
            s = jnp.where(kj <= qi, s, NEG_INF)
        m_new = jnp.maximum(m_run, jnp.max(s, axis=-1, keepdims=True))
        alpha = jnp.exp(m_run - m_new)
        p = jnp.exp(s - m_new)
        l_new = alpha * l_run + jnp.sum(p, axis=-1, keepdims=True)
        acc_new = alpha * acc + jnp.dot(p.astype(BF16), v_ref[rows, :],
                                        preferred_element_type=F32)
        return m_new, l_new, acc_new

    init = (jnp.full((blk, 1), NEG_INF, F32), jnp.zeros((blk, 1), F32),
            jnp.zeros((blk, HEAD_DIM), F32))
    carry = lax.fori_loop(0, qt, lambda kb, c: step(kb, c, False), init)
    _, l_fin, acc = step(qt, carry, True)
    o_ref[...] = (acc / l_fin).astype(o_ref.dtype)


def _fox_attention(q, kv, f_rows, f_cols, bsz, seq, n_heads):
    blk = _pick_tile(seq, (256, 128))
    nkb = seq // blk
    width = n_heads * HEAD_DIM
    qv = q.reshape(bsz, seq, width)
    kvv = kv.reshape(bsz, seq, 2 * width)
    o = pl.pallas_call(
        functools.partial(_fox_kernel, blk=blk),
        grid=(bsz, n_heads, nkb),
        in_specs=[
            pl.BlockSpec((None, blk, HEAD_DIM), lambda b, h, t: (b, t, h)),
            pl.BlockSpec((None, seq, HEAD_DIM), lambda b, h, t: (b, 0, h)),
            pl.BlockSpec((None, seq, HEAD_DIM), lambda b, h, t: (b, 0, n_heads + h)),
            pl.BlockSpec((None, blk, LANES), lambda b, h, t: (b, t, 0)),
            pl.BlockSpec((None, None, nkb, 1, blk), lambda b, h, t: (b, h, 0, 0, 0)),
        ],
        out_specs=pl.BlockSpec((None, blk, HEAD_DIM), lambda b, h, t: (b, t, h)),
        out_shape=jax.ShapeDtypeStruct((bsz, seq, width), BF16),
        compiler_params=_compiler_params(("parallel", "parallel", "arbitrary")),
    )(qv, kvv, kvv, f_rows.reshape(bsz, seq, LANES), f_cols.reshape(bsz, LANES, nkb, 1, blk))
    return o.reshape(bsz * seq, width)


def _rope_tables(seq):
    half = ROT_DIM // 2
    inv = ROPE_THETA ** (-jnp.arange(0, ROT_DIM, 2, dtype=F32) / ROT_DIM)
    ang = jnp.arange(seq, dtype=F32)[:, None] * inv[None, :]
    cos, sin = jnp.cos(ang), jnp.sin(ang)
    zeros = jnp.zeros((seq, HEAD_DIM - ROT_DIM), F32)
    zh = jnp.zeros((seq, half), F32)
    cos_t = jnp.concatenate([cos, cos, jnp.ones((seq, HEAD_DIM - ROT_DIM), F32)], axis=-1)
    s1_t = jnp.concatenate([-sin, zh, zeros], axis=-1)
    s2_t = jnp.concatenate([zh, sin, zeros], axis=-1)
    return cos_t, s1_t, s2_t


def _split_mods(mods, parts):
    bsz, n = mods.shape
    d = n // parts
    return [mods[:, p * d:(p + 1) * d].reshape(bsz, 1, d) for p in range(parts)]


def kernel(x, c, w_ada, b_ada, g_norm_attn, g_norm_ffn, w_qkv_a, g_qk_a, w_o_a, w_ada_kv, b_ada_kv, g_norm_kv, w_kv, g_k_b, w_f, b_f, w_q_b, g_q_b, w_o_b, w_ffn_in, w_ffn_out):
    bsz, seq, d = x.shape
    depth = w_ada.shape[0]
    n_a = w_qkv_a.shape[0]
    n_groups = g_qk_a.shape[2]
    width_a = w_o_a.shape[1]
    n_heads_b = w_f.shape[1]
    assert n_groups == len(DIL_CONFIGS) and n_heads_b <= LANES

    x2 = x.reshape(bsz * seq, d)
    rope_tables = _rope_tables(seq)

    w_qkv_a = w_qkv_a.astype(BF16)
    w_o_a = w_o_a.astype(BF16)
    w_kv = w_kv.astype(BF16)[None]
    w_q_b = w_q_b.astype(BF16)
    w_o_b = w_o_b.astype(BF16)
    w_ffn_in = w_ffn_in.astype(BF16)
    w_ffn_out = w_ffn_out.astype(BF16)
    wf_pad = jnp.pad(w_f, ((0, 0), (0, LANES - n_heads_b))).astype(BF16)
    bf_pad = jnp.pad(b_f, (0, LANES - n_heads_b)).reshape(1, LANES)

    kv = f_rows = f_cols = None
    for layer in range(depth):
        sh_a, sc_a, gt_a, sh_f, sc_f, gt_f = _split_mods(_mods(c, w_ada, b_ada, layer), 6)
        if layer < n_a:
            gains = g_qk_a[layer].reshape(2 * n_groups, 1, HEAD_DIM)
            qkv = _proj(x2, seq, g_norm_attn[layer], sh_a, sc_a, w_qkv_a, layer, gains,
                        n_norm_tiles=2 * n_groups, rope_tables=rope_tables)
            outs = [_dilated_group(qkv, bsz, seq, g, n_groups, width_a, window, dilation)
                    for g, (window, dilation) in enumerate(DIL_CONFIGS)]
            x2 = _mix_out([o for o, _ in outs], [l for _, l in outs], w_o_a, layer, x2, seq, gt_a)
        else:
            i = layer - n_a
            q = _proj(x2, seq, g_norm_attn[layer], sh_a, sc_a, w_q_b, i,
                      g_q_b[i].reshape(1, 1, HEAD_DIM), n_norm_tiles=w_q_b.shape[-1] // 1024)
            o = _fox_attention(q, kv, f_rows, f_cols, bsz, seq, n_heads_b)
            x2 = _mm_residual(o, w_o_b, i, x2, seq, gt_a)
        a = _swiglu_in(x2, seq, g_norm_ffn[layer], sh_f, sc_f, w_ffn_in, layer)
        x2 = _mm_residual(a, w_ffn_out, layer, x2, seq, gt_f)
        if layer == n_a - 1:
            sh_kv, sc_kv = _split_mods(_mods(c, w_ada_kv[None], b_ada_kv[None], 0), 2)
            kv, f_rows, f_cols = _proj(
                x2, seq, g_norm_kv, sh_kv, sc_kv, w_kv, 0, g_k_b.reshape(1, 1, HEAD_DIM),
                n_norm_tiles=w_kv.shape[-1] // 2 // 1024, forget=(wf_pad, bf_pad))
    return x2.reshape(bsz, seq, d)
```

```python
import functools
import math

import numpy as np
import jax
import jax.numpy as jnp
from jax import lax
from jax.experimental import pallas as pl
from jax.experimental.pallas import tpu as pltpu

HEAD_DIM = 128
DIL_CONFIGS = ((128, 1), (512, 4), (2048, 16))
ROT_DIM = HEAD_DIM // 4
ROPE_THETA = 500000.0
ATT_BLOCK = 128
EPS = 1e-6
NEG_INF = -1e30
LOG2E = math.log2(math.e)
LN2 = math.log(2.0)

LANES = 128
BF16_SUBLANES = 16
VMEM_LIMIT_BYTES = 56 * 1024 * 1024

PROJ_ROWS = 1024
PROJ_COLS = 1024
FILL_ROWS = 128
DEINT_ROWS = 512
EPILOGUE_ROWS = 512
FOX_BLOCK = 512

F32 = jnp.float32
BF16 = jnp.bfloat16


def _compiler_params(semantics):
    return pltpu.CompilerParams(dimension_semantics=semantics, vmem_limit_bytes=VMEM_LIMIT_BYTES)


def _pick_tile(n, candidates):
    for t in candidates:
        if n % t == 0:
            return t
    raise ValueError(f"no tile in {candidates} divides {n}")


def _head_rmsnorm(a, gain):
    return a * lax.rsqrt(jnp.mean(a * a, axis=-1, keepdims=True) + EPS) * gain


def _silu(v):
    return v * jax.nn.sigmoid(v)


def _mods_kernel(c_ref, w_ref, b_ref, o_ref):
    c_act = _silu(c_ref[...])
    o_ref[...] = jnp.dot(c_act.astype(BF16), w_ref[...].astype(BF16),
                         preferred_element_type=F32) + b_ref[...]


def _mods(c, w, b, layer):
    bsz, d = c.shape
    n = w.shape[-1]
    tn = _pick_tile(n, (512, 256, 128))
    return pl.pallas_call(
        _mods_kernel,
        grid=(n // tn,),
        in_specs=[
            pl.BlockSpec((bsz, d), lambda j: (0, 0)),
            pl.BlockSpec((None, d, tn), lambda j: (layer, 0, j)),
            pl.BlockSpec((None, 1, tn), lambda j: (layer, 0, j)),
        ],
        out_specs=pl.BlockSpec((bsz, tn), lambda j: (0, j)),
        out_shape=jax.ShapeDtypeStruct((bsz, n), F32),
        compiler_params=_compiler_params(("parallel",)),
        name="mods",
    )(c, w, b.reshape(b.shape[0], 1, n))


def _normmod_rows(x_ref, rows, gain, mul, shift):
    xf = x_ref[rows, :]
    y = xf * lax.rsqrt(jnp.mean(xf * xf, axis=-1, keepdims=True) + EPS) * gain
    return y * mul + shift


def _fill_lhs(x_ref, gn_ref, sh_ref, sc_ref, h_scr):
    tm = x_ref.shape[0]
    gain = gn_ref[...]
    mul = 1.0 + sc_ref[...]
    shift = sh_ref[...]

    def body(it, carry):
        rows = pl.ds(pl.multiple_of(it * FILL_ROWS, FILL_ROWS), FILL_ROWS)
        h_scr[rows, :] = _normmod_rows(x_ref, rows, gain, mul, shift).astype(BF16)
        return carry

    lax.fori_loop(0, tm // FILL_ROWS, body, 0)


def _deint_lhs_kernel(x_ref, gn_ref, sh_ref, sc_ref, o_ref, slab_scr, *, dilations):
    tm, d_model = x_ref.shape
    n_slabs = d_model // LANES
    gain = gn_ref[...]
    mul = 1.0 + sc_ref[...]
    shift = sh_ref[...]

    def natural(it, carry):
        rows = pl.ds(pl.multiple_of(it * FILL_ROWS, FILL_ROWS), FILL_ROWS)
        h = _normmod_rows(x_ref, rows, gain, mul, shift)
        for v, d in enumerate(dilations):
            if d == 1:
                o_ref[v, rows, :] = h.astype(BF16)
        for s in range(n_slabs):
            slab_scr[s, rows, :] = h[:, s * LANES:(s + 1) * LANES]
        return carry

    lax.fori_loop(0, tm // FILL_ROWS, natural, 0)

    for v, d in enumerate(dilations):
        if d == 1:
            continue
        per_residue = tm // d
        chunk = min(per_residue, FILL_ROWS)
        chunks_per_residue = per_residue // chunk

        def gather(it, carry, v=v, d=d, per_residue=per_residue, chunk=chunk,
                   chunks_per_residue=chunks_per_residue):
            r = it // chunks_per_residue
            l0 = (it % chunks_per_residue) * chunk
            src = pl.ds(r + l0 * d, chunk, stride=d)
            dst = pl.ds(pl.multiple_of(r * per_residue + l0, chunk), chunk)
            for s in range(n_slabs):
                o_ref[v, dst, s * LANES:(s + 1) * LANES] = slab_scr[s, src, :].astype(BF16)
            return carry

        lax.fori_loop(0, tm // chunk, gather, 0)


def _deint_lhs(x2, seq, gnorm, shift, scale, dilations):
    m, d = x2.shape
    tm = DEINT_ROWS
    assert seq % tm == 0 and all(tm % (dl * BF16_SUBLANES) == 0 for dl in dilations)
    tpb = seq // tm
    n_var = len(dilations)
    return pl.pallas_call(
        functools.partial(_deint_lhs_kernel, dilations=dilations),
        grid=(m // tm,),
        in_specs=[
            pl.BlockSpec((tm, d), lambda i: (i, 0)),
            pl.BlockSpec((1, d), lambda i: (0, 0)),
            pl.BlockSpec((None, 1, d), lambda i: (i // tpb, 0, 0)),
            pl.BlockSpec((None, 1, d), lambda i: (i // tpb, 0, 0)),
        ],
        out_specs=pl.BlockSpec((n_var, tm, d), lambda i: (0, i, 0)),
        out_shape=jax.ShapeDtypeStruct((n_var, m, d), BF16),
        scratch_shapes=[pltpu.VMEM((d // LANES, tm, LANES), F32)],
        compiler_params=_compiler_params(("parallel",)),
        name="deint_lhs",
    )(x2, gnorm.reshape(1, d), shift, scale)


def _head_epilogue(acc_scr, o_ref, gain, cos_ref=None, sin_ref=None):
    tm, tn = acc_scr.shape

    def body(r, carry):
        rows = pl.ds(pl.multiple_of(r * EPILOGUE_ROWS, EPILOGUE_ROWS), EPILOGUE_ROWS)
        if cos_ref is not None:
            cos = cos_ref[rows, :]
            sin = sin_ref[rows, :]
        for h in range(tn // HEAD_DIM):
            cols = slice(h * HEAD_DIM, (h + 1) * HEAD_DIM)
            y = _head_rmsnorm(acc_scr[rows, cols], gain)
            if cos_ref is not None:
                y = y * cos + pltpu.roll(y, HEAD_DIM // 2, 1) * sin
            o_ref[rows, cols] = y.astype(o_ref.dtype)
        return carry

    lax.fori_loop(0, tm // EPILOGUE_ROWS, body, 0)


def _qkv_kernel(h_ref, w_ref, gain_ref, cos_ref, sin_ref, o_ref, acc_scr, *, n_parts):
    part = pl.program_id(1) % n_parts
    acc_scr[...] = jnp.dot(h_ref[...], w_ref[...], preferred_element_type=F32)

    @pl.when(part < 2)
    def _():
        _head_epilogue(acc_scr, o_ref, gain_ref[...], cos_ref, sin_ref)

    @pl.when(part >= 2)
    def _():
        o_ref[...] = acc_scr[...].astype(o_ref.dtype)


def _qkv_proj(h_var, seq, w, layer, gains, rope_tables):
    n_groups, m, d = h_var.shape
    n = w.shape[-1]
    n_parts = 3
    tm = PROJ_ROWS
    tn = PROJ_COLS
    assert seq % tm == 0 and n == n_parts * n_groups * tn
    tpb = seq // tm

    def col(j):
        return (j % n_parts) * n_groups + j // n_parts

    return pl.pallas_call(
        functools.partial(_qkv_kernel, n_parts=n_parts),
        grid=(m // tm, n_parts * n_groups),
        in_specs=[
            pl.BlockSpec((None, tm, d), lambda i, j: (j // n_parts, i, 0)),
            pl.BlockSpec((None, d, tn), lambda i, j: (layer, 0, col(j))),
            pl.BlockSpec((None, 1, HEAD_DIM),
                         lambda i, j: (jnp.minimum(col(j), 2 * n_groups - 1), 0, 0)),
            pl.BlockSpec((None, tm, HEAD_DIM), lambda i, j: (j // n_parts, i % tpb, 0)),
            pl.BlockSpec((None, tm, HEAD_DIM), lambda i, j: (j // n_parts, i % tpb, 0)),
        ],
        out_specs=pl.BlockSpec((tm, tn), lambda i, j: (i, col(j))),
        out_shape=jax.ShapeDtypeStruct((m, n), BF16),
        scratch_shapes=[pltpu.VMEM((tm, tn), F32)],
        compiler_params=_compiler_params(("parallel", "arbitrary")),
        name="qkv_proj",
    )(h_var, w, gains, *rope_tables)


def _proj_kernel(*refs, n_norm_tiles, forget, tiles_per_batch):
    x_ref, gn_ref, sh_ref, sc_ref, w_ref, gain_ref = refs[:6]
    pos = 6
    if forget:
        wf_ref, bf_ref, tri_ref = refs[pos:pos + 3]
        pos += 3
    o_ref = refs[pos]
    pos += 1
    if forget:
        f_ref, ft_ref = refs[pos:pos + 2]
        pos += 2
    h_scr, acc_scr = refs[pos:pos + 2]
    pos += 2
    if forget:
        carry_scr = refs[pos]

    i = pl.program_id(0)
    j = pl.program_id(1)
    tm = acc_scr.shape[0]

    @pl.when(j == 0)
    def _():
        _fill_lhs(x_ref, gn_ref, sh_ref, sc_ref, h_scr)
        if forget:
            z = jnp.dot(h_scr[...], wf_ref[...], preferred_element_type=F32) + bf_ref[...]
            log_f = jnp.minimum(z, 0.0) - jnp.log1p(jnp.exp(-jnp.abs(z)))
            hi = log_f.astype(BF16)
            rem = log_f - hi.astype(F32)
            mid = rem.astype(BF16)
            lo = (rem - mid.astype(F32)).astype(BF16)
            tri = tri_ref[...]
            cum = (jnp.dot(tri, hi, preferred_element_type=F32)
                   + jnp.dot(tri, mid, preferred_element_type=F32)
                   + jnp.dot(tri, lo, preferred_element_type=F32))

            @pl.when(i % tiles_per_batch == 0)
            def _():
                carry_scr[...] = jnp.zeros_like(carry_scr)

            cum = cum + carry_scr[...]
            carry_scr[...] = cum[tm - 1:tm, :]
            f_ref[...] = cum
            ft_ref[...] = cum.T

    acc_scr[...] = jnp.dot(h_scr[...], w_ref[...], preferred_element_type=F32)

    @pl.when(j < n_norm_tiles)
    def _():
        _head_epilogue(acc_scr, o_ref, gain_ref[...])

    @pl.when(j >= n_norm_tiles)
    def _():
        o_ref[...] = acc_scr[...].astype(o_ref.dtype)


def _proj(x2, seq, gnorm, shift, scale, w, layer, gain, n_norm_tiles, forget=None):
    m, d = x2.shape
    n = w.shape[-1]
    tm = PROJ_ROWS
    tn = PROJ_COLS
    assert seq % tm == 0 and n % tn == 0
    tpb = seq // tm
    with_forget = forget is not None

    in_specs = [
        pl.BlockSpec((tm, d), lambda i, j: (i, 0)),
        pl.BlockSpec((1, d), lambda i, j: (0, 0)),
        pl.BlockSpec((None, 1, d), lambda i, j: (i // tpb, 0, 0)),
        pl.BlockSpec((None, 1, d), lambda i, j: (i // tpb, 0, 0)),
        pl.BlockSpec((None, d, tn), lambda i, j: (layer, 0, j)),
        pl.BlockSpec((1, HEAD_DIM), lambda i, j: (0, 0)),
    ]
    args = [x2, gnorm.reshape(1, d), shift, scale, w, gain]
    out_specs = [pl.BlockSpec((tm, tn), lambda i, j: (i, j))]
    out_shape = [jax.ShapeDtypeStruct((m, n), BF16)]
    scratch = [pltpu.VMEM((tm, d), BF16), pltpu.VMEM((tm, tn), F32)]
    if with_forget:
        wf_pad, bf_pad = forget
        tri = (lax.broadcasted_iota(jnp.int32, (tm, tm), 0)
               >= lax.broadcasted_iota(jnp.int32, (tm, tm), 1)).astype(BF16)
        in_specs += [
            pl.BlockSpec((d, LANES), lambda i, j: (0, 0)),
            pl.BlockSpec((1, LANES), lambda i, j: (0, 0)),
            pl.BlockSpec((tm, tm), lambda i, j: (0, 0)),
        ]
        args += [wf_pad, bf_pad, tri]
        out_specs += [
            pl.BlockSpec((tm, LANES), lambda i, j: (i, 0)),
            pl.BlockSpec((None, LANES, tm), lambda i, j: (i // tpb, 0, i % tpb)),
        ]
        out_shape += [
            jax.ShapeDtypeStruct((m, LANES), F32),
            jax.ShapeDtypeStruct((m // seq, LANES, seq), F32),
        ]
        scratch.append(pltpu.VMEM((1, LANES), F32))

    kern = functools.partial(_proj_kernel, n_norm_tiles=n_norm_tiles, forget=with_forget,
                             tiles_per_batch=tpb)
    outs = pl.pallas_call(
        kern,
        grid=(m // tm, n // tn),
        in_specs=in_specs,
        out_specs=out_specs,
        out_shape=out_shape,
        scratch_shapes=scratch,
        compiler_params=_compiler_params(("arbitrary", "arbitrary")),
        name="proj",
    )(*args)
    return outs if with_forget else outs[0]


def _swiglu_kernel(x_ref, gn_ref, sh_ref, sc_ref, wg_ref, wu_ref, o_ref, h_scr):
    @pl.when(pl.program_id(1) == 0)
    def _():
        _fill_lhs(x_ref, gn_ref, sh_ref, sc_ref, h_scr)

    h = h_scr[...]
    g = jnp.dot(h, wg_ref[...], preferred_element_type=F32)
    u = jnp.dot(h, wu_ref[...], preferred_element_type=F32)
    o_ref[...] = (_silu(g) * u).astype(o_ref.dtype)


def _swiglu_in(x2, seq, gnorm, shift, scale, w, layer):
    m, d = x2.shape
    f = w.shape[-1] // 2
    tm = PROJ_ROWS
    tf = _pick_tile(f, (512, 256, 128))
    assert seq % tm == 0
    tpb = seq // tm
    nf = f // tf
    return pl.pallas_call(
        _swiglu_kernel,
        grid=(m // tm, nf),
        in_specs=[
            pl.BlockSpec((tm, d), lambda i, j: (i, 0)),
            pl.BlockSpec((1, d), lambda i, j: (0, 0)),
            pl.BlockSpec((None, 1, d), lambda i, j: (i // tpb, 0, 0)),
            pl.BlockSpec((None, 1, d), lambda i, j: (i // tpb, 0, 0)),
            pl.BlockSpec((None, d, tf), lambda i, j: (layer, 0, j)),
            pl.BlockSpec((None, d, tf), lambda i, j: (layer, 0, nf + j)),
        ],
        out_specs=pl.BlockSpec((tm, tf), lambda i, j: (i, j)),
        out_shape=jax.ShapeDtypeStruct((m, f), BF16),
        scratch_shapes=[pltpu.VMEM((tm, d), BF16)],
        compiler_params=_compiler_params(("parallel", "arbitrary")),
        name="swiglu_in",
    )(x2, gnorm.reshape(1, d), shift, scale, w, w)


def _mmres_kernel(a_ref, w_ref, x_ref, gt_ref, o_ref):
    y = jnp.dot(a_ref[...], w_ref[...], preferred_element_type=F32)
    o_ref[...] = x_ref[...] + gt_ref[...] * y


def _mm_residual(a, w, layer, x2, seq, gate):
    m, k = a.shape
    n = w.shape[-1]
    tm = PROJ_ROWS
    tn = _pick_tile(n, (512, 256, 128))
    assert seq % tm == 0
    tpb = seq // tm
    return pl.pallas_call(
        _mmres_kernel,
        grid=(m // tm, n // tn),
        in_specs=[
            pl.BlockSpec((tm, k), lambda i, j: (i, 0)),
            pl.BlockSpec((None, k, tn), lambda i, j: (layer, 0, j)),
            pl.BlockSpec((tm, tn), lambda i, j: (i, j)),
            pl.BlockSpec((None, 1, tn), lambda i, j: (i // tpb, 0, j)),
        ],
        out_specs=pl.BlockSpec((tm, tn), lambda i, j: (i, j)),
        out_shape=jax.ShapeDtypeStruct((m, n), F32),
        compiler_params=_compiler_params(("parallel", "parallel")),
        name="mm_residual",
    )(a, w, x2, gate)


def _dilated_kernel(q_ref, kc_ref, kp_ref, vc_ref, vp_ref, o_ref, lse_ref, o_scr, *,
                    n_back, n_heads, dilation):
    blk = pl.program_id(1)
    res = pl.program_id(2)
    width = n_heads * HEAD_DIM
    qi = lax.broadcasted_iota(jnp.int32, (ATT_BLOCK, 2 * ATT_BLOCK), 0)
    kj = lax.broadcasted_iota(jnp.int32, (ATT_BLOCK, 2 * ATT_BLOCK), 1)
    dist = qi + ATT_BLOCK - kj
    mask = (dist >= 0) & (dist <= n_back) & ((kj >= ATT_BLOCK) | (blk > 0))
    lane = lax.broadcasted_iota(jnp.int32, (ATT_BLOCK, LANES), 1)
    scale2 = HEAD_DIM ** -0.5 * LOG2E
    contract_last = (((1,), (1,)), ((), ()))

    q_all = q_ref[...].reshape(ATT_BLOCK, width)
    k_all = jnp.concatenate([kp_ref[...].reshape(ATT_BLOCK, width),
                             kc_ref[...].reshape(ATT_BLOCK, width)], axis=0)
    v_all = jnp.concatenate([vp_ref[...].reshape(ATT_BLOCK, width),
                             vc_ref[...].reshape(ATT_BLOCK, width)], axis=0)
    if dilation == 1:
        rows = slice(None)
    else:
        rows = pl.ds(res, ATT_BLOCK, stride=dilation)

    lse_tile = jnp.zeros((ATT_BLOCK, LANES), F32)
    for h in range(n_heads):
        cols = slice(h * HEAD_DIM, (h + 1) * HEAD_DIM)
        s2 = lax.dot_general(q_all[:, cols], k_all[:, cols], contract_last,
                             preferred_element_type=F32) * scale2
        s2 = jnp.where(mask, s2, NEG_INF)
        mx2 = jnp.max(s2, axis=-1, keepdims=True)
        p = jnp.exp2(s2 - mx2)
        den = jnp.sum(p, axis=-1, keepdims=True)
        o = jnp.dot(p.astype(BF16), v_all[:, cols], preferred_element_type=F32)
        o_scr[h, rows, :] = o / den
        lse_tile = jnp.where(lane == h, mx2 * LN2 + jnp.log(den), lse_tile)
    lse_ref[rows, :] = lse_tile

    @pl.when(res == dilation - 1)
    def _():
        for h in range(n_heads):
            o_ref[:, h * HEAD_DIM:(h + 1) * HEAD_DIM] = o_scr[h].astype(o_ref.dtype)


def _dilated_group(qkv, bsz, seq, group, n_groups, width, window, dilation):
    n_back = window // dilation
    sub_len = seq // dilation
    assert sub_len % ATT_BLOCK == 0
    nb = sub_len // ATT_BLOCK
    n_cols = qkv.shape[1]
    tpb = seq // DEINT_ROWS
    per_residue = DEINT_ROWS // dilation

    if per_residue >= ATT_BLOCK:
        bpt = per_residue // ATT_BLOCK
        view = qkv.reshape(bsz, tpb, dilation, bpt, ATT_BLOCK, n_cols)
        block = (None, None, None, None, ATT_BLOCK, width)

        def rows_index(b, n, r):
            return (b, n // bpt, r, n % bpt, 0)
    else:
        pieces = ATT_BLOCK // per_residue
        view = qkv.reshape(bsz, tpb, dilation, per_residue, n_cols)
        block = (None, pieces, None, per_residue, width)

        def rows_index(b, n, r):
            return (b, n, r, 0)

    def spec(part, prev):
        def index(b, n, r):
            nn = jnp.maximum(n - 1, 0) if prev else n
            return rows_index(b, nn, r) + (part * n_groups + group,)
        return pl.BlockSpec(block, index)

    span = ATT_BLOCK * dilation
    o, lse = pl.pallas_call(
        functools.partial(_dilated_kernel, n_back=n_back, n_heads=width // HEAD_DIM,
                          dilation=dilation),
        grid=(bsz, nb, dilation),
        in_specs=[spec(0, False), spec(1, False), spec(1, True), spec(2, False), spec(2, True)],
        out_specs=[
            pl.BlockSpec((None, span, width), lambda b, n, r: (b, n, 0)),
            pl.BlockSpec((None, span, LANES), lambda b, n, r: (b, n, 0)),
        ],
        out_shape=[
            jax.ShapeDtypeStruct((bsz, seq, width), BF16),
            jax.ShapeDtypeStruct((bsz, seq, LANES), F32),
        ],
        scratch_shapes=[pltpu.VMEM((width // HEAD_DIM, span, HEAD_DIM), F32)],
        compiler_params=_compiler_params(("parallel", "arbitrary", "arbitrary")),
        name="dilated_attention",
    )(view, view, view, view, view)
    return o.reshape(bsz * seq, width), lse.reshape(bsz * seq, LANES)


def _mix_out_kernel(*refs, n_groups, n_heads, row_chunk):
    o_refs = refs[:n_groups]
    lse_refs = refs[n_groups:2 * n_groups]
    w_ref, x_ref, gt_ref, out_ref, lhs_scr = refs[2 * n_groups:]
    tm = x_ref.shape[0]

    def body(r, carry):
        rows = pl.ds(pl.multiple_of(r * row_chunk, row_chunk), row_chunk)
        lses = [ref[rows, :] for ref in lse_refs]
        mx = functools.reduce(jnp.maximum, lses)
        es = [jnp.exp(l - mx) for l in lses]
        inv = 1.0 / functools.reduce(lambda a, b: a + b, es)
        alphas = [e * inv for e in es]
        for h in range(n_heads):
            cols = slice(h * HEAD_DIM, (h + 1) * HEAD_DIM)
            mixed = None
            for g in range(n_groups):
                term = alphas[g][:, h:h + 1] * o_refs[g][rows, cols].astype(F32)
                mixed = term if mixed is None else mixed + term
            lhs_scr[rows, cols] = mixed.astype(BF16)
        return carry

    lax.fori_loop(0, tm // row_chunk, body, 0)
    y = jnp.dot(lhs_scr[...], w_ref[...], preferred_element_type=F32)
    out_ref[...] = x_ref[...] + gt_ref[...] * y


def _mix_out(o_list, lse_list, w, layer, x2, seq, gate):
    m, width = o_list[0].shape
    n = w.shape[-1]
    n_groups = len(o_list)
    tm = _pick_tile(seq, (512, 256, 128))
    tpb = seq // tm
    return pl.pallas_call(
        functools.partial(_mix_out_kernel, n_groups=n_groups, n_heads=width // HEAD_DIM,
                          row_chunk=128),
        grid=(m // tm,),
        in_specs=(
            [pl.BlockSpec((tm, width), lambda i: (i, 0))] * n_groups
            + [pl.BlockSpec((tm, LANES), lambda i: (i, 0))] * n_groups
            + [
                pl.BlockSpec((None, width, n), lambda i: (layer, 0, 0)),
                pl.BlockSpec((tm, n), lambda i: (i, 0)),
                pl.BlockSpec((None, 1, n), lambda i: (i // tpb, 0, 0)),
            ]
        ),
        out_specs=pl.BlockSpec((tm, n), lambda i: (i, 0)),
        out_shape=jax.ShapeDtypeStruct((m, n), F32),
        scratch_shapes=[pltpu.VMEM((tm, width), BF16)],
        compiler_params=_compiler_params(("parallel",)),
        name="mix_out",
    )(*o_list, *lse_list, w, x2, gate)


def _fox_kernel(q_ref, k_ref, v_ref, fq_ref, fk_ref, o_ref, *, blk):
    head = pl.program_id(1)
    qt = pl.program_id(2)
    q = q_ref[...]
    lane = lax.broadcasted_iota(jnp.int32, fq_ref.shape, 1)
    fq = jnp.sum(jnp.where(lane == head, fq_ref[...], 0.0), axis=-1, keepdims=True)
    fq2 = fq * LOG2E
    scale2 = HEAD_DIM ** -0.5 * LOG2E
    contract_last = (((1,), (1,)), ((), ()))

    def step(kb, carry, masked):
        m_run, l_run, acc = carry
        rows = pl.ds(pl.multiple_of(kb * blk, blk), blk)
        s2 = lax.dot_general(q, k_ref[rows, :], contract_last,
                             preferred_element_type=F32) * scale2
        s2 = s2 + fq2 - fk_ref[kb] * LOG2E
        if masked:
            qi = lax.broadcasted_iota(jnp.int32, (blk, blk), 0)
            kj = lax.broadcasted_iota(jnp.int32, (blk, blk), 1)
            s2 = jnp.where(kj <= qi, s2, NEG_INF)
        m_new = jnp.maximum(m_run, jnp.max(s2, axis=-1, keepdims=True))
        alpha = jnp.exp2(m_run - m_new)
        p = jnp.exp2(s2 - m_new)
        l_new = alpha * l_run + jnp.sum(p, axis=-1, keepdims=True)
        acc_new = alpha * acc + jnp.dot(p.astype(BF16), v_ref[rows, :],
                                        preferred_element_type=F32)
        return m_new, l_new, acc_new

    init = (jnp.full((blk, 1), NEG_INF, F32), jnp.zeros((blk, 1), F32),
            jnp.zeros((blk, HEAD_DIM), F32))
    carry = lax.fori_loop(0, qt, lambda kb, c: step(kb, c, False), init)
    _, l_fin, acc = step(qt, carry, True)
    o_ref[...] = (acc / l_fin).astype(o_ref.dtype)


def _fox_attention(q, kv, f_rows, f_cols, bsz, seq, n_heads):
    blk = _pick_tile(seq, (FOX_BLOCK, 256, 128))
    nkb = seq // blk
    width = n_heads * HEAD_DIM
    qv = q.reshape(bsz, seq, width)
    kvv = kv.reshape(bsz, seq, 2 * width)
    o = pl.pallas_call(
        functools.partial(_fox_kernel, blk=blk),
        grid=(bsz, n_heads, nkb),
        in_specs=[
            pl.BlockSpec((None, blk, HEAD_DIM), lambda b, h, t: (b, t, h)),
            pl.BlockSpec((None, seq, HEAD_DIM), lambda b, h, t: (b, 0, h)),
            pl.BlockSpec((None, seq, HEAD_DIM), lambda b, h, t: (b, 0, n_heads + h)),
            pl.BlockSpec((None, blk, LANES), lambda b, h, t: (b, t, 0)),
            pl.BlockSpec((None, None, nkb, 1, blk), lambda b, h, t: (b, h, 0, 0, 0)),
        ],
        out_specs=pl.BlockSpec((None, blk, HEAD_DIM), lambda b, h, t: (b, t, h)),
        out_shape=jax.ShapeDtypeStruct((bsz, seq, width), BF16),
        compiler_params=_compiler_params(("parallel", "parallel", "arbitrary")),
        name="fox_attention",
    )(qv, kvv, kvv, f_rows.reshape(bsz, seq, LANES), f_cols.reshape(bsz, LANES, nkb, 1, blk))
    return o.reshape(bsz * seq, width)


def _rotary_lane_order():
    half = ROT_DIM // 2
    mid = HEAD_DIM // 2
    return np.concatenate([np.arange(0, half), np.arange(ROT_DIM, mid + half),
                           np.arange(half, ROT_DIM), np.arange(mid + half, HEAD_DIM)])


def _rope_tables(seq, dilations):
    half = ROT_DIM // 2
    mid = HEAD_DIM // 2
    inv = ROPE_THETA ** (-jnp.arange(0, ROT_DIM, 2, dtype=F32) / ROT_DIM)
    ang = jnp.arange(seq, dtype=F32)[:, None] * inv[None, :]
    cos, sin = jnp.cos(ang), jnp.sin(ang)
    ones = jnp.ones((seq, mid - half), F32)
    zeros = jnp.zeros((seq, mid - half), F32)
    cos_t = jnp.concatenate([cos, ones, cos, ones], axis=-1)
    sin_t = jnp.concatenate([-sin, zeros, sin, zeros], axis=-1)

    def deinterleave(t, d):
        t = t.reshape(seq // DEINT_ROWS, DEINT_ROWS // d, d, HEAD_DIM)
        return jnp.swapaxes(t, 1, 2).reshape(seq, HEAD_DIM)

    return (jnp.stack([deinterleave(cos_t, d) for d in dilations]),
            jnp.stack([deinterleave(sin_t, d) for d in dilations]))


def _split_mods(mods, parts):
    bsz, n = mods.shape
    d = n // parts
    return [mods[:, p * d:(p + 1) * d].reshape(bsz, 1, d) for p in range(parts)]


def kernel(x, c, w_ada, b_ada, g_norm_attn, g_norm_ffn, w_qkv_a, g_qk_a, w_o_a, w_ada_kv, b_ada_kv, g_norm_kv, w_kv, g_k_b, w_f, b_f, w_q_b, g_q_b, w_o_b, w_ffn_in, w_ffn_out):
    bsz, seq, d = x.shape
    depth = w_ada.shape[0]
    n_a = w_qkv_a.shape[0]
    n_groups = g_qk_a.shape[2]
    width_a = w_o_a.shape[1]
    n_heads_b = w_f.shape[1]
    assert n_groups == len(DIL_CONFIGS) and n_heads_b <= LANES and width_a == PROJ_COLS
    dilations = tuple(dl for _, dl in DIL_CONFIGS)

    x2 = x.reshape(bsz * seq, d)
    rope_tables = _rope_tables(seq, dilations)
    lane_order = _rotary_lane_order()

    w_qkv = w_qkv_a.reshape(n_a, d, 3, n_groups * width_a // HEAD_DIM, HEAD_DIM)
    w_qkv = jnp.concatenate([w_qkv[:, :, :2][..., lane_order], w_qkv[:, :, 2:]], axis=2)
    w_qkv = w_qkv.reshape(w_qkv_a.shape).astype(BF16)
    g_qk = g_qk_a[..., lane_order]
    w_o_a = w_o_a.astype(BF16)
    w_kv = w_kv.astype(BF16)[None]
    w_q_b = w_q_b.astype(BF16)
    w_o_b = w_o_b.astype(BF16)
    w_ffn_in = w_ffn_in.astype(BF16)
    w_ffn_out = w_ffn_out.astype(BF16)
    wf_pad = jnp.pad(w_f, ((0, 0), (0, LANES - n_heads_b))).astype(BF16)
    bf_pad = jnp.pad(b_f, (0, LANES - n_heads_b)).reshape(1, LANES)

    kv = f_rows = f_cols = None
    for layer in range(depth):
        sh_a, sc_a, gt_a, sh_f, sc_f, gt_f = _split_mods(_mods(c, w_ada, b_ada, layer), 6)
        if layer < n_a:
            gains = g_qk[layer].reshape(2 * n_groups, 1, HEAD_DIM)
            h_var = _deint_lhs(x2, seq, g_norm_attn[layer], sh_a, sc_a, dilations)
            qkv = _qkv_proj(h_var, seq, w_qkv, layer, gains, rope_tables)
            outs = [_dilated_group(qkv, bsz, seq, g, n_groups, width_a, window, dilation)
                    for g, (window, dilation) in enumerate(DIL_CONFIGS)]
            x2 = _mix_out([o for o, _ in outs], [l for _, l in outs], w_o_a, layer, x2, seq, gt_a)
        else:
            i = layer - n_a
            q = _proj(x2, seq, g_norm_attn[layer], sh_a, sc_a, w_q_b, i,
                      g_q_b[i].reshape(1, HEAD_DIM),
                      n_norm_tiles=w_q_b.shape[-1] // PROJ_COLS)
            o = _fox_attention(q, kv, f_rows, f_cols, bsz, seq, n_heads_b)
            x2 = _mm_residual(o, w_o_b, i, x2, seq, gt_a)
        a = _swiglu_in(x2, seq, g_norm_ffn[layer], sh_f, sc_f, w_ffn_in, layer)
        x2 = _mm_residual(a, w_ffn_out, layer, x2, seq, gt_f)
        if layer == n_a - 1:
            sh_kv, sc_kv = _split_mods(_mods(c, w_ada_kv[None], b_ada_kv[None], 0), 2)
            kv, f_rows, f_cols = _proj(
                x2, seq, g_norm_kv, sh_kv, sc_kv, w_kv, 0, g_k_b.reshape(1, HEAD_DIM),
                n_norm_tiles=w_kv.shape[-1] // 2 // PROJ_COLS, forget=(wf_pad, bf_pad))
    return x2.reshape(bsz, seq, d)
```

```python
import functools
import math

import numpy as np
import jax
import jax.numpy as jnp
from jax import lax
from jax.experimental import pallas as pl
from jax.experimental.pallas import tpu as pltpu

HEAD_DIM = 128
DIL_CONFIGS = ((128, 1), (512, 4), (2048, 16))
ROT_DIM = HEAD_DIM // 4
ROPE_THETA = 500000.0
ATT_BLOCK = 128
EPS = 1e-6
NEG_INF = -1e30
LOG2E = math.log2(math.e)
LN2 = math.log(2.0)

LANES = 128
BF16_SUBLANES = 16
VMEM_LIMIT_BYTES = 56 * 1024 * 1024

PROJ_ROWS = 1024
PROJ_COLS = 1024
FILL_ROWS = 128
DEINT_ROWS = 512
EPILOGUE_ROWS = 512
FOX_BLOCK = 512
FOX_HEADS_PER_STEP = 4

F32 = jnp.float32
BF16 = jnp.bfloat16


def _compiler_params(semantics):
    return pltpu.CompilerParams(dimension_semantics=semantics, vmem_limit_bytes=VMEM_LIMIT_BYTES)


def _pick_tile(n, candidates):
    for t in candidates:
        if n % t == 0:
            return t
    raise ValueError(f"no tile in {candidates} divides {n}")


def _head_rmsnorm(a, gain):
    return a * lax.rsqrt(jnp.mean(a * a, axis=-1, keepdims=True) + EPS) * gain


def _silu(v):
    return v * jax.nn.sigmoid(v)


def _mods_kernel(c_ref, w_ref, b_ref, o_ref):
    c_act = _silu(c_ref[...])
    o_ref[...] = jnp.dot(c_act.astype(BF16), w_ref[...].astype(BF16),
                         preferred_element_type=F32) + b_ref[...]


def _mods(c, w, b, layer):
    bsz, d = c.shape
    n = w.shape[-1]
    tn = _pick_tile(n, (512, 256, 128))
    return pl.pallas_call(
        _mods_kernel,
        grid=(n // tn,),
        in_specs=[
            pl.BlockSpec((bsz, d), lambda j: (0, 0)),
            pl.BlockSpec((None, d, tn), lambda j: (layer, 0, j)),
            pl.BlockSpec((None, 1, tn), lambda j: (layer, 0, j)),
        ],
        out_specs=pl.BlockSpec((bsz, tn), lambda j: (0, j)),
        out_shape=jax.ShapeDtypeStruct((bsz, n), F32),
        compiler_params=_compiler_params(("parallel",)),
        name="mods",
    )(c, w, b.reshape(b.shape[0], 1, n))


def _normmod_rows(x_ref, rows, gain, mul, shift):
    xf = x_ref[rows, :]
    y = xf * lax.rsqrt(jnp.mean(xf * xf, axis=-1, keepdims=True) + EPS) * gain
    return y * mul + shift


def _fill_lhs(x_ref, gn_ref, sh_ref, sc_ref, h_scr):
    tm = x_ref.shape[0]
    gain = gn_ref[...]
    mul = 1.0 + sc_ref[...]
    shift = sh_ref[...]

    def body(it, carry):
        rows = pl.ds(pl.multiple_of(it * FILL_ROWS, FILL_ROWS), FILL_ROWS)
        h_scr[rows, :] = _normmod_rows(x_ref, rows, gain, mul, shift).astype(BF16)
        return carry

    lax.fori_loop(0, tm // FILL_ROWS, body, 0)


def _deint_lhs_kernel(x_ref, gn_ref, sh_ref, sc_ref, o_ref, slab_scr, *, dilations):
    tm, d_model = x_ref.shape
    n_slabs = d_model // LANES
    gain = gn_ref[...]
    mul = 1.0 + sc_ref[...]
    shift = sh_ref[...]

    def natural(it, carry):
        rows = pl.ds(pl.multiple_of(it * FILL_ROWS, FILL_ROWS), FILL_ROWS)
        h = _normmod_rows(x_ref, rows, gain, mul, shift)
        for v, d in enumerate(dilations):
            if d == 1:
                o_ref[v, rows, :] = h.astype(BF16)
        for s in range(n_slabs):
            slab_scr[s, rows, :] = h[:, s * LANES:(s + 1) * LANES]
        return carry

    lax.fori_loop(0, tm // FILL_ROWS, natural, 0)

    for v, d in enumerate(dilations):
        if d == 1:
            continue
        per_residue = tm // d
        chunk = min(per_residue, FILL_ROWS)
        chunks_per_residue = per_residue // chunk

        def gather(it, carry, v=v, d=d, per_residue=per_residue, chunk=chunk,
                   chunks_per_residue=chunks_per_residue):
            r = it // chunks_per_residue
            l0 = (it % chunks_per_residue) * chunk
            src = pl.ds(r + l0 * d, chunk, stride=d)
            dst = pl.ds(pl.multiple_of(r * per_residue + l0, chunk), chunk)
            for s in range(n_slabs):
                o_ref[v, dst, s * LANES:(s + 1) * LANES] = slab_scr[s, src, :].astype(BF16)
            return carry

        lax.fori_loop(0, tm // chunk, gather, 0)


def _deint_lhs(x2, seq, gnorm, shift, scale, dilations):
    m, d = x2.shape
    tm = DEINT_ROWS
    assert seq % tm == 0 and all(tm % (dl * BF16_SUBLANES) == 0 for dl in dilations)
    tpb = seq // tm
    n_var = len(dilations)
    return pl.pallas_call(
        functools.partial(_deint_lhs_kernel, dilations=dilations),
        grid=(m // tm,),
        in_specs=[
            pl.BlockSpec((tm, d), lambda i: (i, 0)),
            pl.BlockSpec((1, d), lambda i: (0, 0)),
            pl.BlockSpec((None, 1, d), lambda i: (i // tpb, 0, 0)),
            pl.BlockSpec((None, 1, d), lambda i: (i // tpb, 0, 0)),
        ],
        out_specs=pl.BlockSpec((n_var, tm, d), lambda i: (0, i, 0)),
        out_shape=jax.ShapeDtypeStruct((n_var, m, d), BF16),
        scratch_shapes=[pltpu.VMEM((d // LANES, tm, LANES), F32)],
        compiler_params=_compiler_params(("parallel",)),
        name="deint_lhs",
    )(x2, gnorm.reshape(1, d), shift, scale)


def _head_epilogue(acc_scr, o_ref, gain, cos_ref=None, sin_ref=None):
    tm, tn = acc_scr.shape

    def body(r, carry):
        rows = pl.ds(pl.multiple_of(r * EPILOGUE_ROWS, EPILOGUE_ROWS), EPILOGUE_ROWS)
        if cos_ref is not None:
            cos = cos_ref[rows, :]
            sin = sin_ref[rows, :]
        for h in range(tn // HEAD_DIM):
            cols = slice(h * HEAD_DIM, (h + 1) * HEAD_DIM)
            y = _head_rmsnorm(acc_scr[rows, cols], gain)
            if cos_ref is not None:
                y = y * cos + pltpu.roll(y, HEAD_DIM // 2, 1) * sin
            o_ref[rows, cols] = y.astype(o_ref.dtype)
        return carry

    lax.fori_loop(0, tm // EPILOGUE_ROWS, body, 0)


def _qkv_weight_kernel(w_ref, perm_ref, o_ref, *, n_perm_tiles):
    w = w_ref[...].astype(BF16)

    @pl.when(pl.program_id(1) < n_perm_tiles)
    def _():
        for h in range(w.shape[1] // HEAD_DIM):
            cols = slice(h * HEAD_DIM, (h + 1) * HEAD_DIM)
            o_ref[:, cols] = jnp.dot(w[:, cols], perm_ref[...],
                                     preferred_element_type=F32).astype(BF16)

    @pl.when(pl.program_id(1) >= n_perm_tiles)
    def _():
        o_ref[...] = w


def _qkv_weights(w, n_groups):
    n_layers, d, n = w.shape
    tn = PROJ_COLS
    assert n == 3 * n_groups * tn
    order = _rotary_lane_order()
    perm = np.zeros((HEAD_DIM, HEAD_DIM), np.float32)
    perm[order, np.arange(HEAD_DIM)] = 1.0
    return pl.pallas_call(
        functools.partial(_qkv_weight_kernel, n_perm_tiles=2 * n_groups),
        grid=(n_layers, n // tn),
        in_specs=[
            pl.BlockSpec((None, d, tn), lambda l, j: (l, 0, j)),
            pl.BlockSpec((HEAD_DIM, HEAD_DIM), lambda l, j: (0, 0)),
        ],
        out_specs=pl.BlockSpec((None, d, tn), lambda l, j: (l, 0, j)),
        out_shape=jax.ShapeDtypeStruct(w.shape, BF16),
        compiler_params=_compiler_params(("parallel", "parallel")),
        name="qkv_weights",
    )(w, jnp.asarray(perm, BF16))


def _qkv_kernel(h_ref, w_ref, gain_ref, cos_ref, sin_ref, o_ref, acc_scr, *, n_parts):
    part = pl.program_id(1) % n_parts
    acc_scr[...] = jnp.dot(h_ref[...], w_ref[...], preferred_element_type=F32)

    @pl.when(part < 2)
    def _():
        gain = gain_ref[...] * jnp.where(part == 0, HEAD_DIM ** -0.5 * LOG2E, 1.0)
        _head_epilogue(acc_scr, o_ref, gain, cos_ref, sin_ref)

    @pl.when(part >= 2)
    def _():
        o_ref[...] = acc_scr[...].astype(o_ref.dtype)


def _qkv_proj(h_var, seq, w, layer, gains, rope_tables):
    n_groups, m, d = h_var.shape
    n = w.shape[-1]
    n_parts = 3
    tm = PROJ_ROWS
    tn = PROJ_COLS
    assert seq % tm == 0 and n == n_parts * n_groups * tn
    tpb = seq // tm

    def col(j):
        return (j % n_parts) * n_groups + j // n_parts

    return pl.pallas_call(
        functools.partial(_qkv_kernel, n_parts=n_parts),
        grid=(m // tm, n_parts * n_groups),
        in_specs=[
            pl.BlockSpec((None, tm, d), lambda i, j: (j // n_parts, i, 0)),
            pl.BlockSpec((None, d, tn), lambda i, j: (layer, 0, col(j))),
            pl.BlockSpec((None, 1, HEAD_DIM),
                         lambda i, j: (jnp.minimum(col(j), 2 * n_groups - 1), 0, 0)),
            pl.BlockSpec((None, tm, HEAD_DIM), lambda i, j: (j // n_parts, i % tpb, 0)),
            pl.BlockSpec((None, tm, HEAD_DIM), lambda i, j: (j // n_parts, i % tpb, 0)),
        ],
        out_specs=pl.BlockSpec((tm, tn), lambda i, j: (i, col(j))),
        out_shape=jax.ShapeDtypeStruct((m, n), BF16),
        scratch_shapes=[pltpu.VMEM((tm, tn), F32)],
        compiler_params=_compiler_params(("parallel", "arbitrary")),
        name="qkv_proj",
    )(h_var, w, gains, *rope_tables)


def _split3_bf16(v):
    hi = v.astype(BF16)
    rem = v - hi.astype(F32)
    mid = rem.astype(BF16)
    lo = (rem - mid.astype(F32)).astype(BF16)
    return hi, mid, lo


def _proj_kernel(*refs, n_norm_tiles, forget, tiles_per_batch, out_scale):
    x_ref, gn_ref, sh_ref, sc_ref, w_ref, gain_ref = refs[:6]
    pos = 6
    if forget:
        wf_ref, bf_ref, tri_ref, place_ref, ones_ref = refs[pos:pos + 5]
        pos += 5
    o_ref = refs[pos]
    pos += 1
    if forget:
        fk_ref, fq_ref = refs[pos:pos + 2]
        pos += 2
    h_scr, acc_scr = refs[pos:pos + 2]
    pos += 2
    if forget:
        carry_scr = refs[pos]

    i = pl.program_id(0)
    j = pl.program_id(1)
    tm = acc_scr.shape[0]

    @pl.when(j == 0)
    def _():
        _fill_lhs(x_ref, gn_ref, sh_ref, sc_ref, h_scr)
        if forget:
            z = jnp.dot(h_scr[...], wf_ref[...], preferred_element_type=F32) + bf_ref[...]
            log_f = jnp.minimum(z, 0.0) - jnp.log1p(jnp.exp(-jnp.abs(z)))
            tri = tri_ref[...]
            cum = None
            for part in _split3_bf16(log_f):
                term = jnp.dot(tri, part, preferred_element_type=F32)
                cum = term if cum is None else cum + term

            @pl.when(i % tiles_per_batch == 0)
            def _():
                carry_scr[...] = jnp.zeros_like(carry_scr)

            cum = cum + carry_scr[...]
            carry_scr[...] = cum[tm - 1:tm, :]
            parts = _split3_bf16(cum * (-LOG2E))
            for side, out_ref in enumerate((fk_ref, fq_ref)):
                cols = ones_ref[side]
                for p, part in enumerate(parts):
                    cols = cols + jnp.dot(part, place_ref[side, p], preferred_element_type=F32)
                out_ref[...] = cols.astype(BF16)

    acc_scr[...] = jnp.dot(h_scr[...], w_ref[...], preferred_element_type=F32)

    @pl.when(j < n_norm_tiles)
    def _():
        _head_epilogue(acc_scr, o_ref, gain_ref[...] * out_scale)

    @pl.when(j >= n_norm_tiles)
    def _():
        o_ref[...] = acc_scr[...].astype(o_ref.dtype)


def _fox_bias_lanes(n_heads):
    assert 6 * n_heads <= LANES
    place = np.zeros((2, 3, LANES, LANES), np.float32)
    ones = np.zeros((2, 1, LANES), np.float32)
    for h in range(n_heads):
        for p in range(3):
            place[0, p, h, 3 * h + p] = 1.0
            place[1, p, h, 3 * (n_heads + h) + p] = -1.0
    ones[0, 0, 3 * n_heads:6 * n_heads] = 1.0
    ones[1, 0, :3 * n_heads] = 1.0
    return jnp.asarray(place, BF16), jnp.asarray(ones, F32)


def _proj(x2, seq, gnorm, shift, scale, w, layer, gain, n_norm_tiles, forget=None,
          out_scale=1.0):
    m, d = x2.shape
    n = w.shape[-1]
    tm = PROJ_ROWS
    tn = PROJ_COLS
    assert seq % tm == 0 and n % tn == 0
    tpb = seq // tm
    with_forget = forget is not None

    in_specs = [
        pl.BlockSpec((tm, d), lambda i, j: (i, 0)),
        pl.BlockSpec((1, d), lambda i, j: (0, 0)),
        pl.BlockSpec((None, 1, d), lambda i, j: (i // tpb, 0, 0)),
        pl.BlockSpec((None, 1, d), lambda i, j: (i // tpb, 0, 0)),
        pl.BlockSpec((None, d, tn), lambda i, j: (layer, 0, j)),
        pl.BlockSpec((1, HEAD_DIM), lambda i, j: (0, 0)),
    ]
    args = [x2, gnorm.reshape(1, d), shift, scale, w, gain]
    out_specs = [pl.BlockSpec((tm, tn), lambda i, j: (i, j))]
    out_shape = [jax.ShapeDtypeStruct((m, n), BF16)]
    scratch = [pltpu.VMEM((tm, d), BF16), pltpu.VMEM((tm, tn), F32)]
    if with_forget:
        wf_pad, bf_pad, n_heads = forget
        tri = (lax.broadcasted_iota(jnp.int32, (tm, tm), 0)
               >= lax.broadcasted_iota(jnp.int32, (tm, tm), 1)).astype(BF16)
        place, ones = _fox_bias_lanes(n_heads)
        in_specs += [
            pl.BlockSpec((d, LANES), lambda i, j: (0, 0)),
            pl.BlockSpec((1, LANES), lambda i, j: (0, 0)),
            pl.BlockSpec((tm, tm), lambda i, j: (0, 0)),
            pl.BlockSpec((2, 3, LANES, LANES), lambda i, j: (0, 0, 0, 0)),
            pl.BlockSpec((2, 1, LANES), lambda i, j: (0, 0, 0)),
        ]
        args += [wf_pad, bf_pad, tri, place, ones]
        out_specs += [
            pl.BlockSpec((tm, LANES), lambda i, j: (i, 0)),
            pl.BlockSpec((tm, LANES), lambda i, j: (i, 0)),
        ]
        out_shape += [
            jax.ShapeDtypeStruct((m, LANES), BF16),
            jax.ShapeDtypeStruct((m, LANES), BF16),
        ]
        scratch.append(pltpu.VMEM((1, LANES), F32))

    kern = functools.partial(_proj_kernel, n_norm_tiles=n_norm_tiles, forget=with_forget,
                             tiles_per_batch=tpb, out_scale=out_scale)
    outs = pl.pallas_call(
        kern,
        grid=(m // tm, n // tn),
        in_specs=in_specs,
        out_specs=out_specs,
        out_shape=out_shape,
        scratch_shapes=scratch,
        compiler_params=_compiler_params(("arbitrary", "arbitrary")),
        name="proj",
    )(*args)
    return outs if with_forget else outs[0]


def _swiglu_kernel(x_ref, gn_ref, sh_ref, sc_ref, wg_ref, wu_ref, o_ref, h_scr):
    @pl.when(pl.program_id(1) == 0)
    def _():
        _fill_lhs(x_ref, gn_ref, sh_ref, sc_ref, h_scr)

    h = h_scr[...]
    g = jnp.dot(h, wg_ref[...], preferred_element_type=F32)
    u = jnp.dot(h, wu_ref[...], preferred_element_type=F32)
    o_ref[...] = (_silu(g) * u).astype(o_ref.dtype)


def _swiglu_in(x2, seq, gnorm, shift, scale, w, layer):
    m, d = x2.shape
    f = w.shape[-1] // 2
    tm = PROJ_ROWS
    tf = _pick_tile(f, (512, 256, 128))
    assert seq % tm == 0
    tpb = seq // tm
    nf = f // tf
    return pl.pallas_call(
        _swiglu_kernel,
        grid=(m // tm, nf),
        in_specs=[
            pl.BlockSpec((tm, d), lambda i, j: (i, 0)),
            pl.BlockSpec((1, d), lambda i, j: (0, 0)),
            pl.BlockSpec((None, 1, d), lambda i, j: (i // tpb, 0, 0)),
            pl.BlockSpec((None, 1, d), lambda i, j: (i // tpb, 0, 0)),
            pl.BlockSpec((None, d, tf), lambda i, j: (layer, 0, j)),
            pl.BlockSpec((None, d, tf), lambda i, j: (layer, 0, nf + j)),
        ],
        out_specs=pl.BlockSpec((tm, tf), lambda i, j: (i, j)),
        out_shape=jax.ShapeDtypeStruct((m, f), BF16),
        scratch_shapes=[pltpu.VMEM((tm, d), BF16)],
        compiler_params=_compiler_params(("parallel", "arbitrary")),
        name="swiglu_in",
    )(x2, gnorm.reshape(1, d), shift, scale, w, w)


def _mmres_kernel(a_ref, w_ref, x_ref, gt_ref, o_ref):
    y = jnp.dot(a_ref[...], w_ref[...], preferred_element_type=F32)
    o_ref[...] = x_ref[...] + gt_ref[...] * y


def _mm_residual(a, w, layer, x2, seq, gate):
    m, k = a.shape
    n = w.shape[-1]
    tm = PROJ_ROWS
    tn = _pick_tile(n, (512, 256, 128))
    assert seq % tm == 0
    tpb = seq // tm
    return pl.pallas_call(
        _mmres_kernel,
        grid=(m // tm, n // tn),
        in_specs=[
            pl.BlockSpec((tm, k), lambda i, j: (i, 0)),
            pl.BlockSpec((None, k, tn), lambda i, j: (layer, 0, j)),
            pl.BlockSpec((tm, tn), lambda i, j: (i, j)),
            pl.BlockSpec((None, 1, tn), lambda i, j: (i // tpb, 0, j)),
        ],
        out_specs=pl.BlockSpec((tm, tn), lambda i, j: (i, j)),
        out_shape=jax.ShapeDtypeStruct((m, n), F32),
        compiler_params=_compiler_params(("parallel", "parallel")),
        name="mm_residual",
    )(a, w, x2, gate)


def _dilated_kernel(q_ref, kc_ref, kp_ref, vc_ref, vp_ref, o_ref, lse_ref, o_scr, *,
                    n_back, n_heads, dilation):
    blk = pl.program_id(1)
    res = pl.program_id(2)
    width = n_heads * HEAD_DIM
    qi = lax.broadcasted_iota(jnp.int32, (ATT_BLOCK, 2 * ATT_BLOCK), 0)
    kj = lax.broadcasted_iota(jnp.int32, (ATT_BLOCK, 2 * ATT_BLOCK), 1)
    dist = qi + ATT_BLOCK - kj
    mask = (dist >= 0) & (dist <= n_back) & ((kj >= ATT_BLOCK) | (blk > 0))
    lane = lax.broadcasted_iota(jnp.int32, (ATT_BLOCK, LANES), 1)
    contract_last = (((1,), (1,)), ((), ()))

    q_all = q_ref[...].reshape(ATT_BLOCK, width)
    k_all = jnp.concatenate([kp_ref[...].reshape(ATT_BLOCK, width),
                             kc_ref[...].reshape(ATT_BLOCK, width)], axis=0)
    v_all = jnp.concatenate([vp_ref[...].reshape(ATT_BLOCK, width),
                             vc_ref[...].reshape(ATT_BLOCK, width)], axis=0)
    if dilation == 1:
        rows = slice(None)
    else:
        rows = pl.ds(res, ATT_BLOCK, stride=dilation)

    lse_tile = jnp.zeros((ATT_BLOCK, LANES), F32)
    for h in range(n_heads):
        cols = slice(h * HEAD_DIM, (h + 1) * HEAD_DIM)
        s2 = lax.dot_general(q_all[:, cols], k_all[:, cols], contract_last,
                             preferred_element_type=F32)
        s2 = jnp.where(mask, s2, NEG_INF)
        mx2 = jnp.max(s2, axis=-1, keepdims=True)
        p = jnp.exp2(s2 - mx2)
        den = jnp.sum(p, axis=-1, keepdims=True)
        o = jnp.dot(p.astype(BF16), v_all[:, cols], preferred_element_type=F32)
        o_scr[h, rows, :] = o / den
        lse_tile = jnp.where(lane == h, mx2 * LN2 + jnp.log(den), lse_tile)
    lse_ref[rows, :] = lse_tile

    @pl.when(res == dilation - 1)
    def _():
        for h in range(n_heads):
            o_ref[:, h * HEAD_DIM:(h + 1) * HEAD_DIM] = o_scr[h].astype(o_ref.dtype)


def _dilated_group(qkv, bsz, seq, group, n_groups, width, window, dilation):
    n_back = window // dilation
    sub_len = seq // dilation
    assert sub_len % ATT_BLOCK == 0
    nb = sub_len // ATT_BLOCK
    n_cols = qkv.shape[1]
    tpb = seq // DEINT_ROWS
    per_residue = DEINT_ROWS // dilation

    if per_residue >= ATT_BLOCK:
        bpt = per_residue // ATT_BLOCK
        view = qkv.reshape(bsz, tpb, dilation, bpt, ATT_BLOCK, n_cols)
        block = (None, None, None, None, ATT_BLOCK, width)

        def rows_index(b, n, r):
            return (b, n // bpt, r, n % bpt, 0)
    else:
        pieces = ATT_BLOCK // per_residue
        view = qkv.reshape(bsz, tpb, dilation, per_residue, n_cols)
        block = (None, pieces, None, per_residue, width)

        def rows_index(b, n, r):
            return (b, n, r, 0)

    def spec(part, prev):
        def index(b, n, r):
            nn = jnp.maximum(n - 1, 0) if prev else n
            return rows_index(b, nn, r) + (part * n_groups + group,)
        return pl.BlockSpec(block, index)

    span = ATT_BLOCK * dilation
    o, lse = pl.pallas_call(
        functools.partial(_dilated_kernel, n_back=n_back, n_heads=width // HEAD_DIM,
                          dilation=dilation),
        grid=(bsz, nb, dilation),
        in_specs=[spec(0, False), spec(1, False), spec(1, True), spec(2, False), spec(2, True)],
        out_specs=[
            pl.BlockSpec((None, span, width), lambda b, n, r: (b, n, 0)),
            pl.BlockSpec((None, span, LANES), lambda b, n, r: (b, n, 0)),
        ],
        out_shape=[
            jax.ShapeDtypeStruct((bsz, seq, width), BF16),
            jax.ShapeDtypeStruct((bsz, seq, LANES), F32),
        ],
        scratch_shapes=[pltpu.VMEM((width // HEAD_DIM, span, HEAD_DIM), F32)],
        compiler_params=_compiler_params(("parallel", "arbitrary", "arbitrary")),
        name="dilated_attention",
    )(view, view, view, view, view)
    return o.reshape(bsz * seq, width), lse.reshape(bsz * seq, LANES)


def _mix_out_kernel(*refs, n_groups, n_heads, row_chunk):
    o_refs = refs[:n_groups]
    lse_refs = refs[n_groups:2 * n_groups]
    w_ref, x_ref, gt_ref, out_ref, lhs_scr = refs[2 * n_groups:]
    tm = x_ref.shape[0]

    def body(r, carry):
        rows = pl.ds(pl.multiple_of(r * row_chunk, row_chunk), row_chunk)
        lses = [ref[rows, :] for ref in lse_refs]
        mx = functools.reduce(jnp.maximum, lses)
        es = [jnp.exp(l - mx) for l in lses]
        inv = 1.0 / functools.reduce(lambda a, b: a + b, es)
        alphas = [e * inv for e in es]
        for h in range(n_heads):
            cols = slice(h * HEAD_DIM, (h + 1) * HEAD_DIM)
            mixed = None
            for g in range(n_groups):
                term = alphas[g][:, h:h + 1] * o_refs[g][rows, cols].astype(F32)
                mixed = term if mixed is None else mixed + term
            lhs_scr[rows, cols] = mixed.astype(BF16)
        return carry

    lax.fori_loop(0, tm // row_chunk, body, 0)
    y = jnp.dot(lhs_scr[...], w_ref[...], preferred_element_type=F32)
    out_ref[...] = x_ref[...] + gt_ref[...] * y


def _mix_out(o_list, lse_list, w, layer, x2, seq, gate):
    m, width = o_list[0].shape
    n = w.shape[-1]
    n_groups = len(o_list)
    tm = _pick_tile(seq, (512, 256, 128))
    tpb = seq // tm
    return pl.pallas_call(
        functools.partial(_mix_out_kernel, n_groups=n_groups, n_heads=width // HEAD_DIM,
                          row_chunk=128),
        grid=(m // tm,),
        in_specs=(
            [pl.BlockSpec((tm, width), lambda i: (i, 0))] * n_groups
            + [pl.BlockSpec((tm, LANES), lambda i: (i, 0))] * n_groups
            + [
                pl.BlockSpec((None, width, n), lambda i: (layer, 0, 0)),
                pl.BlockSpec((tm, n), lambda i: (i, 0)),
                pl.BlockSpec((None, 1, n), lambda i: (i // tpb, 0, 0)),
            ]
        ),
        out_specs=pl.BlockSpec((tm, n), lambda i: (i, 0)),
        out_shape=jax.ShapeDtypeStruct((m, n), F32),
        scratch_shapes=[pltpu.VMEM((tm, width), BF16)],
        compiler_params=_compiler_params(("parallel",)),
        name="mix_out",
    )(*o_list, *lse_list, w, x2, gate)


def _fox_kernel(q_ref, k_ref, v_ref, fq_ref, fk_ref, o_ref, *, blk, n_heads, heads_per_step):
    head0 = pl.program_id(1) * heads_per_step
    qt = pl.program_id(2)
    lane = lax.broadcasted_iota(jnp.int32, (1, LANES), 1)
    fq_all = fq_ref[...]
    contract_last = (((1,), (1,)), ((), ()))

    qs = []
    for hh in range(heads_per_step):
        off = (lane % (3 * n_heads)) - 3 * (head0 + hh)
        keep = jnp.where(off >= 0, jnp.where(off < 3, 1.0, 0.0), 0.0)
        keep = jnp.where(lane < 6 * n_heads, keep, 0.0).astype(BF16)
        qs.append(jnp.concatenate([q_ref[:, hh * HEAD_DIM:(hh + 1) * HEAD_DIM],
                                   fq_all * keep], axis=1))

    def step(kb, carry, masked):
        rows = pl.ds(pl.multiple_of(kb * blk, blk), blk)
        fk = fk_ref[rows, :]
        out = []
        for hh in range(heads_per_step):
            m_run, l_run, acc = carry[hh]
            cols = slice(hh * HEAD_DIM, (hh + 1) * HEAD_DIM)
            k = jnp.concatenate([k_ref[rows, cols], fk], axis=1)
            s2 = lax.dot_general(qs[hh], k, contract_last, preferred_element_type=F32)
            if masked:
                qi = lax.broadcasted_iota(jnp.int32, (blk, blk), 0)
                kj = lax.broadcasted_iota(jnp.int32, (blk, blk), 1)
                s2 = jnp.where(kj <= qi, s2, NEG_INF)
            m_new = jnp.maximum(m_run, jnp.max(s2, axis=-1, keepdims=True))
            alpha = jnp.exp2(m_run - m_new)
            p = jnp.exp2(s2 - m_new)
            l_new = alpha * l_run + jnp.sum(p, axis=-1, keepdims=True)
            acc_new = alpha * acc + jnp.dot(p.astype(BF16), v_ref[rows, cols],
                                            preferred_element_type=F32)
            out.append((m_new, l_new, acc_new))
        return tuple(out)

    init = tuple((jnp.full((blk, 1), NEG_INF, F32), jnp.zeros((blk, 1), F32),
                  jnp.zeros((blk, HEAD_DIM), F32)) for _ in range(heads_per_step))
    carry = lax.fori_loop(0, qt, lambda kb, c: step(kb, c, False), init)
    final = step(qt, carry, True)
    for hh in range(heads_per_step):
        _, l_fin, acc = final[hh]
        o_ref[:, hh * HEAD_DIM:(hh + 1) * HEAD_DIM] = (acc / l_fin).astype(o_ref.dtype)


def _fox_attention(q, kv, f_query, f_keys, bsz, seq, n_heads):
    blk = _pick_tile(seq, (FOX_BLOCK, 256, 128))
    nkb = seq // blk
    width = n_heads * HEAD_DIM
    hps = FOX_HEADS_PER_STEP
    assert n_heads % hps == 0
    n_hg = n_heads // hps
    qv = q.reshape(bsz, seq, width)
    kvv = kv.reshape(bsz, seq, 2 * width)
    o = pl.pallas_call(
        functools.partial(_fox_kernel, blk=blk, n_heads=n_heads, heads_per_step=hps),
        grid=(bsz, n_hg, nkb),
        in_specs=[
            pl.BlockSpec((None, blk, hps * HEAD_DIM), lambda b, h, t: (b, t, h)),
            pl.BlockSpec((None, seq, hps * HEAD_DIM), lambda b, h, t: (b, 0, h)),
            pl.BlockSpec((None, seq, hps * HEAD_DIM), lambda b, h, t: (b, 0, n_hg + h)),
            pl.BlockSpec((None, blk, LANES), lambda b, h, t: (b, t, 0)),
            pl.BlockSpec((None, seq, LANES), lambda b, h, t: (b, 0, 0)),
        ],
        out_specs=pl.BlockSpec((None, blk, hps * HEAD_DIM), lambda b, h, t: (b, t, h)),
        out_shape=jax.ShapeDtypeStruct((bsz, seq, width), BF16),
        compiler_params=_compiler_params(("parallel", "parallel", "arbitrary")),
        name="fox_attention",
    )(qv, kvv, kvv, f_query.reshape(bsz, seq, LANES), f_keys.reshape(bsz, seq, LANES))
    return o.reshape(bsz * seq, width)


def _rotary_lane_order():
    half = ROT_DIM // 2
    mid = HEAD_DIM // 2
    return np.concatenate([np.arange(0, half), np.arange(ROT_DIM, mid + half),
                           np.arange(half, ROT_DIM), np.arange(mid + half, HEAD_DIM)])


def _rope_tables(seq, dilations):
    half = ROT_DIM // 2
    mid = HEAD_DIM // 2
    inv = ROPE_THETA ** (-jnp.arange(0, ROT_DIM, 2, dtype=F32) / ROT_DIM)
    ang = jnp.arange(seq, dtype=F32)[:, None] * inv[None, :]
    cos, sin = jnp.cos(ang), jnp.sin(ang)
    ones = jnp.ones((seq, mid - half), F32)
    zeros = jnp.zeros((seq, mid - half), F32)
    cos_t = jnp.concatenate([cos, ones, cos, ones], axis=-1)
    sin_t = jnp.concatenate([-sin, zeros, sin, zeros], axis=-1)

    def deinterleave(t, d):
        t = t.reshape(seq // DEINT_ROWS, DEINT_ROWS // d, d, HEAD_DIM)
        return jnp.swapaxes(t, 1, 2).reshape(seq, HEAD_DIM)

    return (jnp.stack([deinterleave(cos_t, d) for d in dilations]),
            jnp.stack([deinterleave(sin_t, d) for d in dilations]))


def _split_mods(mods, parts):
    bsz, n = mods.shape
    d = n // parts
    return [mods[:, p * d:(p + 1) * d].reshape(bsz, 1, d) for p in range(parts)]


def kernel(x, c, w_ada, b_ada, g_norm_attn, g_norm_ffn, w_qkv_a, g_qk_a, w_o_a, w_ada_kv, b_ada_kv, g_norm_kv, w_kv, g_k_b, w_f, b_f, w_q_b, g_q_b, w_o_b, w_ffn_in, w_ffn_out):
    bsz, seq, d = x.shape
    depth = w_ada.shape[0]
    n_a = w_qkv_a.shape[0]
    n_groups = g_qk_a.shape[2]
    width_a = w_o_a.shape[1]
    n_heads_b = w_f.shape[1]
    assert n_groups == len(DIL_CONFIGS) and n_heads_b <= LANES and width_a == PROJ_COLS
    dilations = tuple(dl for _, dl in DIL_CONFIGS)

    x2 = x.reshape(bsz * seq, d)
    rope_tables = _rope_tables(seq, dilations)

    w_qkv = _qkv_weights(w_qkv_a, n_groups)
    g_qk = g_qk_a[..., _rotary_lane_order()]
    w_o_a = w_o_a.astype(BF16)
    w_kv = w_kv.astype(BF16)[None]
    w_q_b = w_q_b.astype(BF16)
    w_o_b = w_o_b.astype(BF16)
    w_ffn_in = w_ffn_in.astype(BF16)
    w_ffn_out = w_ffn_out.astype(BF16)
    wf_pad = jnp.pad(w_f, ((0, 0), (0, LANES - n_heads_b))).astype(BF16)
    bf_pad = jnp.pad(b_f, (0, LANES - n_heads_b)).reshape(1, LANES)

    kv = f_keys = f_query = None
    for layer in range(depth):
        sh_a, sc_a, gt_a, sh_f, sc_f, gt_f = _split_mods(_mods(c, w_ada, b_ada, layer), 6)
        if layer < n_a:
            gains = g_qk[layer].reshape(2 * n_groups, 1, HEAD_DIM)
            h_var = _deint_lhs(x2, seq, g_norm_attn[layer], sh_a, sc_a, dilations)
            qkv = _qkv_proj(h_var, seq, w_qkv, layer, gains, rope_tables)
            outs = [_dilated_group(qkv, bsz, seq, g, n_groups, width_a, window, dilation)
                    for g, (window, dilation) in enumerate(DIL_CONFIGS)]
            x2 = _mix_out([o for o, _ in outs], [l for _, l in outs], w_o_a, layer, x2, seq, gt_a)
        else:
            i = layer - n_a
            q = _proj(x2, seq, g_norm_attn[layer], sh_a, sc_a, w_q_b, i,
                      g_q_b[i].reshape(1, HEAD_DIM),
                      n_norm_tiles=w_q_b.shape[-1] // PROJ_COLS,
                      out_scale=HEAD_DIM ** -0.5 * LOG2E)
            o = _fox_attention(q, kv, f_query, f_keys, bsz, seq, n_heads_b)
            x2 = _mm_residual(o, w_o_b, i, x2, seq, gt_a)
        a = _swiglu_in(x2, seq, g_norm_ffn[layer], sh_f, sc_f, w_ffn_in, layer)
        x2 = _mm_residual(a, w_ffn_out, layer, x2, seq, gt_f)
        if layer == n_a - 1:
            sh_kv, sc_kv = _split_mods(_mods(c, w_ada_kv[None], b_ada_kv[None], 0), 2)
            kv, f_keys, f_query = _proj(
                x2, seq, g_norm_kv, sh_kv, sc_kv, w_kv, 0, g_k_b.reshape(1, HEAD_DIM),
                n_norm_tiles=w_kv.shape[-1] // 2 // PROJ_COLS, forget=(wf_pad, bf_pad, n_heads_b))
    return x2.reshape(bsz, seq, d)
```

```python
import functools
import math

import numpy as np
import jax
import jax.numpy as jnp
from jax import lax
from jax.experimental import pallas as pl
from jax.experimental.pallas import tpu as pltpu

HEAD_DIM = 128
DIL_CONFIGS = ((128, 1), (512, 4), (2048, 16))
ROT_DIM = HEAD_DIM // 4
ROPE_THETA = 500000.0
ATT_BLOCK = 128
EPS = 1e-6
NEG_INF = -1e30
LOG2E = math.log2(math.e)
LN2 = math.log(2.0)

LANES = 128
BF16_SUBLANES = 16
VMEM_LIMIT_BYTES = 56 * 1024 * 1024

PROJ_ROWS = 1024
PROJ_COLS = 1024
RESIDUAL_ROWS = 512
FILL_ROWS = 128
DEINT_ROWS = 512
EPILOGUE_ROWS = 512
FOX_BLOCK = 512
FOX_HEADS_PER_STEP = 4

F32 = jnp.float32
BF16 = jnp.bfloat16


def _compiler_params(semantics):
    return pltpu.CompilerParams(dimension_semantics=semantics, vmem_limit_bytes=VMEM_LIMIT_BYTES)


def _pick_tile(n, candidates):
    for t in candidates:
        if n % t == 0:
            return t
    raise ValueError(f"no tile in {candidates} divides {n}")


def _head_rmsnorm(a, gain):
    return a * lax.rsqrt(jnp.mean(a * a, axis=-1, keepdims=True) + EPS) * gain


def _silu(v):
    return v * jax.nn.sigmoid(v)


def _mods_kernel(c_ref, w_ref, b_ref, o_ref):
    c_act = _silu(c_ref[...])
    o_ref[...] = jnp.dot(c_act.astype(BF16), w_ref[...].astype(BF16),
                         preferred_element_type=F32) + b_ref[...]


def _mods(c, w, b, layer):
    bsz, d = c.shape
    n = w.shape[-1]
    tn = _pick_tile(n, (512, 256, 128))
    return pl.pallas_call(
        _mods_kernel,
        grid=(n // tn,),
        in_specs=[
            pl.BlockSpec((bsz, d), lambda j: (0, 0)),
            pl.BlockSpec((None, d, tn), lambda j: (layer, 0, j)),
            pl.BlockSpec((None, 1, tn), lambda j: (layer, 0, j)),
        ],
        out_specs=pl.BlockSpec((bsz, tn), lambda j: (0, j)),
        out_shape=jax.ShapeDtypeStruct((bsz, n), F32),
        compiler_params=_compiler_params(("parallel",)),
        name="mods",
    )(c, w, b.reshape(b.shape[0], 1, n))


def _normmod_rows(x_ref, rows, gain, mul, shift):
    xf = x_ref[rows, :]
    y = xf * lax.rsqrt(jnp.mean(xf * xf, axis=-1, keepdims=True) + EPS) * gain
    return y * mul + shift


def _fill_lhs(x_ref, gn_ref, sh_ref, sc_ref, h_scr, inv_scr):
    tm, d_model = x_ref.shape
    n_slabs = d_model // LANES

    def rms(it, carry):
        rows = pl.ds(pl.multiple_of(it * FILL_ROWS, FILL_ROWS), FILL_ROWS)
        ssq = None
        for s in range(n_slabs):
            xs = x_ref[rows, s * LANES:(s + 1) * LANES]
            ssq = xs * xs if ssq is None else ssq + xs * xs
        inv = lax.rsqrt(jnp.sum(ssq, axis=-1, keepdims=True) * (1.0 / d_model) + EPS)
        inv_scr[rows, :] = jnp.broadcast_to(inv, (FILL_ROWS, LANES))
        return carry

    lax.fori_loop(0, tm // FILL_ROWS, rms, 0)

    def affine(it, carry):
        rows = pl.ds(pl.multiple_of(it * FILL_ROWS, FILL_ROWS), FILL_ROWS)
        inv = inv_scr[rows, :]
        for s in range(n_slabs):
            cols = slice(s * LANES, (s + 1) * LANES)
            y = x_ref[rows, cols] * inv * gn_ref[:, cols]
            h_scr[rows, cols] = (y * (1.0 + sc_ref[:, cols]) + sh_ref[:, cols]).astype(BF16)
        return carry

    lax.fori_loop(0, tm // FILL_ROWS, affine, 0)


def _deint_lhs_kernel(x_ref, gn_ref, sh_ref, sc_ref, o_ref, slab_scr, *, dilations):
    tm, d_model = x_ref.shape
    n_slabs = d_model // LANES
    gain = gn_ref[...]
    mul = 1.0 + sc_ref[...]
    shift = sh_ref[...]

    def natural(it, carry):
        rows = pl.ds(pl.multiple_of(it * FILL_ROWS, FILL_ROWS), FILL_ROWS)
        h = _normmod_rows(x_ref, rows, gain, mul, shift)
        for v, d in enumerate(dilations):
            if d == 1:
                o_ref[v, rows, :] = h.astype(BF16)
        for s in range(n_slabs):
            slab_scr[s, rows, :] = h[:, s * LANES:(s + 1) * LANES]
        return carry

    lax.fori_loop(0, tm // FILL_ROWS, natural, 0)

    for v, d in enumerate(dilations):
        if d == 1:
            continue
        per_residue = tm // d
        chunk = min(per_residue, FILL_ROWS)
        chunks_per_residue = per_residue // chunk

        def gather(it, carry, v=v, d=d, per_residue=per_residue, chunk=chunk,
                   chunks_per_residue=chunks_per_residue):
            r = it // chunks_per_residue
            l0 = (it % chunks_per_residue) * chunk
            src = pl.ds(r + l0 * d, chunk, stride=d)
            dst = pl.ds(pl.multiple_of(r * per_residue + l0, chunk), chunk)
            for s in range(n_slabs):
                o_ref[v, dst, s * LANES:(s + 1) * LANES] = slab_scr[s, src, :].astype(BF16)
            return carry

        lax.fori_loop(0, tm // chunk, gather, 0)


def _deint_lhs(x2, seq, gnorm, shift, scale, dilations):
    m, d = x2.shape
    tm = DEINT_ROWS
    assert seq % tm == 0 and all(tm % (dl * BF16_SUBLANES) == 0 for dl in dilations)
    tpb = seq // tm
    n_var = len(dilations)
    return pl.pallas_call(
        functools.partial(_deint_lhs_kernel, dilations=dilations),
        grid=(m // tm,),
        in_specs=[
            pl.BlockSpec((tm, d), lambda i: (i, 0)),
            pl.BlockSpec((1, d), lambda i: (0, 0)),
            pl.BlockSpec((None, 1, d), lambda i: (i // tpb, 0, 0)),
            pl.BlockSpec((None, 1, d), lambda i: (i // tpb, 0, 0)),
        ],
        out_specs=pl.BlockSpec((n_var, tm, d), lambda i: (0, i, 0)),
        out_shape=jax.ShapeDtypeStruct((n_var, m, d), BF16),
        scratch_shapes=[pltpu.VMEM((d // LANES, tm, LANES), F32)],
        compiler_params=_compiler_params(("parallel",)),
        name="deint_lhs",
    )(x2, gnorm.reshape(1, d), shift, scale)


def _dot_head_norm(lhs, w_ref, o_ref, acc_scr, gain, cos_ref=None, sin_ref=None):
    tm, tn = acc_scr.shape
    acc_scr[...] = jnp.dot(lhs, w_ref[...], preferred_element_type=F32)

    def body(r, carry):
        rows = pl.ds(pl.multiple_of(r * EPILOGUE_ROWS, EPILOGUE_ROWS), EPILOGUE_ROWS)
        if cos_ref is not None:
            cos = cos_ref[rows, :]
            sin = sin_ref[rows, :]
        for h in range(tn // HEAD_DIM):
            cols = slice(h * HEAD_DIM, (h + 1) * HEAD_DIM)
            y = _head_rmsnorm(acc_scr[rows, cols], gain)
            if cos_ref is not None:
                y = y * cos + pltpu.roll(y, HEAD_DIM // 2, 1) * sin
            o_ref[rows, cols] = y.astype(o_ref.dtype)
        return carry

    lax.fori_loop(0, tm // EPILOGUE_ROWS, body, 0)


def _qkv_weight_kernel(w_ref, perm_ref, o_ref, *, n_perm_tiles):
    w = w_ref[...].astype(BF16)

    @pl.when(pl.program_id(1) < n_perm_tiles)
    def _():
        for h in range(w.shape[1] // HEAD_DIM):
            cols = slice(h * HEAD_DIM, (h + 1) * HEAD_DIM)
            o_ref[:, cols] = jnp.dot(w[:, cols], perm_ref[...],
                                     preferred_element_type=F32).astype(BF16)

    @pl.when(pl.program_id(1) >= n_perm_tiles)
    def _():
        o_ref[...] = w


def _qkv_weights(w, n_groups):
    n_layers, d, n = w.shape
    tn = PROJ_COLS
    assert n == 3 * n_groups * tn
    order = _rotary_lane_order()
    perm = np.zeros((HEAD_DIM, HEAD_DIM), np.float32)
    perm[order, np.arange(HEAD_DIM)] = 1.0
    return pl.pallas_call(
        functools.partial(_qkv_weight_kernel, n_perm_tiles=2 * n_groups),
        grid=(n_layers, n // tn),
        in_specs=[
            pl.BlockSpec((None, d, tn), lambda l, j: (l, 0, j)),
            pl.BlockSpec((HEAD_DIM, HEAD_DIM), lambda l, j: (0, 0)),
        ],
        out_specs=pl.BlockSpec((None, d, tn), lambda l, j: (l, 0, j)),
        out_shape=jax.ShapeDtypeStruct(w.shape, BF16),
        compiler_params=_compiler_params(("parallel", "parallel")),
        name="qkv_weights",
    )(w, jnp.asarray(perm, BF16))


def _qkv_kernel(h_ref, w_ref, gain_ref, cos_ref, sin_ref, o_ref, acc_scr, *, n_parts):
    part = pl.program_id(1) % n_parts

    @pl.when(part < 2)
    def _():
        gain = gain_ref[...] * jnp.where(part == 0, HEAD_DIM ** -0.5 * LOG2E, 1.0)
        _dot_head_norm(h_ref[...], w_ref, o_ref, acc_scr, gain, cos_ref, sin_ref)

    @pl.when(part >= 2)
    def _():
        o_ref[...] = jnp.dot(h_ref[...], w_ref[...],
                             preferred_element_type=F32).astype(o_ref.dtype)


def _qkv_proj(h_var, seq, w, layer, gains, rope_tables):
    n_groups, m, d = h_var.shape
    n = w.shape[-1]
    n_parts = 3
    tm = PROJ_ROWS
    tn = PROJ_COLS
    assert seq % tm == 0 and n == n_parts * n_groups * tn
    tpb = seq // tm

    def col(j):
        return (j % n_parts) * n_groups + j // n_parts

    return pl.pallas_call(
        functools.partial(_qkv_kernel, n_parts=n_parts),
        grid=(m // tm, n_parts * n_groups),
        in_specs=[
            pl.BlockSpec((None, tm, d), lambda i, j: (j // n_parts, i, 0)),
            pl.BlockSpec((None, d, tn), lambda i, j: (layer, 0, col(j))),
            pl.BlockSpec((None, 1, HEAD_DIM),
                         lambda i, j: (jnp.minimum(col(j), 2 * n_groups - 1), 0, 0)),
            pl.BlockSpec((None, tm, HEAD_DIM), lambda i, j: (j // n_parts, i % tpb, 0)),
            pl.BlockSpec((None, tm, HEAD_DIM), lambda i, j: (j // n_parts, i % tpb, 0)),
        ],
        out_specs=pl.BlockSpec((tm, tn), lambda i, j: (i, col(j))),
        out_shape=jax.ShapeDtypeStruct((m, n), BF16),
        scratch_shapes=[pltpu.VMEM((tm, tn), F32)],
        compiler_params=_compiler_params(("parallel", "arbitrary")),
        name="qkv_proj",
    )(h_var, w, gains, *rope_tables)


def _split3_bf16(v):
    hi = v.astype(BF16)
    rem = v - hi.astype(F32)
    mid = rem.astype(BF16)
    lo = (rem - mid.astype(F32)).astype(BF16)
    return hi, mid, lo


def _proj_kernel(*refs, n_norm_tiles, forget, tiles_per_batch, out_scale):
    x_ref, gn_ref, sh_ref, sc_ref, w_ref, gain_ref = refs[:6]
    pos = 6
    if forget:
        wf_ref, bf_ref, tri_ref, place_ref, ones_ref = refs[pos:pos + 5]
        pos += 5
    o_ref = refs[pos]
    pos += 1
    if forget:
        fk_ref, fq_ref = refs[pos:pos + 2]
        pos += 2
    h_scr, inv_scr, acc_scr = refs[pos:pos + 3]
    pos += 3
    if forget:
        carry_scr = refs[pos]

    i = pl.program_id(0)
    j = pl.program_id(1)
    tm = h_scr.shape[0]

    @pl.when(j == 0)
    def _():
        _fill_lhs(x_ref, gn_ref, sh_ref, sc_ref, h_scr, inv_scr)
        if forget:
            z = jnp.dot(h_scr[...], wf_ref[...], preferred_element_type=F32) + bf_ref[...]
            log_f = jnp.minimum(z, 0.0) - jnp.log1p(jnp.exp(-jnp.abs(z)))
            tri = tri_ref[...]
            cum = None
            for part in _split3_bf16(log_f):
                term = jnp.dot(tri, part, preferred_element_type=F32)
                cum = term if cum is None else cum + term

            @pl.when(i % tiles_per_batch == 0)
            def _():
                carry_scr[...] = jnp.zeros_like(carry_scr)

            cum = cum + carry_scr[...]
            carry_scr[...] = cum[tm - 1:tm, :]
            parts = _split3_bf16(cum * (-LOG2E))
            for side, out_ref in enumerate((fk_ref, fq_ref)):
                cols = ones_ref[side]
                for p, part in enumerate(parts):
                    cols = cols + jnp.dot(part, place_ref[side, p], preferred_element_type=F32)
                out_ref[...] = cols.astype(BF16)

    @pl.when(j < n_norm_tiles)
    def _():
        _dot_head_norm(h_scr[...], w_ref, o_ref, acc_scr, gain_ref[...] * out_scale)

    @pl.when(j >= n_norm_tiles)
    def _():
        o_ref[...] = jnp.dot(h_scr[...], w_ref[...],
                             preferred_element_type=F32).astype(o_ref.dtype)


def _fox_bias_lanes(n_heads):
    assert 6 * n_heads <= LANES
    place = np.zeros((2, 3, LANES, LANES), np.float32)
    ones = np.zeros((2, 1, LANES), np.float32)
    for h in range(n_heads):
        for p in range(3):
            place[0, p, h, 3 * h + p] = 1.0
            place[1, p, h, 3 * (n_heads + h) + p] = -1.0
    ones[0, 0, 3 * n_heads:6 * n_heads] = 1.0
    ones[1, 0, :3 * n_heads] = 1.0
    return jnp.asarray(place, BF16), jnp.asarray(ones, F32)


def _proj(x2, seq, gnorm, shift, scale, w, layer, gain, n_norm_tiles, forget=None,
          out_scale=1.0):
    m, d = x2.shape
    n = w.shape[-1]
    tm = PROJ_ROWS
    tn = PROJ_COLS
    assert seq % tm == 0 and n % tn == 0
    tpb = seq // tm
    with_forget = forget is not None

    in_specs = [
        pl.BlockSpec((tm, d), lambda i, j: (i, 0)),
        pl.BlockSpec((1, d), lambda i, j: (0, 0)),
        pl.BlockSpec((None, 1, d), lambda i, j: (i // tpb, 0, 0)),
        pl.BlockSpec((None, 1, d), lambda i, j: (i // tpb, 0, 0)),
        pl.BlockSpec((None, d, tn), lambda i, j: (layer, 0, j)),
        pl.BlockSpec((1, HEAD_DIM), lambda i, j: (0, 0)),
    ]
    args = [x2, gnorm.reshape(1, d), shift, scale, w, gain]
    out_specs = [pl.BlockSpec((tm, tn), lambda i, j: (i, j))]
    out_shape = [jax.ShapeDtypeStruct((m, n), BF16)]
    scratch = [pltpu.VMEM((tm, d), BF16), pltpu.VMEM((tm, LANES), F32),
               pltpu.VMEM((tm, tn), F32)]
    if with_forget:
        wf_pad, bf_pad, n_heads = forget
        tri = (lax.broadcasted_iota(jnp.int32, (tm, tm), 0)
               >= lax.broadcasted_iota(jnp.int32, (tm, tm), 1)).astype(BF16)
        place, ones = _fox_bias_lanes(n_heads)
        in_specs += [
            pl.BlockSpec((d, LANES), lambda i, j: (0, 0)),
            pl.BlockSpec((1, LANES), lambda i, j: (0, 0)),
            pl.BlockSpec((tm, tm), lambda i, j: (0, 0)),
            pl.BlockSpec((2, 3, LANES, LANES), lambda i, j: (0, 0, 0, 0)),
            pl.BlockSpec((2, 1, LANES), lambda i, j: (0, 0, 0)),
        ]
        args += [wf_pad, bf_pad, tri, place, ones]
        out_specs += [
            pl.BlockSpec((tm, LANES), lambda i, j: (i, 0)),
            pl.BlockSpec((tm, LANES), lambda i, j: (i, 0)),
        ]
        out_shape += [
            jax.ShapeDtypeStruct((m, LANES), BF16),
            jax.ShapeDtypeStruct((m, LANES), BF16),
        ]
        scratch.append(pltpu.VMEM((1, LANES), F32))

    kern = functools.partial(_proj_kernel, n_norm_tiles=n_norm_tiles, forget=with_forget,
                             tiles_per_batch=tpb, out_scale=out_scale)
    outs = pl.pallas_call(
        kern,
        grid=(m // tm, n // tn),
        in_specs=in_specs,
        out_specs=out_specs,
        out_shape=out_shape,
        scratch_shapes=scratch,
        compiler_params=_compiler_params(("arbitrary", "arbitrary")),
        name="proj",
    )(*args)
    return outs if with_forget else outs[0]


def _swiglu_kernel(x_ref, gn_ref, sh_ref, sc_ref, wg_ref, wu_ref, o_ref, h_scr, inv_scr):
    @pl.when(pl.program_id(1) == 0)
    def _():
        _fill_lhs(x_ref, gn_ref, sh_ref, sc_ref, h_scr, inv_scr)

    h = h_scr[...]
    g = jnp.dot(h, wg_ref[...], preferred_element_type=F32)
    u = jnp.dot(h, wu_ref[...], preferred_element_type=F32)
    o_ref[...] = (_silu(g) * u).astype(o_ref.dtype)


def _swiglu_in(x2, seq, gnorm, shift, scale, w, layer):
    m, d = x2.shape
    f = w.shape[-1] // 2
    tm = PROJ_ROWS
    tf = _pick_tile(f, (512, 256, 128))
    assert seq % tm == 0
    tpb = seq // tm
    nf = f // tf
    return pl.pallas_call(
        _swiglu_kernel,
        grid=(m // tm, nf),
        in_specs=[
            pl.BlockSpec((tm, d), lambda i, j: (i, 0)),
            pl.BlockSpec((1, d), lambda i, j: (0, 0)),
            pl.BlockSpec((None, 1, d), lambda i, j: (i // tpb, 0, 0)),
            pl.BlockSpec((None, 1, d), lambda i, j: (i // tpb, 0, 0)),
            pl.BlockSpec((None, d, tf), lambda i, j: (layer, 0, j)),
            pl.BlockSpec((None, d, tf), lambda i, j: (layer, 0, nf + j)),
        ],
        out_specs=pl.BlockSpec((tm, tf), lambda i, j: (i, j)),
        out_shape=jax.ShapeDtypeStruct((m, f), BF16),
        scratch_shapes=[pltpu.VMEM((tm, d), BF16), pltpu.VMEM((tm, LANES), F32)],
        compiler_params=_compiler_params(("parallel", "arbitrary")),
        name="swiglu_in",
    )(x2, gnorm.reshape(1, d), shift, scale, w, w)


def _mmres_kernel(a_ref, w_ref, x_ref, gt_ref, o_ref):
    y = jnp.dot(a_ref[...], w_ref[...], preferred_element_type=F32)
    o_ref[...] = x_ref[...] + gt_ref[...] * y


def _mm_residual(a, w, layer, x2, seq, gate):
    m, k = a.shape
    n = w.shape[-1]
    tm = RESIDUAL_ROWS
    assert seq % tm == 0
    tpb = seq // tm
    return pl.pallas_call(
        _mmres_kernel,
        grid=(m // tm,),
        in_specs=[
            pl.BlockSpec((tm, k), lambda i: (i, 0)),
            pl.BlockSpec((None, k, n), lambda i: (layer, 0, 0), pipeline_mode=pl.Buffered(1)),
            pl.BlockSpec((tm, n), lambda i: (i, 0)),
            pl.BlockSpec((None, 1, n), lambda i: (i // tpb, 0, 0)),
        ],
        out_specs=pl.BlockSpec((tm, n), lambda i: (i, 0)),
        out_shape=jax.ShapeDtypeStruct((m, n), F32),
        compiler_params=_compiler_params(("parallel",)),
        name="mm_residual",
    )(a, w, x2, gate)


def _dilated_kernel(q_ref, kc_ref, kp_ref, vc_ref, vp_ref, o_ref, lse_ref, o_scr, *,
                    n_back, n_heads, dilation):
    blk = pl.program_id(1)
    res = pl.program_id(2)
    width = n_heads * HEAD_DIM
    qi = lax.broadcasted_iota(jnp.int32, (ATT_BLOCK, 2 * ATT_BLOCK), 0)
    kj = lax.broadcasted_iota(jnp.int32, (ATT_BLOCK, 2 * ATT_BLOCK), 1)
    dist = qi + ATT_BLOCK - kj
    mask = (dist >= 0) & (dist <= n_back) & ((kj >= ATT_BLOCK) | (blk > 0))
    lane = lax.broadcasted_iota(jnp.int32, (ATT_BLOCK, LANES), 1)
    contract_last = (((1,), (1,)), ((), ()))

    q_all = q_ref[...].reshape(ATT_BLOCK, width)
    k_all = jnp.concatenate([kp_ref[...].reshape(ATT_BLOCK, width),
                             kc_ref[...].reshape(ATT_BLOCK, width)], axis=0)
    v_all = jnp.concatenate([vp_ref[...].reshape(ATT_BLOCK, width),
                             vc_ref[...].reshape(ATT_BLOCK, width)], axis=0)
    if dilation == 1:
        rows = slice(None)
    else:
        rows = pl.ds(res, ATT_BLOCK, stride=dilation)

    lse_tile = jnp.zeros((ATT_BLOCK, LANES), F32)
    for h in range(n_heads):
        cols = slice(h * HEAD_DIM, (h + 1) * HEAD_DIM)
        s2 = lax.dot_general(q_all[:, cols], k_all[:, cols], contract_last,
                             preferred_element_type=F32)
        s2 = jnp.where(mask, s2, NEG_INF)
        mx2 = jnp.max(s2, axis=-1, keepdims=True)
        p = jnp.exp2(s2 - mx2)
        den = jnp.sum(p, axis=-1, keepdims=True)
        o = jnp.dot(p.astype(BF16), v_all[:, cols], preferred_element_type=F32)
        o_scr[h, rows, :] = o / den
        lse_tile = jnp.where(lane == h, mx2 * LN2 + jnp.log(den), lse_tile)
    lse_ref[rows, :] = lse_tile

    @pl.when(res == dilation - 1)
    def _():
        for h in range(n_heads):
            o_ref[:, h * HEAD_DIM:(h + 1) * HEAD_DIM] = o_scr[h].astype(o_ref.dtype)


def _dilated_group(qkv, bsz, seq, group, n_groups, width, window, dilation):
    n_back = window // dilation
    sub_len = seq // dilation
    assert sub_len % ATT_BLOCK == 0
    nb = sub_len // ATT_BLOCK
    n_cols = qkv.shape[1]
    tpb = seq // DEINT_ROWS
    per_residue = DEINT_ROWS // dilation

    if per_residue >= ATT_BLOCK:
        bpt = per_residue // ATT_BLOCK
        view = qkv.reshape(bsz, tpb, dilation, bpt, ATT_BLOCK, n_cols)
        block = (None, None, None, None, ATT_BLOCK, width)

        def rows_index(b, n, r):
            return (b, n // bpt, r, n % bpt, 0)
    else:
        pieces = ATT_BLOCK // per_residue
        view = qkv.reshape(bsz, tpb, dilation, per_residue, n_cols)
        block = (None, pieces, None, per_residue, width)

        def rows_index(b, n, r):
            return (b, n, r, 0)

    def spec(part, prev):
        def index(b, n, r):
            nn = jnp.maximum(n - 1, 0) if prev else n
            return rows_index(b, nn, r) + (part * n_groups + group,)
        return pl.BlockSpec(block, index)

    span = ATT_BLOCK * dilation
    o, lse = pl.pallas_call(
        functools.partial(_dilated_kernel, n_back=n_back, n_heads=width // HEAD_DIM,
                          dilation=dilation),
        grid=(bsz, nb, dilation),
        in_specs=[spec(0, False), spec(1, False), spec(1, True), spec(2, False), spec(2, True)],
        out_specs=[
            pl.BlockSpec((None, span, width), lambda b, n, r: (b, n, 0)),
            pl.BlockSpec((None, span, LANES), lambda b, n, r: (b, n, 0)),
        ],
        out_shape=[
            jax.ShapeDtypeStruct((bsz, seq, width), BF16),
            jax.ShapeDtypeStruct((bsz, seq, LANES), F32),
        ],
        scratch_shapes=[pltpu.VMEM((width // HEAD_DIM, span, HEAD_DIM), F32)],
        compiler_params=_compiler_params(("parallel", "arbitrary", "arbitrary")),
        name="dilated_attention",
    )(view, view, view, view, view)
    return o.reshape(bsz * seq, width), lse.reshape(bsz * seq, LANES)


def _mix_out_kernel(*refs, n_groups, n_heads, row_chunk):
    o_refs = refs[:n_groups]
    lse_refs = refs[n_groups:2 * n_groups]
    w_ref, x_ref, gt_ref, out_ref, lhs_scr = refs[2 * n_groups:]
    tm = x_ref.shape[0]

    def body(r, carry):
        rows = pl.ds(pl.multiple_of(r * row_chunk, row_chunk), row_chunk)
        lses = [ref[rows, :] for ref in lse_refs]
        mx = functools.reduce(jnp.maximum, lses)
        es = [jnp.exp(l - mx) for l in lses]
        inv = 1.0 / functools.reduce(lambda a, b: a + b, es)
        alphas = [e * inv for e in es]
        for h in range(n_heads):
            cols = slice(h * HEAD_DIM, (h + 1) * HEAD_DIM)
            mixed = None
            for g in range(n_groups):
                term = alphas[g][:, h:h + 1] * o_refs[g][rows, cols].astype(F32)
                mixed = term if mixed is None else mixed + term
            lhs_scr[rows, cols] = mixed.astype(BF16)
        return carry

    lax.fori_loop(0, tm // row_chunk, body, 0)
    y = jnp.dot(lhs_scr[...], w_ref[...], preferred_element_type=F32)
    out_ref[...] = x_ref[...] + gt_ref[...] * y


def _mix_out(o_list, lse_list, w, layer, x2, seq, gate):
    m, width = o_list[0].shape
    n = w.shape[-1]
    n_groups = len(o_list)
    tm = _pick_tile(seq, (512, 256, 128))
    tpb = seq // tm
    return pl.pallas_call(
        functools.partial(_mix_out_kernel, n_groups=n_groups, n_heads=width // HEAD_DIM,
                          row_chunk=128),
        grid=(m // tm,),
        in_specs=(
            [pl.BlockSpec((tm, width), lambda i: (i, 0))] * n_groups
            + [pl.BlockSpec((tm, LANES), lambda i: (i, 0))] * n_groups
            + [
                pl.BlockSpec((None, width, n), lambda i: (layer, 0, 0)),
                pl.BlockSpec((tm, n), lambda i: (i, 0)),
                pl.BlockSpec((None, 1, n), lambda i: (i // tpb, 0, 0)),
            ]
        ),
        out_specs=pl.BlockSpec((tm, n), lambda i: (i, 0)),
        out_shape=jax.ShapeDtypeStruct((m, n), F32),
        scratch_shapes=[pltpu.VMEM((tm, width), BF16)],
        compiler_params=_compiler_params(("parallel",)),
        name="mix_out",
    )(*o_list, *lse_list, w, x2, gate)


def _fox_kernel(q_ref, k_ref, v_ref, fq_ref, fk_ref, o_ref, *, blk, n_heads, heads_per_step):
    head0 = pl.program_id(1) * heads_per_step
    qt = pl.program_id(2)
    lane = lax.broadcasted_iota(jnp.int32, (1, LANES), 1)
    fq_all = fq_ref[...]
    contract_last = (((1,), (1,)), ((), ()))

    qs = []
    for hh in range(heads_per_step):
        off = (lane % (3 * n_heads)) - 3 * (head0 + hh)
        keep = jnp.where(off >= 0, jnp.where(off < 3, 1.0, 0.0), 0.0)
        keep = jnp.where(lane < 6 * n_heads, keep, 0.0).astype(BF16)
        qs.append(jnp.concatenate([q_ref[:, hh * HEAD_DIM:(hh + 1) * HEAD_DIM],
                                   fq_all * keep], axis=1))

    def step(kb, carry, masked):
        rows = pl.ds(pl.multiple_of(kb * blk, blk), blk)
        fk = fk_ref[rows, :]
        out = []
        for hh in range(heads_per_step):
            m_run, l_run, acc = carry[hh]
            cols = slice(hh * HEAD_DIM, (hh + 1) * HEAD_DIM)
            k = jnp.concatenate([k_ref[rows, cols], fk], axis=1)
            s2 = lax.dot_general(qs[hh], k, contract_last, preferred_element_type=F32)
            if masked:
                qi = lax.broadcasted_iota(jnp.int32, (blk, blk), 0)
                kj = lax.broadcasted_iota(jnp.int32, (blk, blk), 1)
                s2 = jnp.where(kj <= qi, s2, NEG_INF)
            m_new = jnp.maximum(m_run, jnp.max(s2, axis=-1, keepdims=True))
            alpha = jnp.exp2(m_run - m_new)
            p = jnp.exp2(s2 - m_new)
            l_new = alpha * l_run + jnp.sum(p, axis=-1, keepdims=True)
            acc_new = alpha * acc + jnp.dot(p.astype(BF16), v_ref[rows, cols],
                                            preferred_element_type=F32)
            out.append((m_new, l_new, acc_new))
        return tuple(out)

    init = tuple((jnp.full((blk, 1), NEG_INF, F32), jnp.zeros((blk, 1), F32),
                  jnp.zeros((blk, HEAD_DIM), F32)) for _ in range(heads_per_step))
    carry = lax.fori_loop(0, qt, lambda kb, c: step(kb, c, False), init)
    final = step(qt, carry, True)
    for hh in range(heads_per_step):
        _, l_fin, acc = final[hh]
        o_ref[:, hh * HEAD_DIM:(hh + 1) * HEAD_DIM] = (acc / l_fin).astype(o_ref.dtype)


def _fox_attention(q, kv, f_query, f_keys, bsz, seq, n_heads):
    blk = _pick_tile(seq, (FOX_BLOCK, 256, 128))
    nkb = seq // blk
    width = n_heads * HEAD_DIM
    hps = FOX_HEADS_PER_STEP
    assert n_heads % hps == 0
    n_hg = n_heads // hps
    qv = q.reshape(bsz, seq, width)
    kvv = kv.reshape(bsz, seq, 2 * width)
    o = pl.pallas_call(
        functools.partial(_fox_kernel, blk=blk, n_heads=n_heads, heads_per_step=hps),
        grid=(bsz, n_hg, nkb),
        in_specs=[
            pl.BlockSpec((None, blk, hps * HEAD_DIM), lambda b, h, t: (b, t, h)),
            pl.BlockSpec((None, seq, hps * HEAD_DIM), lambda b, h, t: (b, 0, h)),
            pl.BlockSpec((None, seq, hps * HEAD_DIM), lambda b, h, t: (b, 0, n_hg + h)),
            pl.BlockSpec((None, blk, LANES), lambda b, h, t: (b, t, 0)),
            pl.BlockSpec((None, seq, LANES), lambda b, h, t: (b, 0, 0)),
        ],
        out_specs=pl.BlockSpec((None, blk, hps * HEAD_DIM), lambda b, h, t: (b, t, h)),
        out_shape=jax.ShapeDtypeStruct((bsz, seq, width), BF16),
        compiler_params=_compiler_params(("parallel", "parallel", "arbitrary")),
        name="fox_attention",
    )(qv, kvv, kvv, f_query.reshape(bsz, seq, LANES), f_keys.reshape(bsz, seq, LANES))
    return o.reshape(bsz * seq, width)


def _rotary_lane_order():
    half = ROT_DIM // 2
    mid = HEAD_DIM // 2
    return np.concatenate([np.arange(0, half), np.arange(ROT_DIM, mid + half),
                           np.arange(half, ROT_DIM), np.arange(mid + half, HEAD_DIM)])


def _rope_tables(seq, dilations):
    half = ROT_DIM // 2
    mid = HEAD_DIM // 2
    inv = ROPE_THETA ** (-jnp.arange(0, ROT_DIM, 2, dtype=F32) / ROT_DIM)
    ang = jnp.arange(seq, dtype=F32)[:, None] * inv[None, :]
    cos, sin = jnp.cos(ang), jnp.sin(ang)
    ones = jnp.ones((seq, mid - half), F32)
    zeros = jnp.zeros((seq, mid - half), F32)
    cos_t = jnp.concatenate([cos, ones, cos, ones], axis=-1)
    sin_t = jnp.concatenate([-sin, zeros, sin, zeros], axis=-1)

    def deinterleave(t, d):
        t = t.reshape(seq // DEINT_ROWS, DEINT_ROWS // d, d, HEAD_DIM)
        return jnp.swapaxes(t, 1, 2).reshape(seq, HEAD_DIM)

    return (jnp.stack([deinterleave(cos_t, d) for d in dilations]),
            jnp.stack([deinterleave(sin_t, d) for d in dilations]))


def _split_mods(mods, parts):
    bsz, n = mods.shape
    d = n // parts
    return [mods[:, p * d:(p + 1) * d].reshape(bsz, 1, d) for p in range(parts)]


def kernel(x, c, w_ada, b_ada, g_norm_attn, g_norm_ffn, w_qkv_a, g_qk_a, w_o_a, w_ada_kv, b_ada_kv, g_norm_kv, w_kv, g_k_b, w_f, b_f, w_q_b, g_q_b, w_o_b, w_ffn_in, w_ffn_out):
    bsz, seq, d = x.shape
    depth = w_ada.shape[0]
    n_a = w_qkv_a.shape[0]
    n_groups = g_qk_a.shape[2]
    width_a = w_o_a.shape[1]
    n_heads_b = w_f.shape[1]
    assert n_groups == len(DIL_CONFIGS) and n_heads_b <= LANES and width_a == PROJ_COLS
    dilations = tuple(dl for _, dl in DIL_CONFIGS)

    x2 = x.reshape(bsz * seq, d)
    rope_tables = _rope_tables(seq, dilations)

    w_qkv = _qkv_weights(w_qkv_a, n_groups)
    g_qk = g_qk_a[..., _rotary_lane_order()]
    w_o_a = w_o_a.astype(BF16)
    w_kv = w_kv.astype(BF16)[None]
    w_q_b = w_q_b.astype(BF16)
    w_o_b = w_o_b.astype(BF16)
    w_ffn_in = w_ffn_in.astype(BF16)
    w_ffn_out = w_ffn_out.astype(BF16)
    wf_pad = jnp.pad(w_f, ((0, 0), (0, LANES - n_heads_b))).astype(BF16)
    bf_pad = jnp.pad(b_f, (0, LANES - n_heads_b)).reshape(1, LANES)

    kv = f_keys = f_query = None
    for layer in range(depth):
        sh_a, sc_a, gt_a, sh_f, sc_f, gt_f = _split_mods(_mods(c, w_ada, b_ada, layer), 6)
        if layer < n_a:
            gains = g_qk[layer].reshape(2 * n_groups, 1, HEAD_DIM)
            h_var = _deint_lhs(x2, seq, g_norm_attn[layer], sh_a, sc_a, dilations)
            qkv = _qkv_proj(h_var, seq, w_qkv, layer, gains, rope_tables)
            outs = [_dilated_group(qkv, bsz, seq, g, n_groups, width_a, window, dilation)
                    for g, (window, dilation) in enumerate(DIL_CONFIGS)]
            x2 = _mix_out([o for o, _ in outs], [l for _, l in outs], w_o_a, layer, x2, seq, gt_a)
        else:
            i = layer - n_a
            q = _proj(x2, seq, g_norm_attn[layer], sh_a, sc_a, w_q_b, i,
                      g_q_b[i].reshape(1, HEAD_DIM),
                      n_norm_tiles=w_q_b.shape[-1] // PROJ_COLS,
                      out_scale=HEAD_DIM ** -0.5 * LOG2E)
            o = _fox_attention(q, kv, f_query, f_keys, bsz, seq, n_heads_b)
            x2 = _mm_residual(o, w_o_b, i, x2, seq, gt_a)
        a = _swiglu_in(x2, seq, g_norm_ffn[layer], sh_f, sc_f, w_ffn_in, layer)
        x2 = _mm_residual(a, w_ffn_out, layer, x2, seq, gt_f)
        if layer == n_a - 1:
            sh_kv, sc_kv = _split_mods(_mods(c, w_ada_kv[None], b_ada_kv[None], 0), 2)
            kv, f_keys, f_query = _proj(
                x2, seq, g_norm_kv, sh_kv, sc_kv, w_kv, 0, g_k_b.reshape(1, HEAD_DIM),
                n_norm_tiles=w_kv.shape[-1] // 2 // PROJ_COLS, forget=(wf_pad, bf_pad, n_heads_b))
    return x2.reshape(bsz, seq, d)
```

```python
import functools
import math

import numpy as np
import jax
import jax.numpy as jnp
from jax import lax
from jax.experimental import pallas as pl
from jax.experimental.pallas import tpu as pltpu

HEAD_DIM = 128
DIL_CONFIGS = ((128, 1), (512, 4), (2048, 16))
ROT_DIM = HEAD_DIM // 4
ROPE_THETA = 500000.0
ATT_BLOCK = 128
EPS = 1e-6
NEG_INF = -1e30
LOG2E = math.log2(math.e)
LN2 = math.log(2.0)

LANES = 128
BF16_SUBLANES = 16
VMEM_LIMIT_BYTES = 56 * 1024 * 1024

PROJ_ROWS = 1024
PROJ_COLS = 1024
RESIDUAL_ROWS = 512
FILL_ROWS = 128
DEINT_ROWS = 512
PIECE_ROWS = 256
FOX_BLOCK = 512
FOX_KEY_BLOCK = 512
FOX_HEADS_PER_STEP = 4

F32 = jnp.float32
BF16 = jnp.bfloat16


def _compiler_params(semantics):
    return pltpu.CompilerParams(dimension_semantics=semantics, vmem_limit_bytes=VMEM_LIMIT_BYTES)


def _pick_tile(n, candidates):
    for t in candidates:
        if n % t == 0:
            return t
    raise ValueError(f"no tile in {candidates} divides {n}")


def _head_rmsnorm(a, gain):
    return a * lax.rsqrt(jnp.mean(a * a, axis=-1, keepdims=True) + EPS) * gain


def _silu(v):
    return v * jax.nn.sigmoid(v)


def _mods_kernel(c_ref, w_ref, b_ref, o_ref):
    c_act = _silu(c_ref[...])
    o_ref[...] = jnp.dot(c_act.astype(BF16), w_ref[...].astype(BF16),
                         preferred_element_type=F32) + b_ref[...]


def _mods(c, w, b, layer):
    bsz, d = c.shape
    n = w.shape[-1]
    tn = _pick_tile(n, (512, 256, 128))
    return pl.pallas_call(
        _mods_kernel,
        grid=(n // tn,),
        in_specs=[
            pl.BlockSpec((bsz, d), lambda j: (0, 0)),
            pl.BlockSpec((None, d, tn), lambda j: (layer, 0, j)),
            pl.BlockSpec((None, 1, tn), lambda j: (layer, 0, j)),
        ],
        out_specs=pl.BlockSpec((bsz, tn), lambda j: (0, j)),
        out_shape=jax.ShapeDtypeStruct((bsz, n), F32),
        compiler_params=_compiler_params(("parallel",)),
        name="mods",
    )(c, w, b.reshape(b.shape[0], 1, n))


def _normmod_rows(x_ref, rows, gain, mul, shift):
    xf = x_ref[rows, :]
    y = xf * lax.rsqrt(jnp.mean(xf * xf, axis=-1, keepdims=True) + EPS) * gain
    return y * mul + shift


def _row_pieces(tm):
    return [slice(r, r + PIECE_ROWS) for r in range(0, tm, PIECE_ROWS)]


def _lhs_pieces(first, x_ref, gn_ref, sh_ref, sc_ref, h_scr):
    gain = gn_ref[...]
    mul = 1.0 + sc_ref[...]
    shift = sh_ref[...]
    for rows in _row_pieces(h_scr.shape[0]):
        if first:
            h_scr[rows, :] = _normmod_rows(x_ref, rows, gain, mul, shift).astype(BF16)
        yield rows, h_scr[rows, :]


def _deint_lhs_kernel(x_ref, gn_ref, sh_ref, sc_ref, o_ref, slab_scr, *, dilations):
    tm, d_model = x_ref.shape
    n_slabs = d_model // LANES
    gain = gn_ref[...]
    mul = 1.0 + sc_ref[...]
    shift = sh_ref[...]

    def natural(it, carry):
        rows = pl.ds(pl.multiple_of(it * FILL_ROWS, FILL_ROWS), FILL_ROWS)
        h = _normmod_rows(x_ref, rows, gain, mul, shift)
        for v, d in enumerate(dilations):
            if d == 1:
                o_ref[v, rows, :] = h.astype(BF16)
        for s in range(n_slabs):
            slab_scr[s, rows, :] = h[:, s * LANES:(s + 1) * LANES]
        return carry

    lax.fori_loop(0, tm // FILL_ROWS, natural, 0)

    for v, d in enumerate(dilations):
        if d == 1:
            continue
        per_residue = tm // d
        chunk = min(per_residue, FILL_ROWS)
        chunks_per_residue = per_residue // chunk

        def gather(it, carry, v=v, d=d, per_residue=per_residue, chunk=chunk,
                   chunks_per_residue=chunks_per_residue):
            r = it // chunks_per_residue
            l0 = (it % chunks_per_residue) * chunk
            src = pl.ds(r + l0 * d, chunk, stride=d)
            dst = pl.ds(pl.multiple_of(r * per_residue + l0, chunk), chunk)
            for s in range(n_slabs):
                o_ref[v, dst, s * LANES:(s + 1) * LANES] = slab_scr[s, src, :].astype(BF16)
            return carry

        lax.fori_loop(0, tm // chunk, gather, 0)


def _deint_lhs(x2, seq, gnorm, shift, scale, dilations):
    m, d = x2.shape
    tm = DEINT_ROWS
    assert seq % tm == 0 and all(tm % (dl * BF16_SUBLANES) == 0 for dl in dilations)
    tpb = seq // tm
    n_var = len(dilations)
    return pl.pallas_call(
        functools.partial(_deint_lhs_kernel, dilations=dilations),
        grid=(m // tm,),
        in_specs=[
            pl.BlockSpec((tm, d), lambda i: (i, 0)),
            pl.BlockSpec((1, d), lambda i: (0, 0)),
            pl.BlockSpec((None, 1, d), lambda i: (i // tpb, 0, 0)),
            pl.BlockSpec((None, 1, d), lambda i: (i // tpb, 0, 0)),
        ],
        out_specs=pl.BlockSpec((n_var, tm, d), lambda i: (0, i, 0)),
        out_shape=jax.ShapeDtypeStruct((n_var, m, d), BF16),
        scratch_shapes=[pltpu.VMEM((d // LANES, tm, LANES), F32)],
        compiler_params=_compiler_params(("parallel",)),
        name="deint_lhs",
    )(x2, gnorm.reshape(1, d), shift, scale)


def _dot_head_norm(pieces, w_ref, o_ref, acc_scr, gain, cos_ref=None, sin_ref=None):
    tn = acc_scr.shape[1]
    w = w_ref[...]
    done = []
    for rows, lhs in pieces:
        acc_scr[rows, :] = jnp.dot(lhs, w, preferred_element_type=F32)
        done.append(rows)
    for rows in done:
        if cos_ref is not None:
            cos = cos_ref[rows, :]
            sin = sin_ref[rows, :]
        for h in range(tn // HEAD_DIM):
            cols = slice(h * HEAD_DIM, (h + 1) * HEAD_DIM)
            y = _head_rmsnorm(acc_scr[rows, cols], gain)
            if cos_ref is not None:
                y = y * cos + pltpu.roll(y, HEAD_DIM // 2, 1) * sin
            o_ref[rows, cols] = y.astype(o_ref.dtype)


def _qkv_weight_kernel(w_ref, perm_ref, o_ref, *, n_perm_tiles):
    w = w_ref[...].astype(BF16)

    @pl.when(pl.program_id(1) < n_perm_tiles)
    def _():
        for h in range(w.shape[1] // HEAD_DIM):
            cols = slice(h * HEAD_DIM, (h + 1) * HEAD_DIM)
            o_ref[:, cols] = jnp.dot(w[:, cols], perm_ref[...],
                                     preferred_element_type=F32).astype(BF16)

    @pl.when(pl.program_id(1) >= n_perm_tiles)
    def _():
        o_ref[...] = w


def _qkv_weights(w, n_groups):
    n_layers, d, n = w.shape
    tn = PROJ_COLS
    assert n == 3 * n_groups * tn
    order = _rotary_lane_order()
    perm = np.zeros((HEAD_DIM, HEAD_DIM), np.float32)
    perm[order, np.arange(HEAD_DIM)] = 1.0
    return pl.pallas_call(
        functools.partial(_qkv_weight_kernel, n_perm_tiles=2 * n_groups),
        grid=(n_layers, n // tn),
        in_specs=[
            pl.BlockSpec((None, d, tn), lambda l, j: (l, 0, j)),
            pl.BlockSpec((HEAD_DIM, HEAD_DIM), lambda l, j: (0, 0)),
        ],
        out_specs=pl.BlockSpec((None, d, tn), lambda l, j: (l, 0, j)),
        out_shape=jax.ShapeDtypeStruct(w.shape, BF16),
        compiler_params=_compiler_params(("parallel", "parallel")),
        name="qkv_weights",
    )(w, jnp.asarray(perm, BF16))


def _qkv_kernel(h_ref, w_ref, gain_ref, cos_ref, sin_ref, o_ref, acc_scr, *, n_parts):
    part = pl.program_id(1) % n_parts

    @pl.when(part < 2)
    def _():
        gain = gain_ref[...] * jnp.where(part == 0, HEAD_DIM ** -0.5 * LOG2E, 1.0)
        pieces = [(rows, h_ref[rows, :]) for rows in _row_pieces(h_ref.shape[0])]
        _dot_head_norm(pieces, w_ref, o_ref, acc_scr, gain, cos_ref, sin_ref)

    @pl.when(part >= 2)
    def _():
        o_ref[...] = jnp.dot(h_ref[...], w_ref[...],
                             preferred_element_type=F32).astype(o_ref.dtype)


def _qkv_proj(h_var, seq, w, layer, gains, rope_tables):
    n_groups, m, d = h_var.shape
    n = w.shape[-1]
    n_parts = 3
    tm = PROJ_ROWS
    tn = PROJ_COLS
    assert seq % tm == 0 and n == n_parts * n_groups * tn
    tpb = seq // tm

    def col(j):
        return (j % n_parts) * n_groups + j // n_parts

    return pl.pallas_call(
        functools.partial(_qkv_kernel, n_parts=n_parts),
        grid=(m // tm, n_parts * n_groups),
        in_specs=[
            pl.BlockSpec((None, tm, d), lambda i, j: (j // n_parts, i, 0)),
            pl.BlockSpec((None, d, tn), lambda i, j: (layer, 0, col(j))),
            pl.BlockSpec((None, 1, HEAD_DIM),
                         lambda i, j: (jnp.minimum(col(j), 2 * n_groups - 1), 0, 0)),
            pl.BlockSpec((None, tm, HEAD_DIM), lambda i, j: (j // n_parts, i % tpb, 0)),
            pl.BlockSpec((None, tm, HEAD_DIM), lambda i, j: (j // n_parts, i % tpb, 0)),
        ],
        out_specs=pl.BlockSpec((tm, tn), lambda i, j: (i, col(j))),
        out_shape=jax.ShapeDtypeStruct((m, n), BF16),
        scratch_shapes=[pltpu.VMEM((tm, tn), F32)],
        compiler_params=_compiler_params(("parallel", "arbitrary")),
        name="qkv_proj",
    )(h_var, w, gains, *rope_tables)


def _split3_bf16(v):
    hi = v.astype(BF16)
    rem = v - hi.astype(F32)
    mid = rem.astype(BF16)
    lo = (rem - mid.astype(F32)).astype(BF16)
    return hi, mid, lo


def _proj_kernel(*refs, n_norm_tiles, forget, tiles_per_batch, out_scale):
    x_ref, gn_ref, sh_ref, sc_ref, w_ref, gain_ref = refs[:6]
    pos = 6
    if forget:
        wf_ref, bf_ref, tri_ref, place_ref, ones_ref = refs[pos:pos + 5]
        pos += 5
    o_ref = refs[pos]
    pos += 1
    if forget:
        fk_ref, fq_ref = refs[pos:pos + 2]
        pos += 2
    h_scr, acc_scr = refs[pos:pos + 2]
    pos += 2
    if forget:
        carry_scr = refs[pos]

    i = pl.program_id(0)
    j = pl.program_id(1)
    tm = h_scr.shape[0]
    gain = gain_ref[...] * out_scale

    @pl.when(j == 0)
    def _():
        pieces = _lhs_pieces(True, x_ref, gn_ref, sh_ref, sc_ref, h_scr)
        _dot_head_norm(pieces, w_ref, o_ref, acc_scr, gain)
        if forget:
            z = jnp.dot(h_scr[...], wf_ref[...], preferred_element_type=F32) + bf_ref[...]
            log_f = jnp.minimum(z, 0.0) - jnp.log1p(jnp.exp(-jnp.abs(z)))
            tri = tri_ref[...]
            cum = None
            for part in _split3_bf16(log_f):
                term = jnp.dot(tri, part, preferred_element_type=F32)
                cum = term if cum is None else cum + term

            @pl.when(i % tiles_per_batch == 0)
            def _():
                carry_scr[...] = jnp.zeros_like(carry_scr)

            cum = cum + carry_scr[...]
            carry_scr[...] = cum[tm - 1:tm, :]
            parts = _split3_bf16(cum * (-LOG2E))
            for side, out_ref in enumerate((fk_ref, fq_ref)):
                cols = ones_ref[side]
                for p, part in enumerate(parts):
                    cols = cols + jnp.dot(part, place_ref[side, p], preferred_element_type=F32)
                out_ref[...] = cols.astype(BF16)

    @pl.when((j > 0) & (j < n_norm_tiles))
    def _():
        pieces = _lhs_pieces(False, x_ref, gn_ref, sh_ref, sc_ref, h_scr)
        _dot_head_norm(pieces, w_ref, o_ref, acc_scr, gain)

    @pl.when(j >= n_norm_tiles)
    def _():
        o_ref[...] = jnp.dot(h_scr[...], w_ref[...],
                             preferred_element_type=F32).astype(o_ref.dtype)


def _fox_bias_lanes(n_heads):
    assert 6 * n_heads <= LANES
    place = np.zeros((2, 3, LANES, LANES), np.float32)
    ones = np.zeros((2, 1, LANES), np.float32)
    for h in range(n_heads):
        for p in range(3):
            place[0, p, h, 3 * h + p] = 1.0
            place[1, p, h, 3 * (n_heads + h) + p] = -1.0
    ones[0, 0, 3 * n_heads:6 * n_heads] = 1.0
    ones[1, 0, :3 * n_heads] = 1.0
    return jnp.asarray(place, BF16), jnp.asarray(ones, F32)


def _proj(x2, seq, gnorm, shift, scale, w, layer, gain, n_norm_tiles, forget=None,
          out_scale=1.0):
    m, d = x2.shape
    n = w.shape[-1]
    tm = PROJ_ROWS
    tn = PROJ_COLS
    assert seq % tm == 0 and n % tn == 0
    tpb = seq // tm
    with_forget = forget is not None

    in_specs = [
        pl.BlockSpec((tm, d), lambda i, j: (i, 0)),
        pl.BlockSpec((1, d), lambda i, j: (0, 0)),
        pl.BlockSpec((None, 1, d), lambda i, j: (i // tpb, 0, 0)),
        pl.BlockSpec((None, 1, d), lambda i, j: (i // tpb, 0, 0)),
        pl.BlockSpec((None, d, tn), lambda i, j: (layer, 0, j)),
        pl.BlockSpec((1, HEAD_DIM), lambda i, j: (0, 0)),
    ]
    args = [x2, gnorm.reshape(1, d), shift, scale, w, gain]
    out_specs = [pl.BlockSpec((tm, tn), lambda i, j: (i, j))]
    out_shape = [jax.ShapeDtypeStruct((m, n), BF16)]
    scratch = [pltpu.VMEM((tm, d), BF16), pltpu.VMEM((tm, tn), F32)]
    if with_forget:
        wf_pad, bf_pad, n_heads = forget
        tri = (lax.broadcasted_iota(jnp.int32, (tm, tm), 0)
               >= lax.broadcasted_iota(jnp.int32, (tm, tm), 1)).astype(BF16)
        place, ones = _fox_bias_lanes(n_heads)
        in_specs += [
            pl.BlockSpec((d, LANES), lambda i, j: (0, 0)),
            pl.BlockSpec((1, LANES), lambda i, j: (0, 0)),
            pl.BlockSpec((tm, tm), lambda i, j: (0, 0)),
            pl.BlockSpec((2, 3, LANES, LANES), lambda i, j: (0, 0, 0, 0)),
            pl.BlockSpec((2, 1, LANES), lambda i, j: (0, 0, 0)),
        ]
        args += [wf_pad, bf_pad, tri, place, ones]
        out_specs += [
            pl.BlockSpec((tm, LANES), lambda i, j: (i, 0)),
            pl.BlockSpec((tm, LANES), lambda i, j: (i, 0)),
        ]
        out_shape += [
            jax.ShapeDtypeStruct((m, LANES), BF16),
            jax.ShapeDtypeStruct((m, LANES), BF16),
        ]
        scratch.append(pltpu.VMEM((1, LANES), F32))

    kern = functools.partial(_proj_kernel, n_norm_tiles=n_norm_tiles, forget=with_forget,
                             tiles_per_batch=tpb, out_scale=out_scale)
    outs = pl.pallas_call(
        kern,
        grid=(m // tm, n // tn),
        in_specs=in_specs,
        out_specs=out_specs,
        out_shape=out_shape,
        scratch_shapes=scratch,
        compiler_params=_compiler_params(("arbitrary", "arbitrary")),
        name="proj",
    )(*args)
    return outs if with_forget else outs[0]


def _swiglu_kernel(x_ref, gn_ref, sh_ref, sc_ref, wg_ref, wu_ref, o_ref, h_scr):
    def gated(h):
        g = jnp.dot(h, wg_ref[...], preferred_element_type=F32)
        u = jnp.dot(h, wu_ref[...], preferred_element_type=F32)
        return (_silu(g) * u).astype(o_ref.dtype)

    @pl.when(pl.program_id(1) == 0)
    def _():
        for rows, h in _lhs_pieces(True, x_ref, gn_ref, sh_ref, sc_ref, h_scr):
            o_ref[rows, :] = gated(h)

    @pl.when(pl.program_id(1) > 0)
    def _():
        o_ref[...] = gated(h_scr[...])


def _swiglu_in(x2, seq, gnorm, shift, scale, w, layer):
    m, d = x2.shape
    f = w.shape[-1] // 2
    tm = PROJ_ROWS
    tf = _pick_tile(f, (512, 256, 128))
    assert seq % tm == 0
    tpb = seq // tm
    nf = f // tf
    return pl.pallas_call(
        _swiglu_kernel,
        grid=(m // tm, nf),
        in_specs=[
            pl.BlockSpec((tm, d), lambda i, j: (i, 0)),
            pl.BlockSpec((1, d), lambda i, j: (0, 0)),
            pl.BlockSpec((None, 1, d), lambda i, j: (i // tpb, 0, 0)),
            pl.BlockSpec((None, 1, d), lambda i, j: (i // tpb, 0, 0)),
            pl.BlockSpec((None, d, tf), lambda i, j: (layer, 0, j)),
            pl.BlockSpec((None, d, tf), lambda i, j: (layer, 0, nf + j)),
        ],
        out_specs=pl.BlockSpec((tm, tf), lambda i, j: (i, j)),
        out_shape=jax.ShapeDtypeStruct((m, f), BF16),
        scratch_shapes=[pltpu.VMEM((tm, d), BF16)],
        compiler_params=_compiler_params(("parallel", "arbitrary")),
        name="swiglu_in",
    )(x2, gnorm.reshape(1, d), shift, scale, w, w)


def _mmres_kernel(a_ref, w_ref, x_ref, gt_ref, o_ref):
    y = jnp.dot(a_ref[...], w_ref[...], preferred_element_type=F32)
    o_ref[...] = x_ref[...] + gt_ref[...] * y


def _mm_residual(a, w, layer, x2, seq, gate):
    m, k = a.shape
    n = w.shape[-1]
    tm = RESIDUAL_ROWS
    assert seq % tm == 0
    tpb = seq // tm
    return pl.pallas_call(
        _mmres_kernel,
        grid=(m // tm,),
        in_specs=[
            pl.BlockSpec((tm, k), lambda i: (i, 0)),
            pl.BlockSpec((None, k, n), lambda i: (layer, 0, 0), pipeline_mode=pl.Buffered(1)),
            pl.BlockSpec((tm, n), lambda i: (i, 0)),
            pl.BlockSpec((None, 1, n), lambda i: (i // tpb, 0, 0)),
        ],
        out_specs=pl.BlockSpec((tm, n), lambda i: (i, 0)),
        out_shape=jax.ShapeDtypeStruct((m, n), F32),
        compiler_params=_compiler_params(("parallel",)),
        name="mm_residual",
    )(a, w, x2, gate)


def _dilated_kernel(q_ref, kc_ref, kp_ref, vc_ref, vp_ref, o_ref, lse_ref, o_scr, *,
                    n_back, n_heads, dilation):
    blk = pl.program_id(1)
    res = pl.program_id(2)
    width = n_heads * HEAD_DIM
    qi = lax.broadcasted_iota(jnp.int32, (ATT_BLOCK, 2 * ATT_BLOCK), 0)
    kj = lax.broadcasted_iota(jnp.int32, (ATT_BLOCK, 2 * ATT_BLOCK), 1)
    dist = qi + ATT_BLOCK - kj
    mask = (dist >= 0) & (dist <= n_back) & ((kj >= ATT_BLOCK) | (blk > 0))
    lane = lax.broadcasted_iota(jnp.int32, (ATT_BLOCK, LANES), 1)
    contract_last = (((1,), (1,)), ((), ()))

    q_all = q_ref[...].reshape(ATT_BLOCK, width)
    k_all = jnp.concatenate([kp_ref[...].reshape(ATT_BLOCK, width),
                             kc_ref[...].reshape(ATT_BLOCK, width)], axis=0)
    v_all = jnp.concatenate([vp_ref[...].reshape(ATT_BLOCK, width),
                             vc_ref[...].reshape(ATT_BLOCK, width)], axis=0)
    if dilation == 1:
        rows = slice(None)
    else:
        rows = pl.ds(res, ATT_BLOCK, stride=dilation)

    lse_tile = jnp.zeros((ATT_BLOCK, LANES), F32)
    for h in range(n_heads):
        cols = slice(h * HEAD_DIM, (h + 1) * HEAD_DIM)
        s2 = lax.dot_general(q_all[:, cols], k_all[:, cols], contract_last,
                             preferred_element_type=F32)
        s2 = jnp.where(mask, s2, NEG_INF)
        mx2 = jnp.max(s2, axis=-1, keepdims=True)
        p = jnp.exp2(s2 - mx2)
        den = jnp.sum(p, axis=-1, keepdims=True)
        o = jnp.dot(p.astype(BF16), v_all[:, cols], preferred_element_type=F32)
        o_scr[h, rows, :] = o / den
        lse_tile = jnp.where(lane == h, mx2 * LN2 + jnp.log(den), lse_tile)
    lse_ref[rows, :] = lse_tile

    @pl.when(res == dilation - 1)
    def _():
        for h in range(n_heads):
            o_ref[:, h * HEAD_DIM:(h + 1) * HEAD_DIM] = o_scr[h].astype(o_ref.dtype)


def _dilated_group(qkv, bsz, seq, group, n_groups, width, window, dilation):
    n_back = window // dilation
    sub_len = seq // dilation
    assert sub_len % ATT_BLOCK == 0
    nb = sub_len // ATT_BLOCK
    n_cols = qkv.shape[1]
    tpb = seq // DEINT_ROWS
    per_residue = DEINT_ROWS // dilation

    if per_residue >= ATT_BLOCK:
        bpt = per_residue // ATT_BLOCK
        view = qkv.reshape(bsz, tpb, dilation, bpt, ATT_BLOCK, n_cols)
        block = (None, None, None, None, ATT_BLOCK, width)

        def rows_index(b, n, r):
            return (b, n // bpt, r, n % bpt, 0)
    else:
        pieces = ATT_BLOCK // per_residue
        view = qkv.reshape(bsz, tpb, dilation, per_residue, n_cols)
        block = (None, pieces, None, per_residue, width)

        def rows_index(b, n, r):
            return (b, n, r, 0)

    def spec(part, prev):
        def index(b, n, r):
            nn = jnp.maximum(n - 1, 0) if prev else n
            return rows_index(b, nn, r) + (part * n_groups + group,)
        return pl.BlockSpec(block, index)

    span = ATT_BLOCK * dilation
    o, lse = pl.pallas_call(
        functools.partial(_dilated_kernel, n_back=n_back, n_heads=width // HEAD_DIM,
                          dilation=dilation),
        grid=(bsz, nb, dilation),
        in_specs=[spec(0, False), spec(1, False), spec(1, True), spec(2, False), spec(2, True)],
        out_specs=[
            pl.BlockSpec((None, span, width), lambda b, n, r: (b, n, 0)),
            pl.BlockSpec((None, span, LANES), lambda b, n, r: (b, n, 0)),
        ],
        out_shape=[
            jax.ShapeDtypeStruct((bsz, seq, width), BF16),
            jax.ShapeDtypeStruct((bsz, seq, LANES), F32),
        ],
        scratch_shapes=[pltpu.VMEM((width // HEAD_DIM, span, HEAD_DIM), F32)],
        compiler_params=_compiler_params(("parallel", "arbitrary", "arbitrary")),
        name="dilated_attention",
    )(view, view, view, view, view)
    return o.reshape(bsz * seq, width), lse.reshape(bsz * seq, LANES)


def _mix_out_kernel(*refs, n_groups, n_heads, row_chunk):
    o_refs = refs[:n_groups]
    lse_refs = refs[n_groups:2 * n_groups]
    w_ref, x_ref, gt_ref, out_ref, lhs_scr = refs[2 * n_groups:]
    tm = x_ref.shape[0]
    w = w_ref[...]
    gate = gt_ref[...]

    for r in range(0, tm, row_chunk):
        rows = slice(r, r + row_chunk)
        lses = [ref[rows, :] for ref in lse_refs]
        mx = functools.reduce(jnp.maximum, lses)
        es = [jnp.exp(l - mx) for l in lses]
        inv = 1.0 / functools.reduce(lambda a, b: a + b, es)
        alphas = [e * inv for e in es]
        for h in range(n_heads):
            cols = slice(h * HEAD_DIM, (h + 1) * HEAD_DIM)
            mixed = None
            for g in range(n_groups):
                term = alphas[g][:, h:h + 1] * o_refs[g][rows, cols].astype(F32)
                mixed = term if mixed is None else mixed + term
            lhs_scr[rows, cols] = mixed.astype(BF16)
        y = jnp.dot(lhs_scr[rows, :], w, preferred_element_type=F32)
        out_ref[rows, :] = x_ref[rows, :] + gate * y


def _mix_out(o_list, lse_list, w, layer, x2, seq, gate):
    m, width = o_list[0].shape
    n = w.shape[-1]
    n_groups = len(o_list)
    tm = _pick_tile(seq, (512, 256, 128))
    tpb = seq // tm
    return pl.pallas_call(
        functools.partial(_mix_out_kernel, n_groups=n_groups, n_heads=width // HEAD_DIM,
                          row_chunk=128),
        grid=(m // tm,),
        in_specs=(
            [pl.BlockSpec((tm, width), lambda i: (i, 0))] * n_groups
            + [pl.BlockSpec((tm, LANES), lambda i: (i, 0))] * n_groups
            + [
                pl.BlockSpec((None, width, n), lambda i: (layer, 0, 0)),
                pl.BlockSpec((tm, n), lambda i: (i, 0)),
                pl.BlockSpec((None, 1, n), lambda i: (i // tpb, 0, 0)),
            ]
        ),
        out_specs=pl.BlockSpec((tm, n), lambda i: (i, 0)),
        out_shape=jax.ShapeDtypeStruct((m, n), F32),
        scratch_shapes=[pltpu.VMEM((tm, width), BF16)],
        compiler_params=_compiler_params(("parallel",)),
        name="mix_out",
    )(*o_list, *lse_list, w, x2, gate)


def _fox_kernel(q_ref, k_ref, v_ref, fq_ref, fk_ref, o_ref, *, blk, key_blk, n_heads,
                heads_per_step):
    head0 = pl.program_id(1) * heads_per_step
    qt = pl.program_id(2)
    lane = lax.broadcasted_iota(jnp.int32, (1, LANES), 1)
    fq_all = fq_ref[...]
    contract_last = (((1,), (1,)), ((), ()))

    qs = []
    for hh in range(heads_per_step):
        off = (lane % (3 * n_heads)) - 3 * (head0 + hh)
        keep = jnp.where(off >= 0, jnp.where(off < 3, 1.0, 0.0), 0.0)
        keep = jnp.where(lane < 6 * n_heads, keep, 0.0).astype(BF16)
        qs.append(jnp.concatenate([q_ref[:, hh * HEAD_DIM:(hh + 1) * HEAD_DIM],
                                   fq_all * keep], axis=1))

    def step(kb, carry, diag):
        rows = pl.ds(pl.multiple_of(kb * key_blk, key_blk), key_blk)
        fk = fk_ref[rows, :]
        out = []
        for hh in range(heads_per_step):
            m_run, l_run, acc = carry[hh]
            cols = slice(hh * HEAD_DIM, (hh + 1) * HEAD_DIM)
            k = jnp.concatenate([k_ref[rows, cols], fk], axis=1)
            s2 = lax.dot_general(qs[hh], k, contract_last, preferred_element_type=F32)
            if diag is not None:
                qi = lax.broadcasted_iota(jnp.int32, (blk, key_blk), 0)
                kj = lax.broadcasted_iota(jnp.int32, (blk, key_blk), 1) + diag * key_blk
                s2 = jnp.where(kj <= qi, s2, NEG_INF)
            m_new = jnp.maximum(m_run, jnp.max(s2, axis=-1, keepdims=True))
            alpha = jnp.exp2(m_run - m_new)
            p = jnp.exp2(s2 - m_new)
            l_new = alpha * l_run + jnp.sum(p, axis=-1, keepdims=True)
            acc_new = alpha * acc + jnp.dot(p.astype(BF16), v_ref[rows, cols],
                                            preferred_element_type=F32)
            out.append((m_new, l_new, acc_new))
        return tuple(out)

    init = tuple((jnp.full((blk, 1), NEG_INF, F32), jnp.zeros((blk, 1), F32),
                  jnp.zeros((blk, HEAD_DIM), F32)) for _ in range(heads_per_step))
    per_tile = blk // key_blk
    final = lax.fori_loop(0, qt * per_tile, lambda kb, c: step(kb, c, None), init)
    for dg in range(per_tile):
        final = step(qt * per_tile + dg, final, dg)
    for hh in range(heads_per_step):
        _, l_fin, acc = final[hh]
        o_ref[:, hh * HEAD_DIM:(hh + 1) * HEAD_DIM] = (acc / l_fin).astype(o_ref.dtype)


def _fox_attention(q, kv, f_query, f_keys, bsz, seq, n_heads):
    blk = _pick_tile(seq, (FOX_BLOCK, 256, 128))
    nkb = seq // blk
    width = n_heads * HEAD_DIM
    hps = FOX_HEADS_PER_STEP
    assert n_heads % hps == 0
    n_hg = n_heads // hps
    qv = q.reshape(bsz, seq, width)
    kvv = kv.reshape(bsz, seq, 2 * width)
    o = pl.pallas_call(
        functools.partial(_fox_kernel, blk=blk, key_blk=min(blk, FOX_KEY_BLOCK), n_heads=n_heads,
                          heads_per_step=hps),
        grid=(bsz, n_hg, nkb),
        in_specs=[
            pl.BlockSpec((None, blk, hps * HEAD_DIM), lambda b, h, t: (b, t, h)),
            pl.BlockSpec((None, seq, hps * HEAD_DIM), lambda b, h, t: (b, 0, h)),
            pl.BlockSpec((None, seq, hps * HEAD_DIM), lambda b, h, t: (b, 0, n_hg + h)),
            pl.BlockSpec((None, blk, LANES), lambda b, h, t: (b, t, 0)),
            pl.BlockSpec((None, seq, LANES), lambda b, h, t: (b, 0, 0)),
        ],
        out_specs=pl.BlockSpec((None, blk, hps * HEAD_DIM), lambda b, h, t: (b, t, h)),
        out_shape=jax.ShapeDtypeStruct((bsz, seq, width), BF16),
        compiler_params=_compiler_params(("parallel", "parallel", "arbitrary")),
        name="fox_attention",
    )(qv, kvv, kvv, f_query.reshape(bsz, seq, LANES), f_keys.reshape(bsz, seq, LANES))
    return o.reshape(bsz * seq, width)


def _rotary_lane_order():
    half = ROT_DIM // 2
    mid = HEAD_DIM // 2
    return np.concatenate([np.arange(0, half), np.arange(ROT_DIM, mid + half),
                           np.arange(half, ROT_DIM), np.arange(mid + half, HEAD_DIM)])


def _rope_tables(seq, dilations):
    half = ROT_DIM // 2
    mid = HEAD_DIM // 2
    inv = ROPE_THETA ** (-jnp.arange(0, ROT_DIM, 2, dtype=F32) / ROT_DIM)
    ang = jnp.arange(seq, dtype=F32)[:, None] * inv[None, :]
    cos, sin = jnp.cos(ang), jnp.sin(ang)
    ones = jnp.ones((seq, mid - half), F32)
    zeros = jnp.zeros((seq, mid - half), F32)
    cos_t = jnp.concatenate([cos, ones, cos, ones], axis=-1)
    sin_t = jnp.concatenate([-sin, zeros, sin, zeros], axis=-1)

    def deinterleave(t, d):
        t = t.reshape(seq // DEINT_ROWS, DEINT_ROWS // d, d, HEAD_DIM)
        return jnp.swapaxes(t, 1, 2).reshape(seq, HEAD_DIM)

    return (jnp.stack([deinterleave(cos_t, d) for d in dilations]),
            jnp.stack([deinterleave(sin_t, d) for d in dilations]))


def _split_mods(mods, parts):
    bsz, n = mods.shape
    d = n // parts
    return [mods[:, p * d:(p + 1) * d].reshape(bsz, 1, d) for p in range(parts)]


def kernel(x, c, w_ada, b_ada, g_norm_attn, g_norm_ffn, w_qkv_a, g_qk_a, w_o_a, w_ada_kv, b_ada_kv, g_norm_kv, w_kv, g_k_b, w_f, b_f, w_q_b, g_q_b, w_o_b, w_ffn_in, w_ffn_out):
    bsz, seq, d = x.shape
    depth = w_ada.shape[0]
    n_a = w_qkv_a.shape[0]
    n_groups = g_qk_a.shape[2]
    width_a = w_o_a.shape[1]
    n_heads_b = w_f.shape[1]
    assert n_groups == len(DIL_CONFIGS) and n_heads_b <= LANES and width_a == PROJ_COLS
    dilations = tuple(dl for _, dl in DIL_CONFIGS)

    x2 = x.reshape(bsz * seq, d)
    rope_tables = _rope_tables(seq, dilations)

    w_qkv = _qkv_weights(w_qkv_a, n_groups)
    g_qk = g_qk_a[..., _rotary_lane_order()]
    w_o_a = w_o_a.astype(BF16)
    w_kv = w_kv.astype(BF16)[None]
    w_q_b = w_q_b.astype(BF16)
    w_o_b = w_o_b.astype(BF16)
    w_ffn_in = w_ffn_in.astype(BF16)
    w_ffn_out = w_ffn_out.astype(BF16)
    wf_pad = jnp.pad(w_f, ((0, 0), (0, LANES - n_heads_b))).astype(BF16)
    bf_pad = jnp.pad(b_f, (0, LANES - n_heads_b)).reshape(1, LANES)

    kv = f_keys = f_query = None
    for layer in range(depth):
        sh_a, sc_a, gt_a, sh_f, sc_f, gt_f = _split_mods(_mods(c, w_ada, b_ada, layer), 6)
        if layer < n_a:
            gains = g_qk[layer].reshape(2 * n_groups, 1, HEAD_DIM)
            h_var = _deint_lhs(x2, seq, g_norm_attn[layer], sh_a, sc_a, dilations)
            qkv = _qkv_proj(h_var, seq, w_qkv, layer, gains, rope_tables)
            outs = [_dilated_group(qkv, bsz, seq, g, n_groups, width_a, window, dilation)
                    for g, (window, dilation) in enumerate(DIL_CONFIGS)]
            x2 = _mix_out([o for o, _ in outs], [l for _, l in outs], w_o_a, layer, x2, seq, gt_a)
        else:
            i = layer - n_a
            q = _proj(x2, seq, g_norm_attn[layer], sh_a, sc_a, w_q_b, i,
                      g_q_b[i].reshape(1, HEAD_DIM),
                      n_norm_tiles=w_q_b.shape[-1] // PROJ_COLS,
                      out_scale=HEAD_DIM ** -0.5 * LOG2E)
            o = _fox_attention(q, kv, f_query, f_keys, bsz, seq, n_heads_b)
            x2 = _mm_residual(o, w_o_b, i, x2, seq, gt_a)
        a = _swiglu_in(x2, seq, g_norm_ffn[layer], sh_f, sc_f, w_ffn_in, layer)
        x2 = _mm_residual(a, w_ffn_out, layer, x2, seq, gt_f)
        if layer == n_a - 1:
            sh_kv, sc_kv = _split_mods(_mods(c, w_ada_kv[None], b_ada_kv[None], 0), 2)
            kv, f_keys, f_query = _proj(
                x2, seq, g_norm_kv, sh_kv, sc_kv, w_kv, 0, g_k_b.reshape(1, HEAD_DIM),
                n_norm_tiles=w_kv.shape[-1] // 2 // PROJ_COLS, forget=(wf_pad, bf_pad, n_heads_b))
    return x2.reshape(bsz, seq, d)
```

```python
import functools
import math

import numpy as np
import jax
import jax.numpy as jnp
from jax import lax
from jax.experimental import pallas as pl
from jax.experimental.pallas import tpu as pltpu

HEAD_DIM = 128
DIL_CONFIGS = ((128, 1), (512, 4), (2048, 16))
ROT_DIM = HEAD_DIM // 4
ROPE_THETA = 500000.0
ATT_BLOCK = 128
EPS = 1e-6
NEG_INF = -1e30
LOG2E = math.log2(math.e)
LN2 = math.log(2.0)

LANES = 128
BF16_SUBLANES = 16
VMEM_LIMIT_BYTES = 56 * 1024 * 1024

PROJ_ROWS = 1024
PROJ_COLS = 1024
RESIDUAL_ROWS = 512
FILL_ROWS = 128
DEINT_ROWS = 512
PIECE_ROWS = 256
FOX_BLOCK = 512
FOX_PIECE_ROWS = 512
FOX_HEADS_PER_STEP = 4

F32 = jnp.float32
BF16 = jnp.bfloat16


def _compiler_params(semantics):
    return pltpu.CompilerParams(dimension_semantics=semantics, vmem_limit_bytes=VMEM_LIMIT_BYTES)


def _pick_tile(n, candidates):
    for t in candidates:
        if n % t == 0:
            return t
    raise ValueError(f"no tile in {candidates} divides {n}")


def _head_rmsnorm(a, gain):
    return a * lax.rsqrt(jnp.mean(a * a, axis=-1, keepdims=True) + EPS) * gain


def _silu(v):
    return v * jax.nn.sigmoid(v)


def _mods_kernel(c_ref, w_ref, b_ref, o_ref):
    c_act = _silu(c_ref[...])
    o_ref[...] = jnp.dot(c_act.astype(BF16), w_ref[...].astype(BF16),
                         preferred_element_type=F32) + b_ref[...]


def _mods(c, w, b, layer):
    bsz, d = c.shape
    n = w.shape[-1]
    tn = _pick_tile(n, (512, 256, 128))
    return pl.pallas_call(
        _mods_kernel,
        grid=(n // tn,),
        in_specs=[
            pl.BlockSpec((bsz, d), lambda j: (0, 0)),
            pl.BlockSpec((None, d, tn), lambda j: (layer, 0, j)),
            pl.BlockSpec((None, 1, tn), lambda j: (layer, 0, j)),
        ],
        out_specs=pl.BlockSpec((bsz, tn), lambda j: (0, j)),
        out_shape=jax.ShapeDtypeStruct((bsz, n), F32),
        compiler_params=_compiler_params(("parallel",)),
        name="mods",
    )(c, w, b.reshape(b.shape[0], 1, n))


def _normmod_rows(x_ref, rows, gain, mul, shift):
    xf = x_ref[rows, :]
    y = xf * lax.rsqrt(jnp.mean(xf * xf, axis=-1, keepdims=True) + EPS) * gain
    return y * mul + shift


def _row_pieces(tm):
    return [slice(r, r + PIECE_ROWS) for r in range(0, tm, PIECE_ROWS)]


def _lhs_pieces(first, x_ref, gn_ref, sh_ref, sc_ref, h_scr):
    gain = gn_ref[...]
    mul = 1.0 + sc_ref[...]
    shift = sh_ref[...]
    for rows in _row_pieces(h_scr.shape[0]):
        if first:
            h_scr[rows, :] = _normmod_rows(x_ref, rows, gain, mul, shift).astype(BF16)
        yield rows, h_scr[rows, :]


def _deint_lhs_kernel(x_ref, gn_ref, sh_ref, sc_ref, o_ref, slab_scr, *, dilations):
    tm, d_model = x_ref.shape
    n_slabs = d_model // LANES
    gain = gn_ref[...]
    mul = 1.0 + sc_ref[...]
    shift = sh_ref[...]

    def natural(it, carry):
        rows = pl.ds(pl.multiple_of(it * FILL_ROWS, FILL_ROWS), FILL_ROWS)
        h = _normmod_rows(x_ref, rows, gain, mul, shift)
        for v, d in enumerate(dilations):
            if d == 1:
                o_ref[v, rows, :] = h.astype(BF16)
        for s in range(n_slabs):
            slab_scr[s, rows, :] = h[:, s * LANES:(s + 1) * LANES]
        return carry

    lax.fori_loop(0, tm // FILL_ROWS, natural, 0)

    for v, d in enumerate(dilations):
        if d == 1:
            continue
        per_residue = tm // d
        chunk = min(per_residue, FILL_ROWS)
        chunks_per_residue = per_residue // chunk

        def gather(it, carry, v=v, d=d, per_residue=per_residue, chunk=chunk,
                   chunks_per_residue=chunks_per_residue):
            r = it // chunks_per_residue
            l0 = (it % chunks_per_residue) * chunk
            src = pl.ds(r + l0 * d, chunk, stride=d)
            dst = pl.ds(pl.multiple_of(r * per_residue + l0, chunk), chunk)
            for s in range(n_slabs):
                o_ref[v, dst, s * LANES:(s + 1) * LANES] = slab_scr[s, src, :].astype(BF16)
            return carry

        lax.fori_loop(0, tm // chunk, gather, 0)


def _deint_lhs(x2, seq, gnorm, shift, scale, dilations):
    m, d = x2.shape
    tm = DEINT_ROWS
    assert seq % tm == 0 and all(tm % (dl * BF16_SUBLANES) == 0 for dl in dilations)
    tpb = seq // tm
    n_var = len(dilations)
    return pl.pallas_call(
        functools.partial(_deint_lhs_kernel, dilations=dilations),
        grid=(m // tm,),
        in_specs=[
            pl.BlockSpec((tm, d), lambda i: (i, 0)),
            pl.BlockSpec((1, d), lambda i: (0, 0)),
            pl.BlockSpec((None, 1, d), lambda i: (i // tpb, 0, 0)),
            pl.BlockSpec((None, 1, d), lambda i: (i // tpb, 0, 0)),
        ],
        out_specs=pl.BlockSpec((n_var, tm, d), lambda i: (0, i, 0)),
        out_shape=jax.ShapeDtypeStruct((n_var, m, d), BF16),
        scratch_shapes=[pltpu.VMEM((d // LANES, tm, LANES), F32)],
        compiler_params=_compiler_params(("parallel",)),
        name="deint_lhs",
    )(x2, gnorm.reshape(1, d), shift, scale)


def _dot_head_norm(pieces, w_ref, o_ref, acc_scr, gain, cos_ref=None, sin_ref=None):
    tn = acc_scr.shape[1]
    w = w_ref[...]
    done = []
    for rows, lhs in pieces:
        acc_scr[rows, :] = jnp.dot(lhs, w, preferred_element_type=F32)
        done.append(rows)
    for rows in done:
        if cos_ref is not None:
            cos = cos_ref[rows, :]
            sin = sin_ref[rows, :]
        for h in range(tn // HEAD_DIM):
            cols = slice(h * HEAD_DIM, (h + 1) * HEAD_DIM)
            y = _head_rmsnorm(acc_scr[rows, cols], gain)
            if cos_ref is not None:
                y = y * cos + pltpu.roll(y, HEAD_DIM // 2, 1) * sin
            o_ref[rows, cols] = y.astype(o_ref.dtype)


def _qkv_weight_kernel(w_ref, perm_ref, o_ref, *, n_perm_tiles):
    w = w_ref[...].astype(BF16)

    @pl.when(pl.program_id(1) < n_perm_tiles)
    def _():
        for h in range(w.shape[1] // HEAD_DIM):
            cols = slice(h * HEAD_DIM, (h + 1) * HEAD_DIM)
            o_ref[:, cols] = jnp.dot(w[:, cols], perm_ref[...],
                                     preferred_element_type=F32).astype(BF16)

    @pl.when(pl.program_id(1) >= n_perm_tiles)
    def _():
        o_ref[...] = w


def _qkv_weights(w, n_groups):
    n_layers, d, n = w.shape
    tn = PROJ_COLS
    assert n == 3 * n_groups * tn
    order = _rotary_lane_order()
    perm = np.zeros((HEAD_DIM, HEAD_DIM), np.float32)
    perm[order, np.arange(HEAD_DIM)] = 1.0
    return pl.pallas_call(
        functools.partial(_qkv_weight_kernel, n_perm_tiles=2 * n_groups),
        grid=(n_layers, n // tn),
        in_specs=[
            pl.BlockSpec((None, d, tn), lambda l, j: (l, 0, j)),
            pl.BlockSpec((HEAD_DIM, HEAD_DIM), lambda l, j: (0, 0)),
        ],
        out_specs=pl.BlockSpec((None, d, tn), lambda l, j: (l, 0, j)),
        out_shape=jax.ShapeDtypeStruct(w.shape, BF16),
        compiler_params=_compiler_params(("parallel", "parallel")),
        name="qkv_weights",
    )(w, jnp.asarray(perm, BF16))


def _qkv_kernel(h_ref, w_ref, gain_ref, cos_ref, sin_ref, o_ref, acc_scr, *, n_parts):
    part = pl.program_id(1) % n_parts

    @pl.when(part < 2)
    def _():
        gain = gain_ref[...] * jnp.where(part == 0, HEAD_DIM ** -0.5 * LOG2E, 1.0)
        pieces = [(rows, h_ref[rows, :]) for rows in _row_pieces(h_ref.shape[0])]
        _dot_head_norm(pieces, w_ref, o_ref, acc_scr, gain, cos_ref, sin_ref)

    @pl.when(part >= 2)
    def _():
        o_ref[...] = jnp.dot(h_ref[...], w_ref[...],
                             preferred_element_type=F32).astype(o_ref.dtype)


def _qkv_proj(h_var, seq, w, layer, gains, rope_tables):
    n_groups, m, d = h_var.shape
    n = w.shape[-1]
    n_parts = 3
    tm = PROJ_ROWS
    tn = PROJ_COLS
    assert seq % tm == 0 and n == n_parts * n_groups * tn
    tpb = seq // tm

    def col(j):
        return (j % n_parts) * n_groups + j // n_parts

    return pl.pallas_call(
        functools.partial(_qkv_kernel, n_parts=n_parts),
        grid=(m // tm, n_parts * n_groups),
        in_specs=[
            pl.BlockSpec((None, tm, d), lambda i, j: (j // n_parts, i, 0)),
            pl.BlockSpec((None, d, tn), lambda i, j: (layer, 0, col(j))),
            pl.BlockSpec((None, 1, HEAD_DIM),
                         lambda i, j: (jnp.minimum(col(j), 2 * n_groups - 1), 0, 0)),
            pl.BlockSpec((None, tm, HEAD_DIM), lambda i, j: (j // n_parts, i % tpb, 0)),
            pl.BlockSpec((None, tm, HEAD_DIM), lambda i, j: (j // n_parts, i % tpb, 0)),
        ],
        out_specs=pl.BlockSpec((tm, tn), lambda i, j: (i, col(j))),
        out_shape=jax.ShapeDtypeStruct((m, n), BF16),
        scratch_shapes=[pltpu.VMEM((tm, tn), F32)],
        compiler_params=_compiler_params(("parallel", "arbitrary")),
        name="qkv_proj",
    )(h_var, w, gains, *rope_tables)


def _split3_bf16(v):
    hi = v.astype(BF16)
    rem = v - hi.astype(F32)
    mid = rem.astype(BF16)
    lo = (rem - mid.astype(F32)).astype(BF16)
    return hi, mid, lo


def _proj_kernel(*refs, n_norm_tiles, forget, tiles_per_batch, out_scale):
    x_ref, gn_ref, sh_ref, sc_ref, w_ref, gain_ref = refs[:6]
    pos = 6
    if forget:
        wf_ref, bf_ref, tri_ref, place_ref, ones_ref = refs[pos:pos + 5]
        pos += 5
    o_ref = refs[pos]
    pos += 1
    if forget:
        fk_ref, fq_ref = refs[pos:pos + 2]
        pos += 2
    h_scr, acc_scr = refs[pos:pos + 2]
    pos += 2
    if forget:
        carry_scr = refs[pos]

    i = pl.program_id(0)
    j = pl.program_id(1)
    tm = h_scr.shape[0]
    gain = gain_ref[...] * out_scale

    @pl.when(j == 0)
    def _():
        pieces = _lhs_pieces(True, x_ref, gn_ref, sh_ref, sc_ref, h_scr)
        _dot_head_norm(pieces, w_ref, o_ref, acc_scr, gain)
        if forget:
            z = jnp.dot(h_scr[...], wf_ref[...], preferred_element_type=F32) + bf_ref[...]
            log_f = jnp.minimum(z, 0.0) - jnp.log1p(jnp.exp(-jnp.abs(z)))
            tri = tri_ref[...]
            cum = None
            for part in _split3_bf16(log_f):
                term = jnp.dot(tri, part, preferred_element_type=F32)
                cum = term if cum is None else cum + term

            @pl.when(i % tiles_per_batch == 0)
            def _():
                carry_scr[...] = jnp.zeros_like(carry_scr)

            cum = cum + carry_scr[...]
            carry_scr[...] = cum[tm - 1:tm, :]
            parts = _split3_bf16(cum * (-LOG2E))
            for side, out_ref in enumerate((fk_ref, fq_ref)):
                cols = ones_ref[side]
                for p, part in enumerate(parts):
                    cols = cols + jnp.dot(part, place_ref[side, p], preferred_element_type=F32)
                out_ref[...] = cols.astype(BF16)

    @pl.when((j > 0) & (j < n_norm_tiles))
    def _():
        pieces = _lhs_pieces(False, x_ref, gn_ref, sh_ref, sc_ref, h_scr)
        _dot_head_norm(pieces, w_ref, o_ref, acc_scr, gain)

    @pl.when(j >= n_norm_tiles)
    def _():
        o_ref[...] = jnp.dot(h_scr[...], w_ref[...],
                             preferred_element_type=F32).astype(o_ref.dtype)


def _fox_bias_lanes(n_heads):
    assert 6 * n_heads <= LANES
    place = np.zeros((2, 3, LANES, LANES), np.float32)
    ones = np.zeros((2, 1, LANES), np.float32)
    for h in range(n_heads):
        for p in range(3):
            place[0, p, h, 3 * h + p] = 1.0
            place[1, p, h, 3 * (n_heads + h) + p] = -1.0
    ones[0, 0, 3 * n_heads:6 * n_heads] = 1.0
    ones[1, 0, :3 * n_heads] = 1.0
    return jnp.asarray(place, BF16), jnp.asarray(ones, F32)


def _proj(x2, seq, gnorm, shift, scale, w, layer, gain, n_norm_tiles, forget=None,
          out_scale=1.0):
    m, d = x2.shape
    n = w.shape[-1]
    tm = PROJ_ROWS
    tn = PROJ_COLS
    assert seq % tm == 0 and n % tn == 0
    tpb = seq // tm
    with_forget = forget is not None

    in_specs = [
        pl.BlockSpec((tm, d), lambda i, j: (i, 0)),
        pl.BlockSpec((1, d), lambda i, j: (0, 0)),
        pl.BlockSpec((None, 1, d), lambda i, j: (i // tpb, 0, 0)),
        pl.BlockSpec((None, 1, d), lambda i, j: (i // tpb, 0, 0)),
        pl.BlockSpec((None, d, tn), lambda i, j: (layer, 0, j)),
        pl.BlockSpec((1, HEAD_DIM), lambda i, j: (0, 0)),
    ]
    args = [x2, gnorm.reshape(1, d), shift, scale, w, gain]
    out_specs = [pl.BlockSpec((tm, tn), lambda i, j: (i, j))]
    out_shape = [jax.ShapeDtypeStruct((m, n), BF16)]
    scratch = [pltpu.VMEM((tm, d), BF16), pltpu.VMEM((tm, tn), F32)]
    if with_forget:
        wf_pad, bf_pad, n_heads = forget
        tri = (lax.broadcasted_iota(jnp.int32, (tm, tm), 0)
               >= lax.broadcasted_iota(jnp.int32, (tm, tm), 1)).astype(BF16)
        place, ones = _fox_bias_lanes(n_heads)
        in_specs += [
            pl.BlockSpec((d, LANES), lambda i, j: (0, 0)),
            pl.BlockSpec((1, LANES), lambda i, j: (0, 0)),
            pl.BlockSpec((tm, tm), lambda i, j: (0, 0)),
            pl.BlockSpec((2, 3, LANES, LANES), lambda i, j: (0, 0, 0, 0)),
            pl.BlockSpec((2, 1, LANES), lambda i, j: (0, 0, 0)),
        ]
        args += [wf_pad, bf_pad, tri, place, ones]
        out_specs += [
            pl.BlockSpec((tm, LANES), lambda i, j: (i, 0)),
            pl.BlockSpec((tm, LANES), lambda i, j: (i, 0)),
        ]
        out_shape += [
            jax.ShapeDtypeStruct((m, LANES), BF16),
            jax.ShapeDtypeStruct((m, LANES), BF16),
        ]
        scratch.append(pltpu.VMEM((1, LANES), F32))

    kern = functools.partial(_proj_kernel, n_norm_tiles=n_norm_tiles, forget=with_forget,
                             tiles_per_batch=tpb, out_scale=out_scale)
    outs = pl.pallas_call(
        kern,
        grid=(m // tm, n // tn),
        in_specs=in_specs,
        out_specs=out_specs,
        out_shape=out_shape,
        scratch_shapes=scratch,
        compiler_params=_compiler_params(("arbitrary", "arbitrary")),
        name="proj",
    )(*args)
    return outs if with_forget else outs[0]


def _swiglu_kernel(x_ref, gn_ref, sh_ref, sc_ref, wg_ref, wu_ref, o_ref, h_scr):
    def gated(h):
        g = jnp.dot(h, wg_ref[...], preferred_element_type=F32)
        u = jnp.dot(h, wu_ref[...], preferred_element_type=F32)
        return (_silu(g) * u).astype(o_ref.dtype)

    @pl.when(pl.program_id(1) == 0)
    def _():
        for rows, h in _lhs_pieces(True, x_ref, gn_ref, sh_ref, sc_ref, h_scr):
            o_ref[rows, :] = gated(h)

    @pl.when(pl.program_id(1) > 0)
    def _():
        o_ref[...] = gated(h_scr[...])


def _swiglu_in(x2, seq, gnorm, shift, scale, w, layer):
    m, d = x2.shape
    f = w.shape[-1] // 2
    tm = PROJ_ROWS
    tf = _pick_tile(f, (512, 256, 128))
    assert seq % tm == 0
    tpb = seq // tm
    nf = f // tf
    return pl.pallas_call(
        _swiglu_kernel,
        grid=(m // tm, nf),
        in_specs=[
            pl.BlockSpec((tm, d), lambda i, j: (i, 0)),
            pl.BlockSpec((1, d), lambda i, j: (0, 0)),
            pl.BlockSpec((None, 1, d), lambda i, j: (i // tpb, 0, 0)),
            pl.BlockSpec((None, 1, d), lambda i, j: (i // tpb, 0, 0)),
            pl.BlockSpec((None, d, tf), lambda i, j: (layer, 0, j)),
            pl.BlockSpec((None, d, tf), lambda i, j: (layer, 0, nf + j)),
        ],
        out_specs=pl.BlockSpec((tm, tf), lambda i, j: (i, j)),
        out_shape=jax.ShapeDtypeStruct((m, f), BF16),
        scratch_shapes=[pltpu.VMEM((tm, d), BF16)],
        compiler_params=_compiler_params(("parallel", "arbitrary")),
        name="swiglu_in",
    )(x2, gnorm.reshape(1, d), shift, scale, w, w)


def _mmres_kernel(a_ref, w_ref, x_ref, gt_ref, o_ref):
    y = jnp.dot(a_ref[...], w_ref[...], preferred_element_type=F32)
    o_ref[...] = x_ref[...] + gt_ref[...] * y


def _mm_residual(a, w, layer, x2, seq, gate):
    m, k = a.shape
    n = w.shape[-1]
    tm = RESIDUAL_ROWS
    assert seq % tm == 0
    tpb = seq // tm
    return pl.pallas_call(
        _mmres_kernel,
        grid=(m // tm,),
        in_specs=[
            pl.BlockSpec((tm, k), lambda i: (i, 0)),
            pl.BlockSpec((None, k, n), lambda i: (layer, 0, 0), pipeline_mode=pl.Buffered(1)),
            pl.BlockSpec((tm, n), lambda i: (i, 0)),
            pl.BlockSpec((None, 1, n), lambda i: (i // tpb, 0, 0)),
        ],
        out_specs=pl.BlockSpec((tm, n), lambda i: (i, 0)),
        out_shape=jax.ShapeDtypeStruct((m, n), F32),
        compiler_params=_compiler_params(("parallel",)),
        name="mm_residual",
    )(a, w, x2, gate)


def _dilated_kernel(q_ref, kc_ref, kp_ref, vc_ref, vp_ref, o_ref, lse_ref, o_scr, *,
                    n_back, n_heads, dilation):
    blk = pl.program_id(1)
    res = pl.program_id(2)
    width = n_heads * HEAD_DIM
    qi = lax.broadcasted_iota(jnp.int32, (ATT_BLOCK, 2 * ATT_BLOCK), 0)
    kj = lax.broadcasted_iota(jnp.int32, (ATT_BLOCK, 2 * ATT_BLOCK), 1)
    dist = qi + ATT_BLOCK - kj
    mask = (dist >= 0) & (dist <= n_back) & ((kj >= ATT_BLOCK) | (blk > 0))
    lane = lax.broadcasted_iota(jnp.int32, (ATT_BLOCK, LANES), 1)
    contract_last = (((1,), (1,)), ((), ()))

    q_all = q_ref[...].reshape(ATT_BLOCK, width)
    k_all = jnp.concatenate([kp_ref[...].reshape(ATT_BLOCK, width),
                             kc_ref[...].reshape(ATT_BLOCK, width)], axis=0)
    v_all = jnp.concatenate([vp_ref[...].reshape(ATT_BLOCK, width),
                             vc_ref[...].reshape(ATT_BLOCK, width)], axis=0)
    if dilation == 1:
        rows = slice(None)
    else:
        rows = pl.ds(res, ATT_BLOCK, stride=dilation)

    head_cols = [slice(h * HEAD_DIM, (h + 1) * HEAD_DIM) for h in range(n_heads)]
    scores = [lax.dot_general(q_all[:, cols], k_all[:, cols], contract_last,
                              preferred_element_type=F32) for cols in head_cols]
    probs = []
    lse_tile = jnp.zeros((ATT_BLOCK, LANES), F32)
    for h, s2 in enumerate(scores):
        s2 = jnp.where(mask, s2, NEG_INF)
        mx2 = jnp.max(s2, axis=-1, keepdims=True)
        p = jnp.exp2(s2 - mx2)
        den = jnp.sum(p, axis=-1, keepdims=True)
        probs.append((p.astype(BF16), den))
        lse_tile = jnp.where(lane == h, mx2 * LN2 + jnp.log(den), lse_tile)
    for h, (p, den) in enumerate(probs):
        o = jnp.dot(p, v_all[:, head_cols[h]], preferred_element_type=F32)
        o_scr[h, rows, :] = o / den
    lse_ref[rows, :] = lse_tile

    @pl.when(res == dilation - 1)
    def _():
        for h in range(n_heads):
            o_ref[:, h * HEAD_DIM:(h + 1) * HEAD_DIM] = o_scr[h].astype(o_ref.dtype)


def _dilated_group(qkv, bsz, seq, group, n_groups, width, window, dilation):
    n_back = window // dilation
    sub_len = seq // dilation
    assert sub_len % ATT_BLOCK == 0
    nb = sub_len // ATT_BLOCK
    n_cols = qkv.shape[1]
    tpb = seq // DEINT_ROWS
    per_residue = DEINT_ROWS // dilation

    if per_residue >= ATT_BLOCK:
        bpt = per_residue // ATT_BLOCK
        view = qkv.reshape(bsz, tpb, dilation, bpt, ATT_BLOCK, n_cols)
        block = (None, None, None, None, ATT_BLOCK, width)

        def rows_index(b, n, r):
            return (b, n // bpt, r, n % bpt, 0)
    else:
        pieces = ATT_BLOCK // per_residue
        view = qkv.reshape(bsz, tpb, dilation, per_residue, n_cols)
        block = (None, pieces, None, per_residue, width)

        def rows_index(b, n, r):
            return (b, n, r, 0)

    def spec(part, prev):
        def index(b, n, r):
            nn = jnp.maximum(n - 1, 0) if prev else n
            return rows_index(b, nn, r) + (part * n_groups + group,)
        return pl.BlockSpec(block, index)

    span = ATT_BLOCK * dilation
    o, lse = pl.pallas_call(
        functools.partial(_dilated_kernel, n_back=n_back, n_heads=width // HEAD_DIM,
                          dilation=dilation),
        grid=(bsz, nb, dilation),
        in_specs=[spec(0, False), spec(1, False), spec(1, True), spec(2, False), spec(2, True)],
        out_specs=[
            pl.BlockSpec((None, span, width), lambda b, n, r: (b, n, 0)),
            pl.BlockSpec((None, span, LANES), lambda b, n, r: (b, n, 0)),
        ],
        out_shape=[
            jax.ShapeDtypeStruct((bsz, seq, width), BF16),
            jax.ShapeDtypeStruct((bsz, seq, LANES), F32),
        ],
        scratch_shapes=[pltpu.VMEM((width // HEAD_DIM, span, HEAD_DIM), F32)],
        compiler_params=_compiler_params(("parallel", "arbitrary", "arbitrary")),
        name="dilated_attention",
    )(view, view, view, view, view)
    return o.reshape(bsz * seq, width), lse.reshape(bsz * seq, LANES)


def _mix_out_kernel(*refs, n_groups, n_heads, row_chunk):
    o_refs = refs[:n_groups]
    lse_refs = refs[n_groups:2 * n_groups]
    w_ref, x_ref, gt_ref, out_ref, lhs_scr = refs[2 * n_groups:]
    tm = x_ref.shape[0]
    w = w_ref[...]
    gate = gt_ref[...]

    for r in range(0, tm, row_chunk):
        rows = slice(r, r + row_chunk)
        lses = [ref[rows, :] for ref in lse_refs]
        mx = functools.reduce(jnp.maximum, lses)
        es = [jnp.exp(l - mx) for l in lses]
        inv = 1.0 / functools.reduce(lambda a, b: a + b, es)
        alphas = [e * inv for e in es]
        for h in range(n_heads):
            cols = slice(h * HEAD_DIM, (h + 1) * HEAD_DIM)
            mixed = None
            for g in range(n_groups):
                term = alphas[g][:, h:h + 1] * o_refs[g][rows, cols].astype(F32)
                mixed = term if mixed is None else mixed + term
            lhs_scr[rows, cols] = mixed.astype(BF16)
        y = jnp.dot(lhs_scr[rows, :], w, preferred_element_type=F32)
        out_ref[rows, :] = x_ref[rows, :] + gate * y


def _mix_out(o_list, lse_list, w, layer, x2, seq, gate):
    m, width = o_list[0].shape
    n = w.shape[-1]
    n_groups = len(o_list)
    tm = _pick_tile(seq, (512, 256, 128))
    tpb = seq // tm
    return pl.pallas_call(
        functools.partial(_mix_out_kernel, n_groups=n_groups, n_heads=width // HEAD_DIM,
                          row_chunk=128),
        grid=(m // tm,),
        in_specs=(
            [pl.BlockSpec((tm, width), lambda i: (i, 0))] * n_groups
            + [pl.BlockSpec((tm, LANES), lambda i: (i, 0))] * n_groups
            + [
                pl.BlockSpec((None, width, n), lambda i: (layer, 0, 0)),
                pl.BlockSpec((tm, n), lambda i: (i, 0)),
                pl.BlockSpec((None, 1, n), lambda i: (i // tpb, 0, 0)),
            ]
        ),
        out_specs=pl.BlockSpec((tm, n), lambda i: (i, 0)),
        out_shape=jax.ShapeDtypeStruct((m, n), F32),
        scratch_shapes=[pltpu.VMEM((tm, width), BF16)],
        compiler_params=_compiler_params(("parallel",)),
        name="mix_out",
    )(*o_list, *lse_list, w, x2, gate)


def _fox_kernel(q_ref, k_ref, v_ref, fq_ref, fk_ref, o_ref, *, blk, piece, n_heads,
                heads_per_step):
    head0 = pl.program_id(1) * heads_per_step
    qt = pl.program_id(2)
    lane = lax.broadcasted_iota(jnp.int32, (1, LANES), 1)
    fq_all = fq_ref[...]
    contract_last = (((1,), (1,)), ((), ()))

    qs = []
    for hh in range(heads_per_step):
        off = (lane % (3 * n_heads)) - 3 * (head0 + hh)
        keep = jnp.where(off >= 0, jnp.where(off < 3, 1.0, 0.0), 0.0)
        keep = jnp.where(lane < 6 * n_heads, keep, 0.0).astype(BF16)
        qs.append(jnp.concatenate([q_ref[:, hh * HEAD_DIM:(hh + 1) * HEAD_DIM],
                                   fq_all * keep], axis=1))

    n_pieces = blk // piece
    chains = [(hh, pc) for hh in range(heads_per_step) for pc in range(n_pieces)]

    def step(kb, carry, diagonal):
        scores = []
        for hh, pc in chains:
            n_keys = (pc + 1) * piece if diagonal else blk
            rows = pl.ds(pl.multiple_of(kb * blk, blk), n_keys)
            cols = slice(hh * HEAD_DIM, (hh + 1) * HEAD_DIM)
            k = jnp.concatenate([k_ref[rows, cols], fk_ref[rows, :]], axis=1)
            q = qs[hh][pc * piece:(pc + 1) * piece, :]
            s2 = lax.dot_general(q, k, contract_last, preferred_element_type=F32)
            if diagonal:
                qi = lax.broadcasted_iota(jnp.int32, (piece, n_keys), 0) + pc * piece
                kj = lax.broadcasted_iota(jnp.int32, (piece, n_keys), 1)
                s2 = jnp.where(kj <= qi, s2, NEG_INF)
            scores.append((s2, rows, cols))
        probs = []
        for (s2, rows, cols), (m_run, l_run, acc) in zip(scores, carry):
            m_new = jnp.maximum(m_run, jnp.max(s2, axis=-1, keepdims=True))
            alpha = jnp.exp2(m_run - m_new)
            p = jnp.exp2(s2 - m_new)
            l_new = alpha * l_run + jnp.sum(p, axis=-1, keepdims=True)
            probs.append((p.astype(BF16), m_new, l_new, alpha * acc))
        out = []
        for (p, m_new, l_new, acc_scaled), (_, rows, cols) in zip(probs, scores):
            acc_new = acc_scaled + jnp.dot(p, v_ref[rows, cols], preferred_element_type=F32)
            out.append((m_new, l_new, acc_new))
        return tuple(out)

    init = tuple((jnp.full((piece, 1), NEG_INF, F32), jnp.zeros((piece, 1), F32),
                  jnp.zeros((piece, HEAD_DIM), F32)) for _ in chains)
    final = lax.fori_loop(0, qt, lambda kb, c: step(kb, c, False), init)
    final = step(qt, final, True)
    for (hh, pc), (_, l_fin, acc) in zip(chains, final):
        o_ref[pc * piece:(pc + 1) * piece, hh * HEAD_DIM:(hh + 1) * HEAD_DIM] = (
            acc / l_fin).astype(o_ref.dtype)


def _fox_attention(q, kv, f_query, f_keys, bsz, seq, n_heads):
    blk = _pick_tile(seq, (FOX_BLOCK, 256, 128))
    nkb = seq // blk
    width = n_heads * HEAD_DIM
    hps = FOX_HEADS_PER_STEP
    assert n_heads % hps == 0
    n_hg = n_heads // hps
    qv = q.reshape(bsz, seq, width)
    kvv = kv.reshape(bsz, seq, 2 * width)
    o = pl.pallas_call(
        functools.partial(_fox_kernel, blk=blk, piece=min(blk, FOX_PIECE_ROWS), n_heads=n_heads,
                          heads_per_step=hps),
        grid=(bsz, n_hg, nkb),
        in_specs=[
            pl.BlockSpec((None, blk, hps * HEAD_DIM), lambda b, h, t: (b, t, h)),
            pl.BlockSpec((None, seq, hps * HEAD_DIM), lambda b, h, t: (b, 0, h)),
            pl.BlockSpec((None, seq, hps * HEAD_DIM), lambda b, h, t: (b, 0, n_hg + h)),
            pl.BlockSpec((None, blk, LANES), lambda b, h, t: (b, t, 0)),
            pl.BlockSpec((None, seq, LANES), lambda b, h, t: (b, 0, 0)),
        ],
        out_specs=pl.BlockSpec((None, blk, hps * HEAD_DIM), lambda b, h, t: (b, t, h)),
        out_shape=jax.ShapeDtypeStruct((bsz, seq, width), BF16),
        compiler_params=_compiler_params(("parallel", "parallel", "arbitrary")),
        name="fox_attention",
    )(qv, kvv, kvv, f_query.reshape(bsz, seq, LANES), f_keys.reshape(bsz, seq, LANES))
    return o.reshape(bsz * seq, width)


def _rotary_lane_order():
    half = ROT_DIM // 2
    mid = HEAD_DIM // 2
    return np.concatenate([np.arange(0, half), np.arange(ROT_DIM, mid + half),
                           np.arange(half, ROT_DIM), np.arange(mid + half, HEAD_DIM)])


def _rope_tables(seq, dilations):
    half = ROT_DIM // 2
    mid = HEAD_DIM // 2
    inv = ROPE_THETA ** (-jnp.arange(0, ROT_DIM, 2, dtype=F32) / ROT_DIM)
    ang = jnp.arange(seq, dtype=F32)[:, None] * inv[None, :]
    cos, sin = jnp.cos(ang), jnp.sin(ang)
    ones = jnp.ones((seq, mid - half), F32)
    zeros = jnp.zeros((seq, mid - half), F32)
    cos_t = jnp.concatenate([cos, ones, cos, ones], axis=-1)
    sin_t = jnp.concatenate([-sin, zeros, sin, zeros], axis=-1)

    def deinterleave(t, d):
        t = t.reshape(seq // DEINT_ROWS, DEINT_ROWS // d, d, HEAD_DIM)
        return jnp.swapaxes(t, 1, 2).reshape(seq, HEAD_DIM)

    return (jnp.stack([deinterleave(cos_t, d) for d in dilations]),
            jnp.stack([deinterleave(sin_t, d) for d in dilations]))


def _split_mods(mods, parts):
    bsz, n = mods.shape
    d = n // parts
    return [mods[:, p * d:(p + 1) * d].reshape(bsz, 1, d) for p in range(parts)]


def kernel(x, c, w_ada, b_ada, g_norm_attn, g_norm_ffn, w_qkv_a, g_qk_a, w_o_a, w_ada_kv, b_ada_kv, g_norm_kv, w_kv, g_k_b, w_f, b_f, w_q_b, g_q_b, w_o_b, w_ffn_in, w_ffn_out):
    bsz, seq, d = x.shape
    depth = w_ada.shape[0]
    n_a = w_qkv_a.shape[0]
    n_groups = g_qk_a.shape[2]
    width_a = w_o_a.shape[1]
    n_heads_b = w_f.shape[1]
    assert n_groups == len(DIL_CONFIGS) and n_heads_b <= LANES and width_a == PROJ_COLS
    dilations = tuple(dl for _, dl in DIL_CONFIGS)

    x2 = x.reshape(bsz * seq, d)
    rope_tables = _rope_tables(seq, dilations)

    w_qkv = _qkv_weights(w_qkv_a, n_groups)
    g_qk = g_qk_a[..., _rotary_lane_order()]
    w_o_a = w_o_a.astype(BF16)
    w_kv = w_kv.astype(BF16)[None]
    w_q_b = w_q_b.astype(BF16)
    w_o_b = w_o_b.astype(BF16)
    w_ffn_in = w_ffn_in.astype(BF16)
    w_ffn_out = w_ffn_out.astype(BF16)
    wf_pad = jnp.pad(w_f, ((0, 0), (0, LANES - n_heads_b))).astype(BF16)
    bf_pad = jnp.pad(b_f, (0, LANES - n_heads_b)).reshape(1, LANES)

    kv = f_keys = f_query = None
    for layer in range(depth):
        sh_a, sc_a, gt_a, sh_f, sc_f, gt_f = _split_mods(_mods(c, w_ada, b_ada, layer), 6)
        if layer < n_a:
            gains = g_qk[layer].reshape(2 * n_groups, 1, HEAD_DIM)
            h_var = _deint_lhs(x2, seq, g_norm_attn[layer], sh_a, sc_a, dilations)
            qkv = _qkv_proj(h_var, seq, w_qkv, layer, gains, rope_tables)
            outs = [_dilated_group(qkv, bsz, seq, g, n_groups, width_a, window, dilation)
                    for g, (window, dilation) in enumerate(DIL_CONFIGS)]
            x2 = _mix_out([o for o, _ in outs], [l for _, l in outs], w_o_a, layer, x2, seq, gt_a)
        else:
            i = layer - n_a
            q = _proj(x2, seq, g_norm_attn[layer], sh_a, sc_a, w_q_b, i,
                      g_q_b[i].reshape(1, HEAD_DIM),
                      n_norm_tiles=w_q_b.shape[-1] // PROJ_COLS,
                      out_scale=HEAD_DIM ** -0.5 * LOG2E)
            o = _fox_attention(q, kv, f_query, f_keys, bsz, seq, n_heads_b)
            x2 = _mm_residual(o, w_o_b, i, x2, seq, gt_a)
        a = _swiglu_in(x2, seq, g_norm_ffn[layer], sh_f, sc_f, w_ffn_in, layer)
        x2 = _mm_residual(a, w_ffn_out, layer, x2, seq, gt_f)
        if layer == n_a - 1:
            sh_kv, sc_kv = _split_mods(_mods(c, w_ada_kv[None], b_ada_kv[None], 0), 2)
            kv, f_keys, f_query = _proj(
                x2, seq, g_norm_kv, sh_kv, sc_kv, w_kv, 0, g_k_b.reshape(1, HEAD_DIM),
                n_norm_tiles=w_kv.shape[-1] // 2 // PROJ_COLS, forget=(wf_pad, bf_pad, n_heads_b))
    return x2.reshape(bsz, seq, d)
```

```python
import functools
import math

import numpy as np
import jax
import jax.numpy as jnp
from jax import lax
from jax.experimental import pallas as pl
from jax.experimental.pallas import tpu as pltpu

HEAD_DIM = 128
DIL_CONFIGS = ((128, 1), (512, 4), (2048, 16))
ROT_DIM = HEAD_DIM // 4
ROPE_THETA = 500000.0
ATT_BLOCK = 128
EPS = 1e-6
NEG_INF = -1e30
LOG2E = math.log2(math.e)
LN2 = math.log(2.0)

LANES = 128
BF16_SUBLANES = 16
VMEM_LIMIT_BYTES = 56 * 1024 * 1024

PROJ_ROWS = 1024
PROJ_COLS = 1024
RESIDUAL_ROWS = 512
FILL_ROWS = 128
DEINT_ROWS = 512
PIECE_ROWS = 256
FOX_BLOCK = 512
FOX_PIECE_ROWS = 512
FOX_HEADS_PER_STEP = 4

F32 = jnp.float32
BF16 = jnp.bfloat16


def _compiler_params(semantics):
    return pltpu.CompilerParams(dimension_semantics=semantics, vmem_limit_bytes=VMEM_LIMIT_BYTES)


def _pick_tile(n, candidates):
    for t in candidates:
        if n % t == 0:
            return t
    raise ValueError(f"no tile in {candidates} divides {n}")


def _head_rmsnorm(a, gain):
    return a * lax.rsqrt(jnp.mean(a * a, axis=-1, keepdims=True) + EPS) * gain


def _silu(v):
    return v * jax.nn.sigmoid(v)


def _mods_kernel(c_ref, w_ref, b_ref, o_ref):
    c_act = _silu(c_ref[...])
    o_ref[...] = jnp.dot(c_act.astype(BF16), w_ref[...].astype(BF16),
                         preferred_element_type=F32) + b_ref[...]


def _mods(c, w, b, layer):
    bsz, d = c.shape
    n = w.shape[-1]
    tn = _pick_tile(n, (512, 256, 128))
    return pl.pallas_call(
        _mods_kernel,
        grid=(n // tn,),
        in_specs=[
            pl.BlockSpec((bsz, d), lambda j: (0, 0)),
            pl.BlockSpec((None, d, tn), lambda j: (layer, 0, j)),
            pl.BlockSpec((None, 1, tn), lambda j: (layer, 0, j)),
        ],
        out_specs=pl.BlockSpec((bsz, tn), lambda j: (0, j)),
        out_shape=jax.ShapeDtypeStruct((bsz, n), F32),
        compiler_params=_compiler_params(("parallel",)),
        name="mods",
    )(c, w, b.reshape(b.shape[0], 1, n))


def _normmod_rows(x_ref, rows, gain, mul, shift):
    xf = x_ref[rows, :]
    y = xf * lax.rsqrt(jnp.mean(xf * xf, axis=-1, keepdims=True) + EPS) * gain
    return y * mul + shift


def _row_pieces(tm):
    return [slice(r, r + PIECE_ROWS) for r in range(0, tm, PIECE_ROWS)]


def _lhs_pieces(first, x_ref, gn_ref, sh_ref, sc_ref, h_scr):
    gain = gn_ref[...]
    mul = 1.0 + sc_ref[...]
    shift = sh_ref[...]
    for rows in _row_pieces(h_scr.shape[0]):
        if first:
            h_scr[rows, :] = _normmod_rows(x_ref, rows, gain, mul, shift).astype(BF16)
        yield rows, h_scr[rows, :]


def _deint_lhs_kernel(x_ref, gn_ref, sh_ref, sc_ref, o_ref, slab_scr, *, dilations):
    tm, d_model = x_ref.shape
    n_slabs = d_model // LANES
    gain = gn_ref[...]
    mul = 1.0 + sc_ref[...]
    shift = sh_ref[...]

    def natural(it, carry):
        rows = pl.ds(pl.multiple_of(it * FILL_ROWS, FILL_ROWS), FILL_ROWS)
        h = _normmod_rows(x_ref, rows, gain, mul, shift)
        for v, d in enumerate(dilations):
            if d == 1:
                o_ref[v, rows, :] = h.astype(BF16)
        for s in range(n_slabs):
            slab_scr[s, rows, :] = h[:, s * LANES:(s + 1) * LANES]
        return carry

    lax.fori_loop(0, tm // FILL_ROWS, natural, 0)

    for v, d in enumerate(dilations):
        if d == 1:
            continue
        per_residue = tm // d
        chunk = min(per_residue, FILL_ROWS)
        chunks_per_residue = per_residue // chunk

        def gather(it, carry, v=v, d=d, per_residue=per_residue, chunk=chunk,
                   chunks_per_residue=chunks_per_residue):
            r = it // chunks_per_residue
            l0 = (it % chunks_per_residue) * chunk
            src = pl.ds(r + l0 * d, chunk, stride=d)
            dst = pl.ds(pl.multiple_of(r * per_residue + l0, chunk), chunk)
            for s in range(n_slabs):
                o_ref[v, dst, s * LANES:(s + 1) * LANES] = slab_scr[s, src, :].astype(BF16)
            return carry

        lax.fori_loop(0, tm // chunk, gather, 0)


def _deint_lhs(x2, seq, gnorm, shift, scale, dilations):
    m, d = x2.shape
    tm = DEINT_ROWS
    assert seq % tm == 0 and all(tm % (dl * BF16_SUBLANES) == 0 for dl in dilations)
    tpb = seq // tm
    n_var = len(dilations)
    return pl.pallas_call(
        functools.partial(_deint_lhs_kernel, dilations=dilations),
        grid=(m // tm,),
        in_specs=[
            pl.BlockSpec((tm, d), lambda i: (i, 0)),
            pl.BlockSpec((1, d), lambda i: (0, 0)),
            pl.BlockSpec((None, 1, d), lambda i: (i // tpb, 0, 0)),
            pl.BlockSpec((None, 1, d), lambda i: (i // tpb, 0, 0)),
        ],
        out_specs=pl.BlockSpec((n_var, tm, d), lambda i: (0, i, 0)),
        out_shape=jax.ShapeDtypeStruct((n_var, m, d), BF16),
        scratch_shapes=[pltpu.VMEM((d // LANES, tm, LANES), F32)],
        compiler_params=_compiler_params(("parallel",)),
        name="deint_lhs",
    )(x2, gnorm.reshape(1, d), shift, scale)


def _dot_head_norm(pieces, w_ref, o_ref, acc_scr, gain, cos_ref=None, sin_ref=None):
    tn = acc_scr.shape[1]
    w = w_ref[...]
    done = []
    for rows, lhs in pieces:
        acc_scr[rows, :] = jnp.dot(lhs, w, preferred_element_type=F32)
        done.append(rows)
    for rows in done:
        if cos_ref is not None:
            cos = cos_ref[rows, :]
            sin = sin_ref[rows, :]
        for h in range(tn // HEAD_DIM):
            cols = slice(h * HEAD_DIM, (h + 1) * HEAD_DIM)
            y = _head_rmsnorm(acc_scr[rows, cols], gain)
            if cos_ref is not None:
                y = y * cos + pltpu.roll(y, HEAD_DIM // 2, 1) * sin
            o_ref[rows, cols] = y.astype(o_ref.dtype)


def _qkv_weight_kernel(w_ref, perm_ref, o_ref, *, n_perm_tiles):
    w = w_ref[...].astype(BF16)

    @pl.when(pl.program_id(1) < n_perm_tiles)
    def _():
        for h in range(w.shape[1] // HEAD_DIM):
            cols = slice(h * HEAD_DIM, (h + 1) * HEAD_DIM)
            o_ref[:, cols] = jnp.dot(w[:, cols], perm_ref[...],
                                     preferred_element_type=F32).astype(BF16)

    @pl.when(pl.program_id(1) >= n_perm_tiles)
    def _():
        o_ref[...] = w


def _qkv_weights(w, n_groups):
    n_layers, d, n = w.shape
    tn = PROJ_COLS
    assert n == 3 * n_groups * tn
    order = _rotary_lane_order()
    perm = np.zeros((HEAD_DIM, HEAD_DIM), np.float32)
    perm[order, np.arange(HEAD_DIM)] = 1.0
    return pl.pallas_call(
        functools.partial(_qkv_weight_kernel, n_perm_tiles=2 * n_groups),
        grid=(n_layers, n // tn),
        in_specs=[
            pl.BlockSpec((None, d, tn), lambda l, j: (l, 0, j)),
            pl.BlockSpec((HEAD_DIM, HEAD_DIM), lambda l, j: (0, 0)),
        ],
        out_specs=pl.BlockSpec((None, d, tn), lambda l, j: (l, 0, j)),
        out_shape=jax.ShapeDtypeStruct(w.shape, BF16),
        compiler_params=_compiler_params(("parallel", "parallel")),
        name="qkv_weights",
    )(w, jnp.asarray(perm, BF16))


def _qkv_kernel(h_ref, w_ref, gain_ref, cos_ref, sin_ref, o_ref, acc_scr, *, n_parts):
    part = pl.program_id(1) % n_parts

    @pl.when(part < 2)
    def _():
        gain = gain_ref[...] * jnp.where(part == 0, HEAD_DIM ** -0.5 * LOG2E, 1.0)
        pieces = [(rows, h_ref[rows, :]) for rows in _row_pieces(h_ref.shape[0])]
        _dot_head_norm(pieces, w_ref, o_ref, acc_scr, gain, cos_ref, sin_ref)

    @pl.when(part >= 2)
    def _():
        o_ref[...] = jnp.dot(h_ref[...], w_ref[...],
                             preferred_element_type=F32).astype(o_ref.dtype)


def _qkv_proj(h_var, seq, w, layer, gains, rope_tables):
    n_groups, m, d = h_var.shape
    n = w.shape[-1]
    n_parts = 3
    tm = PROJ_ROWS
    tn = PROJ_COLS
    assert seq % tm == 0 and n == n_parts * n_groups * tn
    tpb = seq // tm

    def col(j):
        return (j % n_parts) * n_groups + j // n_parts

    return pl.pallas_call(
        functools.partial(_qkv_kernel, n_parts=n_parts),
        grid=(m // tm, n_parts * n_groups),
        in_specs=[
            pl.BlockSpec((None, tm, d), lambda i, j: (j // n_parts, i, 0)),
            pl.BlockSpec((None, d, tn), lambda i, j: (layer, 0, col(j))),
            pl.BlockSpec((None, 1, HEAD_DIM),
                         lambda i, j: (jnp.minimum(col(j), 2 * n_groups - 1), 0, 0)),
            pl.BlockSpec((None, tm, HEAD_DIM), lambda i, j: (j // n_parts, i % tpb, 0)),
            pl.BlockSpec((None, tm, HEAD_DIM), lambda i, j: (j // n_parts, i % tpb, 0)),
        ],
        out_specs=pl.BlockSpec((tm, tn), lambda i, j: (i, col(j))),
        out_shape=jax.ShapeDtypeStruct((m, n), BF16),
        scratch_shapes=[pltpu.VMEM((tm, tn), F32)],
        compiler_params=_compiler_params(("parallel", "arbitrary")),
        name="qkv_proj",
    )(h_var, w, gains, *rope_tables)


def _split3_bf16(v):
    hi = v.astype(BF16)
    rem = v - hi.astype(F32)
    mid = rem.astype(BF16)
    lo = (rem - mid.astype(F32)).astype(BF16)
    return hi, mid, lo


def _proj_kernel(*refs, n_norm_tiles, forget, tiles_per_batch, out_scale):
    x_ref, gn_ref, sh_ref, sc_ref, w_ref, gain_ref = refs[:6]
    pos = 6
    if forget:
        wf_ref, bf_ref, tri_ref, place_ref, ones_ref = refs[pos:pos + 5]
        pos += 5
    o_ref = refs[pos]
    pos += 1
    if forget:
        fk_ref, fq_ref = refs[pos:pos + 2]
        pos += 2
    h_scr, acc_scr = refs[pos:pos + 2]
    pos += 2
    if forget:
        carry_scr = refs[pos]

    i = pl.program_id(0)
    j = pl.program_id(1)
    tm = h_scr.shape[0]
    gain = gain_ref[...] * out_scale

    @pl.when(j == 0)
    def _():
        pieces = _lhs_pieces(True, x_ref, gn_ref, sh_ref, sc_ref, h_scr)
        _dot_head_norm(pieces, w_ref, o_ref, acc_scr, gain)
        if forget:
            z = jnp.dot(h_scr[...], wf_ref[...], preferred_element_type=F32) + bf_ref[...]
            log_f = jnp.minimum(z, 0.0) - jnp.log1p(jnp.exp(-jnp.abs(z)))
            tri = tri_ref[...]
            cum = None
            for part in _split3_bf16(log_f):
                term = jnp.dot(tri, part, preferred_element_type=F32)
                cum = term if cum is None else cum + term

            @pl.when(i % tiles_per_batch == 0)
            def _():
                carry_scr[...] = jnp.zeros_like(carry_scr)

            cum = cum + carry_scr[...]
            carry_scr[...] = cum[tm - 1:tm, :]
            parts = _split3_bf16(cum * (-LOG2E))
            for side, out_ref in enumerate((fk_ref, fq_ref)):
                cols = ones_ref[side]
                for p, part in enumerate(parts):
                    cols = cols + jnp.dot(part, place_ref[side, p], preferred_element_type=F32)
                out_ref[...] = cols.astype(BF16)

    @pl.when((j > 0) & (j < n_norm_tiles))
    def _():
        pieces = _lhs_pieces(False, x_ref, gn_ref, sh_ref, sc_ref, h_scr)
        _dot_head_norm(pieces, w_ref, o_ref, acc_scr, gain)

    @pl.when(j >= n_norm_tiles)
    def _():
        o_ref[...] = jnp.dot(h_scr[...], w_ref[...],
                             preferred_element_type=F32).astype(o_ref.dtype)


def _fox_bias_lanes(n_heads):
    assert 6 * n_heads <= LANES
    place = np.zeros((2, 3, LANES, LANES), np.float32)
    ones = np.zeros((2, 1, LANES), np.float32)
    for h in range(n_heads):
        for p in range(3):
            place[0, p, h, 3 * h + p] = 1.0
            place[1, p, h, 3 * (n_heads + h) + p] = -1.0
    ones[0, 0, 3 * n_heads:6 * n_heads] = 1.0
    ones[1, 0, :3 * n_heads] = 1.0
    return jnp.asarray(place, BF16), jnp.asarray(ones, F32)


def _proj(x2, seq, gnorm, shift, scale, w, layer, gain, n_norm_tiles, forget=None,
          out_scale=1.0):
    m, d = x2.shape
    n = w.shape[-1]
    tm = PROJ_ROWS
    tn = PROJ_COLS
    assert seq % tm == 0 and n % tn == 0
    tpb = seq // tm
    with_forget = forget is not None

    in_specs = [
        pl.BlockSpec((tm, d), lambda i, j: (i, 0)),
        pl.BlockSpec((1, d), lambda i, j: (0, 0)),
        pl.BlockSpec((None, 1, d), lambda i, j: (i // tpb, 0, 0)),
        pl.BlockSpec((None, 1, d), lambda i, j: (i // tpb, 0, 0)),
        pl.BlockSpec((None, d, tn), lambda i, j: (layer, 0, j)),
        pl.BlockSpec((1, HEAD_DIM), lambda i, j: (0, 0)),
    ]
    args = [x2, gnorm.reshape(1, d), shift, scale, w, gain]
    out_specs = [pl.BlockSpec((tm, tn), lambda i, j: (i, j))]
    out_shape = [jax.ShapeDtypeStruct((m, n), BF16)]
    scratch = [pltpu.VMEM((tm, d), BF16), pltpu.VMEM((tm, tn), F32)]
    if with_forget:
        wf_pad, bf_pad, n_heads = forget
        tri = (lax.broadcasted_iota(jnp.int32, (tm, tm), 0)
               >= lax.broadcasted_iota(jnp.int32, (tm, tm), 1)).astype(BF16)
        place, ones = _fox_bias_lanes(n_heads)
        in_specs += [
            pl.BlockSpec((d, LANES), lambda i, j: (0, 0)),
            pl.BlockSpec((1, LANES), lambda i, j: (0, 0)),
            pl.BlockSpec((tm, tm), lambda i, j: (0, 0)),
            pl.BlockSpec((2, 3, LANES, LANES), lambda i, j: (0, 0, 0, 0)),
            pl.BlockSpec((2, 1, LANES), lambda i, j: (0, 0, 0)),
        ]
        args += [wf_pad, bf_pad, tri, place, ones]
        out_specs += [
            pl.BlockSpec((tm, LANES), lambda i, j: (i, 0)),
            pl.BlockSpec((tm, LANES), lambda i, j: (i, 0)),
        ]
        out_shape += [
            jax.ShapeDtypeStruct((m, LANES), BF16),
            jax.ShapeDtypeStruct((m, LANES), BF16),
        ]
        scratch.append(pltpu.VMEM((1, LANES), F32))

    kern = functools.partial(_proj_kernel, n_norm_tiles=n_norm_tiles, forget=with_forget,
                             tiles_per_batch=tpb, out_scale=out_scale)
    outs = pl.pallas_call(
        kern,
        grid=(m // tm, n // tn),
        in_specs=in_specs,
        out_specs=out_specs,
        out_shape=out_shape,
        scratch_shapes=scratch,
        compiler_params=_compiler_params(("arbitrary", "arbitrary")),
        name="proj",
    )(*args)
    return outs if with_forget else outs[0]


def _swiglu_kernel(x_ref, gn_ref, sh_ref, sc_ref, wg_ref, wu_ref, o_ref, h_scr):
    def gated(h):
        g = jnp.dot(h, wg_ref[...], preferred_element_type=F32)
        u = jnp.dot(h, wu_ref[...], preferred_element_type=F32)
        return (_silu(g) * u).astype(o_ref.dtype)

    @pl.when(pl.program_id(1) == 0)
    def _():
        for rows, h in _lhs_pieces(True, x_ref, gn_ref, sh_ref, sc_ref, h_scr):
            o_ref[rows, :] = gated(h)

    @pl.when(pl.program_id(1) > 0)
    def _():
        o_ref[...] = gated(h_scr[...])


def _swiglu_in(x2, seq, gnorm, shift, scale, w, layer):
    m, d = x2.shape
    f = w.shape[-1] // 2
    tm = PROJ_ROWS
    tf = _pick_tile(f, (512, 256, 128))
    assert seq % tm == 0
    tpb = seq // tm
    nf = f // tf
    return pl.pallas_call(
        _swiglu_kernel,
        grid=(m // tm, nf),
        in_specs=[
            pl.BlockSpec((tm, d), lambda i, j: (i, 0)),
            pl.BlockSpec((1, d), lambda i, j: (0, 0)),
            pl.BlockSpec((None, 1, d), lambda i, j: (i // tpb, 0, 0)),
            pl.BlockSpec((None, 1, d), lambda i, j: (i // tpb, 0, 0)),
            pl.BlockSpec((None, d, tf), lambda i, j: (layer, 0, j)),
            pl.BlockSpec((None, d, tf), lambda i, j: (layer, 0, nf + j)),
        ],
        out_specs=pl.BlockSpec((tm, tf), lambda i, j: (i, j)),
        out_shape=jax.ShapeDtypeStruct((m, f), BF16),
        scratch_shapes=[pltpu.VMEM((tm, d), BF16)],
        compiler_params=_compiler_params(("parallel", "arbitrary")),
        name="swiglu_in",
    )(x2, gnorm.reshape(1, d), shift, scale, w, w)


def _mmres_kernel(a_ref, w_ref, x_ref, gt_ref, o_ref):
    y = jnp.dot(a_ref[...], w_ref[...], preferred_element_type=F32)
    o_ref[...] = x_ref[...] + gt_ref[...] * y


def _mm_residual(a, w, layer, x2, seq, gate):
    m, k = a.shape
    n = w.shape[-1]
    tm = RESIDUAL_ROWS
    assert seq % tm == 0
    tpb = seq // tm
    return pl.pallas_call(
        _mmres_kernel,
        grid=(m // tm,),
        in_specs=[
            pl.BlockSpec((tm, k), lambda i: (i, 0)),
            pl.BlockSpec((None, k, n), lambda i: (layer, 0, 0), pipeline_mode=pl.Buffered(1)),
            pl.BlockSpec((tm, n), lambda i: (i, 0)),
            pl.BlockSpec((None, 1, n), lambda i: (i // tpb, 0, 0)),
        ],
        out_specs=pl.BlockSpec((tm, n), lambda i: (i, 0)),
        out_shape=jax.ShapeDtypeStruct((m, n), F32),
        compiler_params=_compiler_params(("parallel",)),
        name="mm_residual",
    )(a, w, x2, gate)


def _dilated_kernel(*refs, n_back, n_heads, dilation, has_prev):
    if has_prev:
        q_ref, kc_ref, vc_ref, o_ref, lse_ref, o_scr, kp_scr, vp_scr = refs
    else:
        q_ref, kc_ref, vc_ref, o_ref, lse_ref, o_scr = refs
    blk = pl.program_id(1)
    res = pl.program_id(2)
    if has_prev:
        kp_ref = kp_scr.at[res]
        vp_ref = vp_scr.at[res]

        @pl.when(blk == 0)
        def _():
            kp_ref[...] = jnp.zeros_like(kp_ref)
            vp_ref[...] = jnp.zeros_like(vp_ref)
    width = n_heads * HEAD_DIM
    n_keys = (2 if has_prev else 1) * ATT_BLOCK
    qi = lax.broadcasted_iota(jnp.int32, (ATT_BLOCK, n_keys), 0)
    kj = lax.broadcasted_iota(jnp.int32, (ATT_BLOCK, n_keys), 1)
    dist = qi + (n_keys - ATT_BLOCK) - kj
    mask = (dist >= 0) & (dist <= n_back)
    if has_prev:
        mask = mask & ((kj >= ATT_BLOCK) | (blk > 0))
    lane = lax.broadcasted_iota(jnp.int32, (ATT_BLOCK, LANES), 1)
    contract_last = (((1,), (1,)), ((), ()))

    q_all = q_ref[...].reshape(ATT_BLOCK, width)
    k_all = kc_ref[...].reshape(ATT_BLOCK, width)
    v_all = vc_ref[...].reshape(ATT_BLOCK, width)
    if has_prev:
        k_all = jnp.concatenate([kp_ref[...].reshape(ATT_BLOCK, width), k_all], axis=0)
        v_all = jnp.concatenate([vp_ref[...].reshape(ATT_BLOCK, width), v_all], axis=0)
    if dilation == 1:
        rows = slice(None)
    else:
        rows = pl.ds(res, ATT_BLOCK, stride=dilation)

    head_cols = [slice(h * HEAD_DIM, (h + 1) * HEAD_DIM) for h in range(n_heads)]
    scores = [lax.dot_general(q_all[:, cols], k_all[:, cols], contract_last,
                              preferred_element_type=F32) for cols in head_cols]
    probs = []
    lse_tile = jnp.zeros((ATT_BLOCK, LANES), F32)
    for h, s2 in enumerate(scores):
        s2 = jnp.where(mask, s2, NEG_INF)
        mx2 = jnp.max(s2, axis=-1, keepdims=True)
        p = jnp.exp2(s2 - mx2)
        den = jnp.sum(p, axis=-1, keepdims=True)
        probs.append((p.astype(BF16), den))
        lse_tile = jnp.where(lane == h, mx2 * LN2 + jnp.log(den), lse_tile)
    for h, (p, den) in enumerate(probs):
        o = jnp.dot(p, v_all[:, head_cols[h]], preferred_element_type=F32)
        o_scr[h, rows, :] = o / den
    lse_ref[rows, :] = lse_tile
    if has_prev:
        kp_ref[...] = kc_ref[...].reshape(ATT_BLOCK, width)
        vp_ref[...] = vc_ref[...].reshape(ATT_BLOCK, width)

    @pl.when(res == dilation - 1)
    def _():
        for h in range(n_heads):
            o_ref[:, h * HEAD_DIM:(h + 1) * HEAD_DIM] = o_scr[h].astype(o_ref.dtype)


def _dilated_group(qkv, bsz, seq, group, n_groups, width, window, dilation):
    n_back = window // dilation
    sub_len = seq // dilation
    assert sub_len % ATT_BLOCK == 0
    nb = sub_len // ATT_BLOCK
    n_cols = qkv.shape[1]
    tpb = seq // DEINT_ROWS
    per_residue = DEINT_ROWS // dilation

    if per_residue >= ATT_BLOCK:
        bpt = per_residue // ATT_BLOCK
        view = qkv.reshape(bsz, tpb, dilation, bpt, ATT_BLOCK, n_cols)
        block = (None, None, None, None, ATT_BLOCK, width)

        def rows_index(b, n, r):
            return (b, n // bpt, r, n % bpt, 0)
    else:
        pieces = ATT_BLOCK // per_residue
        view = qkv.reshape(bsz, tpb, dilation, per_residue, n_cols)
        block = (None, pieces, None, per_residue, width)

        def rows_index(b, n, r):
            return (b, n, r, 0)

    def spec(part):
        def index(b, n, r):
            return rows_index(b, n, r) + (part * n_groups + group,)
        return pl.BlockSpec(block, index)

    span = ATT_BLOCK * dilation
    has_prev = nb > 1
    in_specs = [spec(0), spec(1), spec(2)]
    scratch = [pltpu.VMEM((width // HEAD_DIM, span, HEAD_DIM), F32)]
    if has_prev:
        scratch += [pltpu.VMEM((dilation, ATT_BLOCK, width), BF16)] * 2
    o, lse = pl.pallas_call(
        functools.partial(_dilated_kernel, n_back=n_back, n_heads=width // HEAD_DIM,
                          dilation=dilation, has_prev=has_prev),
        grid=(bsz, nb, dilation),
        in_specs=in_specs,
        out_specs=[
            pl.BlockSpec((None, span, width), lambda b, n, r: (b, n, 0)),
            pl.BlockSpec((None, span, LANES), lambda b, n, r: (b, n, 0)),
        ],
        out_shape=[
            jax.ShapeDtypeStruct((bsz, seq, width), BF16),
            jax.ShapeDtypeStruct((bsz, seq, LANES), F32),
        ],
        scratch_shapes=scratch,
        compiler_params=_compiler_params(("parallel", "arbitrary", "arbitrary")),
        name="dilated_attention",
    )(*([view] * len(in_specs)))
    return o.reshape(bsz * seq, width), lse.reshape(bsz * seq, LANES)


def _mix_out_kernel(*refs, n_groups, n_heads, row_chunk):
    o_refs = refs[:n_groups]
    lse_refs = refs[n_groups:2 * n_groups]
    w_ref, x_ref, gt_ref, out_ref, lhs_scr = refs[2 * n_groups:]
    tm = x_ref.shape[0]
    w = w_ref[...]
    gate = gt_ref[...]

    for r in range(0, tm, row_chunk):
        rows = slice(r, r + row_chunk)
        lses = [ref[rows, :] for ref in lse_refs]
        mx = functools.reduce(jnp.maximum, lses)
        es = [jnp.exp(l - mx) for l in lses]
        inv = 1.0 / functools.reduce(lambda a, b: a + b, es)
        alphas = [e * inv for e in es]
        for h in range(n_heads):
            cols = slice(h * HEAD_DIM, (h + 1) * HEAD_DIM)
            mixed = None
            for g in range(n_groups):
                term = alphas[g][:, h:h + 1] * o_refs[g][rows, cols].astype(F32)
                mixed = term if mixed is None else mixed + term
            lhs_scr[rows, cols] = mixed.astype(BF16)
        y = jnp.dot(lhs_scr[rows, :], w, preferred_element_type=F32)
        out_ref[rows, :] = x_ref[rows, :] + gate * y


def _mix_out(o_list, lse_list, w, layer, x2, seq, gate):
    m, width = o_list[0].shape
    n = w.shape[-1]
    n_groups = len(o_list)
    tm = _pick_tile(seq, (512, 256, 128))
    tpb = seq // tm
    return pl.pallas_call(
        functools.partial(_mix_out_kernel, n_groups=n_groups, n_heads=width // HEAD_DIM,
                          row_chunk=128),
        grid=(m // tm,),
        in_specs=(
            [pl.BlockSpec((tm, width), lambda i: (i, 0))] * n_groups
            + [pl.BlockSpec((tm, LANES), lambda i: (i, 0))] * n_groups
            + [
                pl.BlockSpec((None, width, n), lambda i: (layer, 0, 0)),
                pl.BlockSpec((tm, n), lambda i: (i, 0)),
                pl.BlockSpec((None, 1, n), lambda i: (i // tpb, 0, 0)),
            ]
        ),
        out_specs=pl.BlockSpec((tm, n), lambda i: (i, 0)),
        out_shape=jax.ShapeDtypeStruct((m, n), F32),
        scratch_shapes=[pltpu.VMEM((tm, width), BF16)],
        compiler_params=_compiler_params(("parallel",)),
        name="mix_out",
    )(*o_list, *lse_list, w, x2, gate)


def _fox_kernel(q_ref, k_ref, v_ref, fq_ref, fk_ref, o_ref, *, blk, piece, n_heads,
                heads_per_step):
    head0 = pl.program_id(1) * heads_per_step
    qt = pl.program_id(2)
    lane = lax.broadcasted_iota(jnp.int32, (1, LANES), 1)
    fq_all = fq_ref[...]
    contract_last = (((1,), (1,)), ((), ()))

    qs = []
    for hh in range(heads_per_step):
        off = (lane % (3 * n_heads)) - 3 * (head0 + hh)
        keep = jnp.where(off >= 0, jnp.where(off < 3, 1.0, 0.0), 0.0)
        keep = jnp.where(lane < 6 * n_heads, keep, 0.0).astype(BF16)
        qs.append(jnp.concatenate([q_ref[:, hh * HEAD_DIM:(hh + 1) * HEAD_DIM],
                                   fq_all * keep], axis=1))

    n_pieces = blk // piece
    chains = [(hh, pc) for hh in range(heads_per_step) for pc in range(n_pieces)]

    def step(kb, carry, diagonal):
        scores = []
        for hh, pc in chains:
            n_keys = (pc + 1) * piece if diagonal else blk
            rows = pl.ds(pl.multiple_of(kb * blk, blk), n_keys)
            cols = slice(hh * HEAD_DIM, (hh + 1) * HEAD_DIM)
            k = jnp.concatenate([k_ref[rows, cols], fk_ref[rows, :]], axis=1)
            q = qs[hh][pc * piece:(pc + 1) * piece, :]
            s2 = lax.dot_general(q, k, contract_last, preferred_element_type=F32)
            if diagonal:
                qi = lax.broadcasted_iota(jnp.int32, (piece, n_keys), 0) + pc * piece
                kj = lax.broadcasted_iota(jnp.int32, (piece, n_keys), 1)
                s2 = jnp.where(kj <= qi, s2, NEG_INF)
            scores.append((s2, rows, cols))
        probs = []
        for (s2, rows, cols), (m_run, l_run, acc) in zip(scores, carry):
            m_new = jnp.maximum(m_run, jnp.max(s2, axis=-1, keepdims=True))
            alpha = jnp.exp2(m_run - m_new)
            p = jnp.exp2(s2 - m_new)
            l_new = alpha * l_run + jnp.sum(p, axis=-1, keepdims=True)
            probs.append((p.astype(BF16), m_new, l_new, alpha * acc))
        out = []
        for (p, m_new, l_new, acc_scaled), (_, rows, cols) in zip(probs, scores):
            acc_new = acc_scaled + jnp.dot(p, v_ref[rows, cols], preferred_element_type=F32)
            out.append((m_new, l_new, acc_new))
        return tuple(out)

    init = tuple((jnp.full((piece, 1), NEG_INF, F32), jnp.zeros((piece, 1), F32),
                  jnp.zeros((piece, HEAD_DIM), F32)) for _ in chains)
    final = lax.fori_loop(0, qt, lambda kb, c: step(kb, c, False), init)
    final = step(qt, final, True)
    for (hh, pc), (_, l_fin, acc) in zip(chains, final):
        o_ref[pc * piece:(pc + 1) * piece, hh * HEAD_DIM:(hh + 1) * HEAD_DIM] = (
            acc / l_fin).astype(o_ref.dtype)


def _fox_attention(q, kv, f_query, f_keys, bsz, seq, n_heads):
    blk = _pick_tile(seq, (FOX_BLOCK, 256, 128))
    nkb = seq // blk
    width = n_heads * HEAD_DIM
    hps = FOX_HEADS_PER_STEP
    assert n_heads % hps == 0
    n_hg = n_heads // hps
    qv = q.reshape(bsz, seq, width)
    kvv = kv.reshape(bsz, seq, 2 * width)
    o = pl.pallas_call(
        functools.partial(_fox_kernel, blk=blk, piece=min(blk, FOX_PIECE_ROWS), n_heads=n_heads,
                          heads_per_step=hps),
        grid=(bsz, n_hg, nkb),
        in_specs=[
            pl.BlockSpec((None, blk, hps * HEAD_DIM), lambda b, h, t: (b, t, h)),
            pl.BlockSpec((None, seq, hps * HEAD_DIM), lambda b, h, t: (b, 0, h)),
            pl.BlockSpec((None, seq, hps * HEAD_DIM), lambda b, h, t: (b, 0, n_hg + h)),
            pl.BlockSpec((None, blk, LANES), lambda b, h, t: (b, t, 0)),
            pl.BlockSpec((None, seq, LANES), lambda b, h, t: (b, 0, 0)),
        ],
        out_specs=pl.BlockSpec((None, blk, hps * HEAD_DIM), lambda b, h, t: (b, t, h)),
        out_shape=jax.ShapeDtypeStruct((bsz, seq, width), BF16),
        compiler_params=_compiler_params(("parallel", "parallel", "arbitrary")),
        name="fox_attention",
    )(qv, kvv, kvv, f_query.reshape(bsz, seq, LANES), f_keys.reshape(bsz, seq, LANES))
    return o.reshape(bsz * seq, width)


def _rotary_lane_order():
    half = ROT_DIM // 2
    mid = HEAD_DIM // 2
    return np.concatenate([np.arange(0, half), np.arange(ROT_DIM, mid + half),
                           np.arange(half, ROT_DIM), np.arange(mid + half, HEAD_DIM)])


def _rope_tables(seq, dilations):
    half = ROT_DIM // 2
    mid = HEAD_DIM // 2
    inv = ROPE_THETA ** (-jnp.arange(0, ROT_DIM, 2, dtype=F32) / ROT_DIM)
    ang = jnp.arange(seq, dtype=F32)[:, None] * inv[None, :]
    cos, sin = jnp.cos(ang), jnp.sin(ang)
    ones = jnp.ones((seq, mid - half), F32)
    zeros = jnp.zeros((seq, mid - half), F32)
    cos_t = jnp.concatenate([cos, ones, cos, ones], axis=-1)
    sin_t = jnp.concatenate([-sin, zeros, sin, zeros], axis=-1)

    def deinterleave(t, d):
        t = t.reshape(seq // DEINT_ROWS, DEINT_ROWS // d, d, HEAD_DIM)
        return jnp.swapaxes(t, 1, 2).reshape(seq, HEAD_DIM)

    return (jnp.stack([deinterleave(cos_t, d) for d in dilations]),
            jnp.stack([deinterleave(sin_t, d) for d in dilations]))


def _split_mods(mods, parts):
    bsz, n = mods.shape
    d = n // parts
    return [mods[:, p * d:(p + 1) * d].reshape(bsz, 1, d) for p in range(parts)]


def kernel(x, c, w_ada, b_ada, g_norm_attn, g_norm_ffn, w_qkv_a, g_qk_a, w_o_a, w_ada_kv, b_ada_kv, g_norm_kv, w_kv, g_k_b, w_f, b_f, w_q_b, g_q_b, w_o_b, w_ffn_in, w_ffn_out):
    bsz, seq, d = x.shape
    depth = w_ada.shape[0]
    n_a = w_qkv_a.shape[0]
    n_groups = g_qk_a.shape[2]
    width_a = w_o_a.shape[1]
    n_heads_b = w_f.shape[1]
    assert n_groups == len(DIL_CONFIGS) and n_heads_b <= LANES and width_a == PROJ_COLS
    dilations = tuple(dl for _, dl in DIL_CONFIGS)

    x2 = x.reshape(bsz * seq, d)
    rope_tables = _rope_tables(seq, dilations)

    w_qkv = _qkv_weights(w_qkv_a, n_groups)
    g_qk = g_qk_a[..., _rotary_lane_order()]
    w_o_a = w_o_a.astype(BF16)
    w_kv = w_kv.astype(BF16)[None]
    w_q_b = w_q_b.astype(BF16)
    w_o_b = w_o_b.astype(BF16)
    w_ffn_in = w_ffn_in.astype(BF16)
    w_ffn_out = w_ffn_out.astype(BF16)
    wf_pad = jnp.pad(w_f, ((0, 0), (0, LANES - n_heads_b))).astype(BF16)
    bf_pad = jnp.pad(b_f, (0, LANES - n_heads_b)).reshape(1, LANES)

    kv = f_keys = f_query = None
    for layer in range(depth):
        sh_a, sc_a, gt_a, sh_f, sc_f, gt_f = _split_mods(_mods(c, w_ada, b_ada, layer), 6)
        if layer < n_a:
            gains = g_qk[layer].reshape(2 * n_groups, 1, HEAD_DIM)
            h_var = _deint_lhs(x2, seq, g_norm_attn[layer], sh_a, sc_a, dilations)
            qkv = _qkv_proj(h_var, seq, w_qkv, layer, gains, rope_tables)
            outs = [_dilated_group(qkv, bsz, seq, g, n_groups, width_a, window, dilation)
                    for g, (window, dilation) in enumerate(DIL_CONFIGS)]
            x2 = _mix_out([o for o, _ in outs], [l for _, l in outs], w_o_a, layer, x2, seq, gt_a)
        else:
            i = layer - n_a
            q = _proj(x2, seq, g_norm_attn[layer], sh_a, sc_a, w_q_b, i,
                      g_q_b[i].reshape(1, HEAD_DIM),
                      n_norm_tiles=w_q_b.shape[-1] // PROJ_COLS,
                      out_scale=HEAD_DIM ** -0.5 * LOG2E)
            o = _fox_attention(q, kv, f_query, f_keys, bsz, seq, n_heads_b)
            x2 = _mm_residual(o, w_o_b, i, x2, seq, gt_a)
        a = _swiglu_in(x2, seq, g_norm_ffn[layer], sh_f, sc_f, w_ffn_in, layer)
        x2 = _mm_residual(a, w_ffn_out, layer, x2, seq, gt_f)
        if layer == n_a - 1:
            sh_kv, sc_kv = _split_mods(_mods(c, w_ada_kv[None], b_ada_kv[None], 0), 2)
            kv, f_keys, f_query = _proj(
                x2, seq, g_norm_kv, sh_kv, sc_kv, w_kv, 0, g_k_b.reshape(1, HEAD_DIM),
                n_norm_tiles=w_kv.shape[-1] // 2 // PROJ_COLS, forget=(wf_pad, bf_pad, n_heads_b))
    return x2.reshape(bsz, seq, d)
```

```python
import functools
import math

import numpy as np
import jax
import jax.numpy as jnp
from jax import lax
from jax.experimental import pallas as pl
from jax.experimental.pallas import tpu as pltpu

HEAD_DIM = 128
DIL_CONFIGS = ((128, 1), (512, 4), (2048, 16))
ROT_DIM = HEAD_DIM // 4
ROPE_THETA = 500000.0
ATT_BLOCK = 128
EPS = 1e-6
NEG_INF = -1e30
LOG2E = math.log2(math.e)
LN2 = math.log(2.0)

LANES = 128
BF16_SUBLANES = 16
VMEM_LIMIT_BYTES = 56 * 1024 * 1024

PROJ_ROWS = 1024
PROJ_COLS = 1024
RESIDUAL_ROWS = 512
FILL_ROWS = 128
CUMSUM_ROWS = 256
DEINT_ROWS = 512
PIECE_ROWS = 256
FOX_BLOCK = 512
FOX_DIAG_ROWS = 512
FOX_HEADS_PER_STEP = 4

F32 = jnp.float32
BF16 = jnp.bfloat16


def _compiler_params(semantics):
    return pltpu.CompilerParams(dimension_semantics=semantics, vmem_limit_bytes=VMEM_LIMIT_BYTES)


def _pick_tile(n, candidates):
    for t in candidates:
        if n % t == 0:
            return t
    raise ValueError(f"no tile in {candidates} divides {n}")


def _head_rmsnorm(a, gain):
    return a * lax.rsqrt(jnp.mean(a * a, axis=-1, keepdims=True) + EPS) * gain


def _silu(v):
    return v * jax.nn.sigmoid(v)


def _mods_kernel(c_ref, w_ref, b_ref, o_ref):
    c_act = _silu(c_ref[...])
    o_ref[...] = jnp.dot(c_act.astype(BF16), w_ref[...].astype(BF16),
                         preferred_element_type=F32) + b_ref[...]


def _mods(c, w, b, layer):
    bsz, d = c.shape
    n = w.shape[-1]
    tn = _pick_tile(n, (512, 256, 128))
    return pl.pallas_call(
        _mods_kernel,
        grid=(n // tn,),
        in_specs=[
            pl.BlockSpec((bsz, d), lambda j: (0, 0)),
            pl.BlockSpec((None, d, tn), lambda j: (layer, 0, j)),
            pl.BlockSpec((None, 1, tn), lambda j: (layer, 0, j)),
        ],
        out_specs=pl.BlockSpec((bsz, tn), lambda j: (0, j)),
        out_shape=jax.ShapeDtypeStruct((bsz, n), F32),
        compiler_params=_compiler_params(("parallel",)),
        name="mods",
    )(c, w, b.reshape(b.shape[0], 1, n))


def _normmod_rows(x_ref, rows, gain, mul, shift):
    xf = x_ref[rows, :]
    y = xf * lax.rsqrt(jnp.mean(xf * xf, axis=-1, keepdims=True) + EPS) * gain
    return y * mul + shift


def _row_pieces(tm):
    return [slice(r, r + PIECE_ROWS) for r in range(0, tm, PIECE_ROWS)]


def _lhs_pieces(first, x_ref, gn_ref, sh_ref, sc_ref, h_scr):
    gain = gn_ref[...]
    mul = 1.0 + sc_ref[...]
    shift = sh_ref[...]
    for rows in _row_pieces(h_scr.shape[0]):
        if first:
            h_scr[rows, :] = _normmod_rows(x_ref, rows, gain, mul, shift).astype(BF16)
        yield rows, h_scr[rows, :]


def _deint_lhs_kernel(x_ref, gn_ref, sh_ref, sc_ref, o_ref, slab_scr, *, dilations, bases):
    tm, d_model = x_ref.shape
    n_slabs = d_model // LANES
    gain = gn_ref[...]
    mul = 1.0 + sc_ref[...]
    shift = sh_ref[...]

    def natural(it, carry):
        rows = pl.ds(pl.multiple_of(it * FILL_ROWS, FILL_ROWS), FILL_ROWS)
        ssq = None
        for s in range(n_slabs):
            xs = x_ref[rows, s * LANES:(s + 1) * LANES]
            ssq = xs * xs if ssq is None else ssq + xs * xs
        inv = lax.rsqrt(jnp.sum(ssq, axis=-1, keepdims=True) * (1.0 / d_model) + EPS)
        inv = jnp.broadcast_to(inv, (FILL_ROWS, LANES))
        for s in range(n_slabs):
            cols = slice(s * LANES, (s + 1) * LANES)
            h = x_ref[rows, cols] * inv * gain[:, cols] * mul[:, cols] + shift[:, cols]
            slab_scr[0, s, rows, :] = h
            for v, d in enumerate(dilations):
                if d == 1:
                    o_ref[v, rows, cols] = h.astype(BF16)
        return carry

    lax.fori_loop(0, tm // FILL_ROWS, natural, 0)

    for v, d in enumerate(dilations):
        if d == 1:
            continue
        p = max(b for b in bases if d % b == 0 and b < d)
        f = d // p
        src_slabs = slab_scr.at[bases.index(p)]
        dst_slabs = slab_scr.at[bases.index(d)] if d in bases else None
        per_residue = tm // d
        units_per_trip = max(1, FILL_ROWS // per_residue)

        def gather(it, carry, v=v, d=d, p=p, f=f, per_residue=per_residue,
                   units_per_trip=units_per_trip, src_slabs=src_slabs, dst_slabs=dst_slabs):
            for u in range(units_per_trip):
                r = it * units_per_trip + u
                k = r // p
                rp = r % p
                src = pl.ds(rp * (tm // p) + k, per_residue, stride=f)
                dst = pl.ds(pl.multiple_of(r * per_residue, per_residue), per_residue)
                for s in range(n_slabs):
                    piece = src_slabs[s, src, :]
                    o_ref[v, dst, s * LANES:(s + 1) * LANES] = piece.astype(BF16)
                    if dst_slabs is not None:
                        dst_slabs[s, dst, :] = piece
            return carry

        lax.fori_loop(0, d // units_per_trip, gather, 0)


def _deint_lhs(x2, seq, gnorm, shift, scale, dilations):
    m, d = x2.shape
    tm = DEINT_ROWS
    assert seq % tm == 0 and all(tm % (dl * BF16_SUBLANES) == 0 for dl in dilations)
    tpb = seq // tm
    n_var = len(dilations)
    assert list(dilations) == sorted(dilations) and dilations[0] == 1
    bases = [1]
    for dl in dilations[1:]:
        base = max(b for b in dilations if dl % b == 0 and b < dl)
        if base not in bases:
            bases.append(base)
    bases = tuple(bases)
    return pl.pallas_call(
        functools.partial(_deint_lhs_kernel, dilations=dilations, bases=bases),
        grid=(m // tm,),
        in_specs=[
            pl.BlockSpec((tm, d), lambda i: (i, 0)),
            pl.BlockSpec((1, d), lambda i: (0, 0)),
            pl.BlockSpec((None, 1, d), lambda i: (i // tpb, 0, 0)),
            pl.BlockSpec((None, 1, d), lambda i: (i // tpb, 0, 0)),
        ],
        out_specs=pl.BlockSpec((n_var, tm, d), lambda i: (0, i, 0)),
        out_shape=jax.ShapeDtypeStruct((n_var, m, d), BF16),
        scratch_shapes=[pltpu.VMEM((len(bases), d // LANES, tm, LANES), F32)],
        compiler_params=_compiler_params(("parallel",)),
        name="deint_lhs",
    )(x2, gnorm.reshape(1, d), shift, scale)


def _dot_head_norm(pieces, w_ref, o_ref, acc_scr, gain, cos_ref=None, sin_ref=None):
    tn = acc_scr.shape[1]
    w = w_ref[...]
    done = []
    for rows, lhs in pieces:
        acc_scr[rows, :] = jnp.dot(lhs, w, preferred_element_type=F32)
        done.append(rows)
    for rows in done:
        if cos_ref is not None:
            cos = cos_ref[rows, :]
            sin = sin_ref[rows, :]
        for h in range(tn // HEAD_DIM):
            cols = slice(h * HEAD_DIM, (h + 1) * HEAD_DIM)
            y = _head_rmsnorm(acc_scr[rows, cols], gain)
            if cos_ref is not None:
                y = y * cos + pltpu.roll(y, HEAD_DIM // 2, 1) * sin
            o_ref[rows, cols] = y.astype(o_ref.dtype)


def _qkv_weight_kernel(w_ref, perm_ref, o_ref, *, n_perm_tiles):
    w = w_ref[...].astype(BF16)

    @pl.when(pl.program_id(1) < n_perm_tiles)
    def _():
        for h in range(w.shape[1] // HEAD_DIM):
            cols = slice(h * HEAD_DIM, (h + 1) * HEAD_DIM)
            o_ref[:, cols] = jnp.dot(w[:, cols], perm_ref[...],
                                     preferred_element_type=F32).astype(BF16)

    @pl.when(pl.program_id(1) >= n_perm_tiles)
    def _():
        o_ref[...] = w


def _qkv_weights(w, n_groups):
    n_layers, d, n = w.shape
    tn = PROJ_COLS
    assert n == 3 * n_groups * tn
    order = _rotary_lane_order()
    perm = np.zeros((HEAD_DIM, HEAD_DIM), np.float32)
    perm[order, np.arange(HEAD_DIM)] = 1.0
    return pl.pallas_call(
        functools.partial(_qkv_weight_kernel, n_perm_tiles=2 * n_groups),
        grid=(n_layers, n // tn),
        in_specs=[
            pl.BlockSpec((None, d, tn), lambda l, j: (l, 0, j)),
            pl.BlockSpec((HEAD_DIM, HEAD_DIM), lambda l, j: (0, 0)),
        ],
        out_specs=pl.BlockSpec((None, d, tn), lambda l, j: (l, 0, j)),
        out_shape=jax.ShapeDtypeStruct(w.shape, BF16),
        compiler_params=_compiler_params(("parallel", "parallel")),
        name="qkv_weights",
    )(w, jnp.asarray(perm, BF16))


def _qkv_kernel(h_ref, w_ref, gain_ref, cos_ref, sin_ref, o_ref, acc_scr, *, n_parts):
    part = pl.program_id(1) % n_parts

    @pl.when(part < 2)
    def _():
        gain = gain_ref[...] * jnp.where(part == 0, HEAD_DIM ** -0.5 * LOG2E, 1.0)
        pieces = [(rows, h_ref[rows, :]) for rows in _row_pieces(h_ref.shape[0])]
        _dot_head_norm(pieces, w_ref, o_ref, acc_scr, gain, cos_ref, sin_ref)

    @pl.when(part >= 2)
    def _():
        o_ref[...] = jnp.dot(h_ref[...], w_ref[...],
                             preferred_element_type=F32).astype(o_ref.dtype)


def _qkv_proj(h_var, seq, w, layer, gains, rope_tables):
    n_groups, m, d = h_var.shape
    n = w.shape[-1]
    n_parts = 3
    tm = PROJ_ROWS
    tn = PROJ_COLS
    assert seq % tm == 0 and n == n_parts * n_groups * tn
    tpb = seq // tm

    def col(j):
        return (j % n_parts) * n_groups + j // n_parts

    return pl.pallas_call(
        functools.partial(_qkv_kernel, n_parts=n_parts),
        grid=(m // tm, n_parts * n_groups),
        in_specs=[
            pl.BlockSpec((None, tm, d), lambda i, j: (j // n_parts, i, 0)),
            pl.BlockSpec((None, d, tn), lambda i, j: (layer, 0, col(j))),
            pl.BlockSpec((None, 1, HEAD_DIM),
                         lambda i, j: (jnp.minimum(col(j), 2 * n_groups - 1), 0, 0)),
            pl.BlockSpec((None, tm, HEAD_DIM), lambda i, j: (j // n_parts, i % tpb, 0)),
            pl.BlockSpec((None, tm, HEAD_DIM), lambda i, j: (j // n_parts, i % tpb, 0)),
        ],
        out_specs=pl.BlockSpec((tm, tn), lambda i, j: (i, col(j))),
        out_shape=jax.ShapeDtypeStruct((m, n), BF16),
        scratch_shapes=[pltpu.VMEM((tm, tn), F32)],
        compiler_params=_compiler_params(("parallel", "arbitrary")),
        name="qkv_proj",
    )(h_var, w, gains, *rope_tables)


def _split3_bf16(v):
    hi = v.astype(BF16)
    rem = v - hi.astype(F32)
    mid = rem.astype(BF16)
    lo = (rem - mid.astype(F32)).astype(BF16)
    return hi, mid, lo


def _proj_kernel(*refs, n_norm_tiles, forget, tiles_per_batch, out_scale):
    x_ref, gn_ref, sh_ref, sc_ref, w_ref, gain_ref = refs[:6]
    pos = 6
    if forget:
        wf_ref, bf_ref, tri_ref, place_ref, ones_ref = refs[pos:pos + 5]
        pos += 5
    o_ref = refs[pos]
    pos += 1
    if forget:
        fk_ref, fq_ref = refs[pos:pos + 2]
        pos += 2
    h_scr, acc_scr = refs[pos:pos + 2]
    pos += 2
    if forget:
        carry_scr = refs[pos]

    i = pl.program_id(0)
    j = pl.program_id(1)
    tm = h_scr.shape[0]
    gain = gain_ref[...] * out_scale

    @pl.when(j == 0)
    def _():
        pieces = _lhs_pieces(True, x_ref, gn_ref, sh_ref, sc_ref, h_scr)
        _dot_head_norm(pieces, w_ref, o_ref, acc_scr, gain)
        if forget:
            z = jnp.dot(h_scr[...], wf_ref[...], preferred_element_type=F32) + bf_ref[...]
            log_f = jnp.minimum(z, 0.0) - jnp.log1p(jnp.exp(-jnp.abs(z)))
            @pl.when(i % tiles_per_batch == 0)
            def _():
                carry_scr[...] = jnp.zeros_like(carry_scr)

            tri = tri_ref[...]
            parts = _split3_bf16(log_f)
            running = carry_scr[...]
            blocks = []
            for r0 in range(0, tm, CUMSUM_ROWS):
                blk_cum = running
                for part in parts:
                    blk_cum = blk_cum + jnp.dot(tri, part[r0:r0 + CUMSUM_ROWS, :],
                                                preferred_element_type=F32)
                running = blk_cum[CUMSUM_ROWS - 1:CUMSUM_ROWS, :]
                blocks.append(blk_cum)
            carry_scr[...] = running
            cum = jnp.concatenate(blocks, axis=0)
            bias = ones_ref[...]
            for p, part in enumerate(_split3_bf16(cum * (-LOG2E))):
                bias = bias + jnp.dot(part, place_ref[p], preferred_element_type=F32)
            fk_ref[...] = bias[:, :LANES].astype(BF16)
            fq_ref[...] = bias[:, LANES:].astype(BF16)

    @pl.when((j > 0) & (j < n_norm_tiles))
    def _():
        pieces = _lhs_pieces(False, x_ref, gn_ref, sh_ref, sc_ref, h_scr)
        _dot_head_norm(pieces, w_ref, o_ref, acc_scr, gain)

    @pl.when(j >= n_norm_tiles)
    def _():
        o_ref[...] = jnp.dot(h_scr[...], w_ref[...],
                             preferred_element_type=F32).astype(o_ref.dtype)


def _fox_bias_lanes(n_heads):
    assert 6 * n_heads <= LANES
    place = np.zeros((3, LANES, 2 * LANES), np.float32)
    ones = np.zeros((1, 2 * LANES), np.float32)
    for h in range(n_heads):
        for p in range(3):
            place[p, h, 3 * h + p] = 1.0
            place[p, h, LANES + 3 * (n_heads + h) + p] = -1.0
    ones[0, 3 * n_heads:6 * n_heads] = 1.0
    ones[0, LANES:LANES + 3 * n_heads] = 1.0
    return jnp.asarray(place, BF16), jnp.asarray(ones, F32)


def _proj(x2, seq, gnorm, shift, scale, w, layer, gain, n_norm_tiles, forget=None,
          out_scale=1.0):
    m, d = x2.shape
    n = w.shape[-1]
    tm = PROJ_ROWS
    tn = PROJ_COLS
    assert seq % tm == 0 and n % tn == 0
    tpb = seq // tm
    with_forget = forget is not None

    in_specs = [
        pl.BlockSpec((tm, d), lambda i, j: (i, 0)),
        pl.BlockSpec((1, d), lambda i, j: (0, 0)),
        pl.BlockSpec((None, 1, d), lambda i, j: (i // tpb, 0, 0)),
        pl.BlockSpec((None, 1, d), lambda i, j: (i // tpb, 0, 0)),
        pl.BlockSpec((None, d, tn), lambda i, j: (layer, 0, j)),
        pl.BlockSpec((1, HEAD_DIM), lambda i, j: (0, 0)),
    ]
    args = [x2, gnorm.reshape(1, d), shift, scale, w, gain]
    out_specs = [pl.BlockSpec((tm, tn), lambda i, j: (i, j))]
    out_shape = [jax.ShapeDtypeStruct((m, n), BF16)]
    scratch = [pltpu.VMEM((tm, d), BF16), pltpu.VMEM((tm, tn), F32)]
    if with_forget:
        wf_pad, bf_pad, n_heads = forget
        tri = (lax.broadcasted_iota(jnp.int32, (CUMSUM_ROWS, CUMSUM_ROWS), 0)
               >= lax.broadcasted_iota(jnp.int32, (CUMSUM_ROWS, CUMSUM_ROWS), 1)).astype(BF16)
        place, ones = _fox_bias_lanes(n_heads)
        in_specs += [
            pl.BlockSpec((d, LANES), lambda i, j: (0, 0)),
            pl.BlockSpec((1, LANES), lambda i, j: (0, 0)),
            pl.BlockSpec((CUMSUM_ROWS, CUMSUM_ROWS), lambda i, j: (0, 0)),
            pl.BlockSpec((3, LANES, 2 * LANES), lambda i, j: (0, 0, 0)),
            pl.BlockSpec((1, 2 * LANES), lambda i, j: (0, 0)),
        ]
        args += [wf_pad, bf_pad, tri, place, ones]
        out_specs += [
            pl.BlockSpec((tm, LANES), lambda i, j: (i, 0)),
            pl.BlockSpec((tm, LANES), lambda i, j: (i, 0)),
        ]
        out_shape += [
            jax.ShapeDtypeStruct((m, LANES), BF16),
            jax.ShapeDtypeStruct((m, LANES), BF16),
        ]
        scratch.append(pltpu.VMEM((1, LANES), F32))

    kern = functools.partial(_proj_kernel, n_norm_tiles=n_norm_tiles, forget=with_forget,
                             tiles_per_batch=tpb, out_scale=out_scale)
    outs = pl.pallas_call(
        kern,
        grid=(m // tm, n // tn),
        in_specs=in_specs,
        out_specs=out_specs,
        out_shape=out_shape,
        scratch_shapes=scratch,
        compiler_params=_compiler_params(("arbitrary", "arbitrary")),
        name="proj",
    )(*args)
    return outs if with_forget else outs[0]


def _swiglu_kernel(x_ref, gn_ref, sh_ref, sc_ref, wg_ref, wu_ref, o_ref, h_scr):
    def gated(h):
        g = jnp.dot(h, wg_ref[...], preferred_element_type=F32)
        u = jnp.dot(h, wu_ref[...], preferred_element_type=F32)
        return (_silu(g) * u).astype(o_ref.dtype)

    @pl.when(pl.program_id(1) == 0)
    def _():
        for rows, h in _lhs_pieces(True, x_ref, gn_ref, sh_ref, sc_ref, h_scr):
            o_ref[rows, :] = gated(h)

    @pl.when(pl.program_id(1) > 0)
    def _():
        o_ref[...] = gated(h_scr[...])


def _swiglu_in(x2, seq, gnorm, shift, scale, w, layer):
    m, d = x2.shape
    f = w.shape[-1] // 2
    tm = PROJ_ROWS
    tf = _pick_tile(f, (512, 256, 128))
    assert seq % tm == 0
    tpb = seq // tm
    nf = f // tf
    return pl.pallas_call(
        _swiglu_kernel,
        grid=(m // tm, nf),
        in_specs=[
            pl.BlockSpec((tm, d), lambda i, j: (i, 0)),
            pl.BlockSpec((1, d), lambda i, j: (0, 0)),
            pl.BlockSpec((None, 1, d), lambda i, j: (i // tpb, 0, 0)),
            pl.BlockSpec((None, 1, d), lambda i, j: (i // tpb, 0, 0)),
            pl.BlockSpec((None, d, tf), lambda i, j: (layer, 0, j)),
            pl.BlockSpec((None, d, tf), lambda i, j: (layer, 0, nf + j)),
        ],
        out_specs=pl.BlockSpec((tm, tf), lambda i, j: (i, j)),
        out_shape=jax.ShapeDtypeStruct((m, f), BF16),
        scratch_shapes=[pltpu.VMEM((tm, d), BF16)],
        compiler_params=_compiler_params(("parallel", "arbitrary")),
        name="swiglu_in",
    )(x2, gnorm.reshape(1, d), shift, scale, w, w)


def _mmres_kernel(a_ref, w_ref, x_ref, gt_ref, o_ref):
    y = jnp.dot(a_ref[...], w_ref[...], preferred_element_type=F32)
    o_ref[...] = x_ref[...] + gt_ref[...] * y


def _mm_residual(a, w, layer, x2, seq, gate):
    m, k = a.shape
    n = w.shape[-1]
    tm = RESIDUAL_ROWS
    assert seq % tm == 0
    tpb = seq // tm
    return pl.pallas_call(
        _mmres_kernel,
        grid=(m // tm,),
        in_specs=[
            pl.BlockSpec((tm, k), lambda i: (i, 0)),
            pl.BlockSpec((None, k, n), lambda i: (layer, 0, 0), pipeline_mode=pl.Buffered(1)),
            pl.BlockSpec((tm, n), lambda i: (i, 0)),
            pl.BlockSpec((None, 1, n), lambda i: (i // tpb, 0, 0)),
        ],
        out_specs=pl.BlockSpec((tm, n), lambda i: (i, 0)),
        out_shape=jax.ShapeDtypeStruct((m, n), F32),
        compiler_params=_compiler_params(("parallel",)),
        name="mm_residual",
    )(a, w, x2, gate)


def _dilated_kernel(*refs, n_back, n_heads, dilation, has_prev):
    if has_prev:
        q_ref, kc_ref, vc_ref, o_ref, lse_ref, o_scr, kp_scr, vp_scr = refs
    else:
        q_ref, kc_ref, vc_ref, o_ref, lse_ref, o_scr = refs
    blk = pl.program_id(1)
    res = pl.program_id(2)
    if has_prev:
        kp_ref = kp_scr.at[res]
        vp_ref = vp_scr.at[res]

        @pl.when(blk == 0)
        def _():
            kp_ref[...] = jnp.zeros_like(kp_ref)
            vp_ref[...] = jnp.zeros_like(vp_ref)
    width = n_heads * HEAD_DIM
    n_keys = (2 if has_prev else 1) * ATT_BLOCK
    qi = lax.broadcasted_iota(jnp.int32, (ATT_BLOCK, n_keys), 0)
    kj = lax.broadcasted_iota(jnp.int32, (ATT_BLOCK, n_keys), 1)
    dist = qi + (n_keys - ATT_BLOCK) - kj
    mask = (dist >= 0) & (dist <= n_back)
    if has_prev:
        mask = mask & ((kj >= ATT_BLOCK) | (blk > 0))
    lane = lax.broadcasted_iota(jnp.int32, (ATT_BLOCK, LANES), 1)
    contract_last = (((1,), (1,)), ((), ()))

    q_all = q_ref[...].reshape(ATT_BLOCK, width)
    k_all = kc_ref[...].reshape(ATT_BLOCK, width)
    v_all = vc_ref[...].reshape(ATT_BLOCK, width)
    if has_prev:
        k_all = jnp.concatenate([kp_ref[...].reshape(ATT_BLOCK, width), k_all], axis=0)
        v_all = jnp.concatenate([vp_ref[...].reshape(ATT_BLOCK, width), v_all], axis=0)
    if dilation == 1:
        rows = slice(None)
    else:
        rows = pl.ds(res, ATT_BLOCK, stride=dilation)

    head_cols = [slice(h * HEAD_DIM, (h + 1) * HEAD_DIM) for h in range(n_heads)]
    scores = [lax.dot_general(q_all[:, cols], k_all[:, cols], contract_last,
                              preferred_element_type=F32) for cols in head_cols]
    probs = []
    lse_tile = jnp.zeros((ATT_BLOCK, LANES), F32)
    for h, s2 in enumerate(scores):
        s2 = jnp.where(mask, s2, NEG_INF)
        mx2 = jnp.max(s2, axis=-1, keepdims=True)
        p = jnp.exp2(s2 - mx2)
        den = jnp.sum(p, axis=-1, keepdims=True)
        probs.append((p.astype(BF16), den))
        lse_tile = jnp.where(lane == h, mx2 * LN2 + jnp.log(den), lse_tile)
    for h, (p, den) in enumerate(probs):
        o = jnp.dot(p, v_all[:, head_cols[h]], preferred_element_type=F32)
        o_scr[h, rows, :] = o / den
    lse_ref[rows, :] = lse_tile
    if has_prev:
        kp_ref[...] = kc_ref[...].reshape(ATT_BLOCK, width)
        vp_ref[...] = vc_ref[...].reshape(ATT_BLOCK, width)

    @pl.when(res == dilation - 1)
    def _():
        for h in range(n_heads):
            o_ref[:, h * HEAD_DIM:(h + 1) * HEAD_DIM] = o_scr[h].astype(o_ref.dtype)


def _dilated_group(qkv, bsz, seq, group, n_groups, width, window, dilation):
    n_back = window // dilation
    sub_len = seq // dilation
    assert sub_len % ATT_BLOCK == 0
    nb = sub_len // ATT_BLOCK
    n_cols = qkv.shape[1]
    tpb = seq // DEINT_ROWS
    per_residue = DEINT_ROWS // dilation

    if per_residue >= ATT_BLOCK:
        bpt = per_residue // ATT_BLOCK
        view = qkv.reshape(bsz, tpb, dilation, bpt, ATT_BLOCK, n_cols)
        block = (None, None, None, None, ATT_BLOCK, width)

        def rows_index(b, n, r):
            return (b, n // bpt, r, n % bpt, 0)
    else:
        pieces = ATT_BLOCK // per_residue
        view = qkv.reshape(bsz, tpb, dilation, per_residue, n_cols)
        block = (None, pieces, None, per_residue, width)

        def rows_index(b, n, r):
            return (b, n, r, 0)

    def spec(part):
        def index(b, n, r):
            return rows_index(b, n, r) + (part * n_groups + group,)
        return pl.BlockSpec(block, index)

    span = ATT_BLOCK * dilation
    has_prev = nb > 1
    in_specs = [spec(0), spec(1), spec(2)]
    scratch = [pltpu.VMEM((width // HEAD_DIM, span, HEAD_DIM), F32)]
    if has_prev:
        scratch += [pltpu.VMEM((dilation, ATT_BLOCK, width), BF16)] * 2
    o, lse = pl.pallas_call(
        functools.partial(_dilated_kernel, n_back=n_back, n_heads=width // HEAD_DIM,
                          dilation=dilation, has_prev=has_prev),
        grid=(bsz, nb, dilation),
        in_specs=in_specs,
        out_specs=[
            pl.BlockSpec((None, span, width), lambda b, n, r: (b, n, 0)),
            pl.BlockSpec((None, span, LANES), lambda b, n, r: (b, n, 0)),
        ],
        out_shape=[
            jax.ShapeDtypeStruct((bsz, seq, width), BF16),
            jax.ShapeDtypeStruct((bsz, seq, LANES), F32),
        ],
        scratch_shapes=scratch,
        compiler_params=_compiler_params(("parallel", "arbitrary", "arbitrary")),
        name="dilated_attention",
    )(*([view] * len(in_specs)))
    return o.reshape(bsz * seq, width), lse.reshape(bsz * seq, LANES)


def _mix_out_kernel(*refs, n_groups, n_heads, row_chunk):
    o_refs = refs[:n_groups]
    lse_refs = refs[n_groups:2 * n_groups]
    w_ref, x_ref, gt_ref, out_ref, lhs_scr = refs[2 * n_groups:]
    tm = x_ref.shape[0]
    w = w_ref[...]
    gate = gt_ref[...]

    for r in range(0, tm, row_chunk):
        rows = slice(r, r + row_chunk)
        lses = [ref[rows, :] for ref in lse_refs]
        mx = functools.reduce(jnp.maximum, lses)
        es = [jnp.exp(l - mx) for l in lses]
        inv = 1.0 / functools.reduce(lambda a, b: a + b, es)
        alphas = [e * inv for e in es]
        for h in range(n_heads):
            cols = slice(h * HEAD_DIM, (h + 1) * HEAD_DIM)
            mixed = None
            for g in range(n_groups):
                term = alphas[g][:, h:h + 1] * o_refs[g][rows, cols].astype(F32)
                mixed = term if mixed is None else mixed + term
            lhs_scr[rows, cols] = mixed.astype(BF16)
        y = jnp.dot(lhs_scr[rows, :], w, preferred_element_type=F32)
        out_ref[rows, :] = x_ref[rows, :] + gate * y


def _mix_out(o_list, lse_list, w, layer, x2, seq, gate):
    m, width = o_list[0].shape
    n = w.shape[-1]
    n_groups = len(o_list)
    tm = _pick_tile(seq, (512, 256, 128))
    tpb = seq // tm
    return pl.pallas_call(
        functools.partial(_mix_out_kernel, n_groups=n_groups, n_heads=width // HEAD_DIM,
                          row_chunk=128),
        grid=(m // tm,),
        in_specs=(
            [pl.BlockSpec((tm, width), lambda i: (i, 0))] * n_groups
            + [pl.BlockSpec((tm, LANES), lambda i: (i, 0))] * n_groups
            + [
                pl.BlockSpec((None, width, n), lambda i: (layer, 0, 0)),
                pl.BlockSpec((tm, n), lambda i: (i, 0)),
                pl.BlockSpec((None, 1, n), lambda i: (i // tpb, 0, 0)),
            ]
        ),
        out_specs=pl.BlockSpec((tm, n), lambda i: (i, 0)),
        out_shape=jax.ShapeDtypeStruct((m, n), F32),
        scratch_shapes=[pltpu.VMEM((tm, width), BF16)],
        compiler_params=_compiler_params(("parallel",)),
        name="mix_out",
    )(*o_list, *lse_list, w, x2, gate)


def _fox_kernel(q_ref, k_ref, v_ref, fq_ref, fk_ref, o_ref, *, blk, diag_rows, n_heads,
                heads_per_step):
    head0 = pl.program_id(1) * heads_per_step
    qt = pl.program_id(2)
    lane = lax.broadcasted_iota(jnp.int32, (1, LANES), 1)
    fq_all = fq_ref[...]
    contract_last = (((1,), (1,)), ((), ()))

    qs = []
    for hh in range(heads_per_step):
        off = (lane % (3 * n_heads)) - 3 * (head0 + hh)
        keep = jnp.where(off >= 0, jnp.where(off < 3, 1.0, 0.0), 0.0)
        keep = jnp.where(lane < 6 * n_heads, keep, 0.0).astype(BF16)
        qs.append(jnp.concatenate([q_ref[:, hh * HEAD_DIM:(hh + 1) * HEAD_DIM],
                                   fq_all * keep], axis=1))

    def update(kb, items, diagonal):
        scores = []
        for hh, row0, n_rows, n_keys, _ in items:
            rows = pl.ds(pl.multiple_of(kb * blk, blk), n_keys)
            cols = slice(hh * HEAD_DIM, (hh + 1) * HEAD_DIM)
            k = jnp.concatenate([k_ref[rows, cols], fk_ref[rows, :]], axis=1)
            s2 = lax.dot_general(qs[hh][row0:row0 + n_rows, :], k, contract_last,
                                 preferred_element_type=F32)
            if diagonal:
                qi = lax.broadcasted_iota(jnp.int32, (n_rows, n_keys), 0) + row0
                kj = lax.broadcasted_iota(jnp.int32, (n_rows, n_keys), 1)
                s2 = jnp.where(kj <= qi, s2, NEG_INF)
            scores.append((s2, rows, cols))
        probs = []
        for (s2, _, _), (_, _, _, _, (m_run, l_run, acc)) in zip(scores, items):
            m_new = jnp.maximum(m_run, jnp.max(s2, axis=-1, keepdims=True))
            alpha = jnp.exp2(m_run - m_new)
            p = jnp.exp2(s2 - m_new)
            l_new = alpha * l_run + jnp.sum(p, axis=-1, keepdims=True)
            probs.append((p.astype(BF16), m_new, l_new, alpha * acc))
        out = []
        for (p, m_new, l_new, acc_scaled), (_, rows, cols) in zip(probs, scores):
            acc_new = acc_scaled + jnp.dot(p, v_ref[rows, cols], preferred_element_type=F32)
            out.append((m_new, l_new, acc_new))
        return tuple(out)

    def below_diagonal(kb, carry):
        return update(kb, [(hh, 0, blk, blk, carry[hh]) for hh in range(heads_per_step)], False)

    init = tuple((jnp.full((blk, 1), NEG_INF, F32), jnp.zeros((blk, 1), F32),
                  jnp.zeros((blk, HEAD_DIM), F32)) for _ in range(heads_per_step))
    carry = lax.fori_loop(0, qt, below_diagonal, init)

    items = []
    for hh in range(heads_per_step):
        for row0 in range(0, blk, diag_rows):
            piece_carry = tuple(c[row0:row0 + diag_rows] for c in carry[hh])
            items.append((hh, row0, diag_rows, row0 + diag_rows, piece_carry))
    for (hh, row0, _, _, _), (_, l_fin, acc) in zip(items, update(qt, items, True)):
        o_ref[row0:row0 + diag_rows, hh * HEAD_DIM:(hh + 1) * HEAD_DIM] = (
            acc / l_fin).astype(o_ref.dtype)


def _fox_attention(q, kv, f_query, f_keys, bsz, seq, n_heads):
    blk = _pick_tile(seq, (FOX_BLOCK, 256, 128))
    nkb = seq // blk
    width = n_heads * HEAD_DIM
    hps = FOX_HEADS_PER_STEP
    assert n_heads % hps == 0
    n_hg = n_heads // hps
    qv = q.reshape(bsz, seq, width)
    kvv = kv.reshape(bsz, seq, 2 * width)
    o = pl.pallas_call(
        functools.partial(_fox_kernel, blk=blk, diag_rows=min(blk, FOX_DIAG_ROWS), n_heads=n_heads,
                          heads_per_step=hps),
        grid=(bsz, n_hg, nkb),
        in_specs=[
            pl.BlockSpec((None, blk, hps * HEAD_DIM), lambda b, h, t: (b, t, h)),
            pl.BlockSpec((None, seq, hps * HEAD_DIM), lambda b, h, t: (b, 0, h)),
            pl.BlockSpec((None, seq, hps * HEAD_DIM), lambda b, h, t: (b, 0, n_hg + h)),
            pl.BlockSpec((None, blk, LANES), lambda b, h, t: (b, t, 0)),
            pl.BlockSpec((None, seq, LANES), lambda b, h, t: (b, 0, 0)),
        ],
        out_specs=pl.BlockSpec((None, blk, hps * HEAD_DIM), lambda b, h, t: (b, t, h)),
        out_shape=jax.ShapeDtypeStruct((bsz, seq, width), BF16),
        compiler_params=_compiler_params(("parallel", "parallel", "arbitrary")),
        name="fox_attention",
    )(qv, kvv, kvv, f_query.reshape(bsz, seq, LANES), f_keys.reshape(bsz, seq, LANES))
    return o.reshape(bsz * seq, width)


def _rotary_lane_order():
    half = ROT_DIM // 2
    mid = HEAD_DIM // 2
    return np.concatenate([np.arange(0, half), np.arange(ROT_DIM, mid + half),
                           np.arange(half, ROT_DIM), np.arange(mid + half, HEAD_DIM)])


def _rope_tables(seq, dilations):
    half = ROT_DIM // 2
    mid = HEAD_DIM // 2
    inv = ROPE_THETA ** (-jnp.arange(0, ROT_DIM, 2, dtype=F32) / ROT_DIM)
    ang = jnp.arange(seq, dtype=F32)[:, None] * inv[None, :]
    cos, sin = jnp.cos(ang), jnp.sin(ang)
    ones = jnp.ones((seq, mid - half), F32)
    zeros = jnp.zeros((seq, mid - half), F32)
    cos_t = jnp.concatenate([cos, ones, cos, ones], axis=-1)
    sin_t = jnp.concatenate([-sin, zeros, sin, zeros], axis=-1)

    def deinterleave(t, d):
        t = t.reshape(seq // DEINT_ROWS, DEINT_ROWS // d, d, HEAD_DIM)
        return jnp.swapaxes(t, 1, 2).reshape(seq, HEAD_DIM)

    return (jnp.stack([deinterleave(cos_t, d) for d in dilations]),
            jnp.stack([deinterleave(sin_t, d) for d in dilations]))


def _split_mods(mods, parts):
    bsz, n = mods.shape
    d = n // parts
    return [mods[:, p * d:(p + 1) * d].reshape(bsz, 1, d) for p in range(parts)]


def kernel(x, c, w_ada, b_ada, g_norm_attn, g_norm_ffn, w_qkv_a, g_qk_a, w_o_a, w_ada_kv, b_ada_kv, g_norm_kv, w_kv, g_k_b, w_f, b_f, w_q_b, g_q_b, w_o_b, w_ffn_in, w_ffn_out):
    bsz, seq, d = x.shape
    depth = w_ada.shape[0]
    n_a = w_qkv_a.shape[0]
    n_groups = g_qk_a.shape[2]
    width_a = w_o_a.shape[1]
    n_heads_b = w_f.shape[1]
    assert n_groups == len(DIL_CONFIGS) and n_heads_b <= LANES and width_a == PROJ_COLS
    dilations = tuple(dl for _, dl in DIL_CONFIGS)

    x2 = x.reshape(bsz * seq, d)
    rope_tables = _rope_tables(seq, dilations)

    w_qkv = _qkv_weights(w_qkv_a, n_groups)
    g_qk = g_qk_a[..., _rotary_lane_order()]
    w_o_a = w_o_a.astype(BF16)
    w_kv = w_kv.astype(BF16)[None]
    w_q_b = w_q_b.astype(BF16)
    w_o_b = w_o_b.astype(BF16)
    w_ffn_in = w_ffn_in.astype(BF16)
    w_ffn_out = w_ffn_out.astype(BF16)
    wf_pad = jnp.pad(w_f, ((0, 0), (0, LANES - n_heads_b))).astype(BF16)
    bf_pad = jnp.pad(b_f, (0, LANES - n_heads_b)).reshape(1, LANES)

    kv = f_keys = f_query = None
    for layer in range(depth):
        sh_a, sc_a, gt_a, sh_f, sc_f, gt_f = _split_mods(_mods(c, w_ada, b_ada, layer), 6)
        if layer < n_a:
            gains = g_qk[layer].reshape(2 * n_groups, 1, HEAD_DIM)
            h_var = _deint_lhs(x2, seq, g_norm_attn[layer], sh_a, sc_a, dilations)
            qkv = _qkv_proj(h_var, seq, w_qkv, layer, gains, rope_tables)
            outs = [_dilated_group(qkv, bsz, seq, g, n_groups, width_a, window, dilation)
                    for g, (window, dilation) in enumerate(DIL_CONFIGS)]
            x2 = _mix_out([o for o, _ in outs], [l for _, l in outs], w_o_a, layer, x2, seq, gt_a)
        else:
            i = layer - n_a
            q = _proj(x2, seq, g_norm_attn[layer], sh_a, sc_a, w_q_b, i,
                      g_q_b[i].reshape(1, HEAD_DIM),
                      n_norm_tiles=w_q_b.shape[-1] // PROJ_COLS,
                      out_scale=HEAD_DIM ** -0.5 * LOG2E)
            o = _fox_attention(q, kv, f_query, f_keys, bsz, seq, n_heads_b)
            x2 = _mm_residual(o, w_o_b, i, x2, seq, gt_a)
        a = _swiglu_in(x2, seq, g_norm_ffn[layer], sh_f, sc_f, w_ffn_in, layer)
        x2 = _mm_residual(a, w_ffn_out, layer, x2, seq, gt_f)
        if layer == n_a - 1:
            sh_kv, sc_kv = _split_mods(_mods(c, w_ada_kv[None], b_ada_kv[None], 0), 2)
            kv, f_keys, f_query = _proj(
                x2, seq, g_norm_kv, sh_kv, sc_kv, w_kv, 0, g_k_b.reshape(1, HEAD_DIM),
                n_norm_tiles=w_kv.shape[-1] // 2 // PROJ_COLS, forget=(wf_pad, bf_pad, n_heads_b))
    return x2.reshape(bsz, seq, d)
```

```python
import functools
import math

import numpy as np
import jax
import jax.numpy as jnp
from jax import lax
from jax.experimental import pallas as pl
from jax.experimental.pallas import tpu as pltpu

HEAD_DIM = 128
DIL_CONFIGS = ((128, 1), (512, 4), (2048, 16))
ROT_DIM = HEAD_DIM // 4
ROPE_THETA = 500000.0
ATT_BLOCK = 128
DILATED_UNITS = 2
EPS = 1e-6
NEG_INF = -1e30
LOG2E = math.log2(math.e)
LN2 = math.log(2.0)

LANES = 128
BF16_SUBLANES = 16
VMEM_LIMIT_BYTES = 56 * 1024 * 1024

PROJ_ROWS = 1024
PROJ_COLS = 1024
RESIDUAL_ROWS = 512
FILL_ROWS = 128
CUMSUM_ROWS = 256
DEINT_ROWS = 512
PIECE_ROWS = 256
FOX_BLOCK = 512
FOX_DIAG_ROWS = 512
FOX_HEADS_PER_STEP = 8

F32 = jnp.float32
BF16 = jnp.bfloat16


def _compiler_params(semantics):
    return pltpu.CompilerParams(dimension_semantics=semantics, vmem_limit_bytes=VMEM_LIMIT_BYTES)


def _pick_tile(n, candidates):
    for t in candidates:
        if n % t == 0:
            return t
    raise ValueError(f"no tile in {candidates} divides {n}")


def _head_rmsnorm(a, gain):
    return a * lax.rsqrt(jnp.mean(a * a, axis=-1, keepdims=True) + EPS) * gain


def _silu(v):
    return v * jax.nn.sigmoid(v)


def _mods_kernel(c_ref, w_ref, b_ref, o_ref):
    c_act = _silu(c_ref[...])
    o_ref[...] = jnp.dot(c_act.astype(BF16), w_ref[...].astype(BF16),
                         preferred_element_type=F32) + b_ref[...]


def _mods(c, w, b, layer):
    bsz, d = c.shape
    n = w.shape[-1]
    tn = _pick_tile(n, (512, 256, 128))
    return pl.pallas_call(
        _mods_kernel,
        grid=(n // tn,),
        in_specs=[
            pl.BlockSpec((bsz, d), lambda j: (0, 0)),
            pl.BlockSpec((None, d, tn), lambda j: (layer, 0, j)),
            pl.BlockSpec((None, 1, tn), lambda j: (layer, 0, j)),
        ],
        out_specs=pl.BlockSpec((bsz, tn), lambda j: (0, j)),
        out_shape=jax.ShapeDtypeStruct((bsz, n), F32),
        compiler_params=_compiler_params(("parallel",)),
        name="mods",
    )(c, w, b.reshape(b.shape[0], 1, n))


def _normmod_rows(x_ref, rows, gain, mul, shift):
    xf = x_ref[rows, :]
    y = xf * lax.rsqrt(jnp.mean(xf * xf, axis=-1, keepdims=True) + EPS) * gain
    return y * mul + shift


def _row_pieces(tm):
    return [slice(r, r + PIECE_ROWS) for r in range(0, tm, PIECE_ROWS)]


def _lhs_pieces(first, x_ref, gn_ref, sh_ref, sc_ref, h_scr):
    gain = gn_ref[...]
    mul = 1.0 + sc_ref[...]
    shift = sh_ref[...]
    for rows in _row_pieces(h_scr.shape[0]):
        if first:
            h_scr[rows, :] = _normmod_rows(x_ref, rows, gain, mul, shift).astype(BF16)
        yield rows, h_scr[rows, :]


def _deint_lhs_kernel(x_ref, gn_ref, sh_ref, sc_ref, o_ref, slab_scr, *, dilations, bases):
    tm, d_model = x_ref.shape
    n_slabs = d_model // LANES
    gain = gn_ref[...]
    mul = 1.0 + sc_ref[...]
    shift = sh_ref[...]

    def natural(it, carry):
        rows = pl.ds(pl.multiple_of(it * FILL_ROWS, FILL_ROWS), FILL_ROWS)
        ssq = None
        for s in range(n_slabs):
            xs = x_ref[rows, s * LANES:(s + 1) * LANES]
            ssq = xs * xs if ssq is None else ssq + xs * xs
        inv = lax.rsqrt(jnp.sum(ssq, axis=-1, keepdims=True) * (1.0 / d_model) + EPS)
        inv = jnp.broadcast_to(inv, (FILL_ROWS, LANES))
        for s in range(n_slabs):
            cols = slice(s * LANES, (s + 1) * LANES)
            h = x_ref[rows, cols] * inv * gain[:, cols] * mul[:, cols] + shift[:, cols]
            slab_scr[0, s, rows, :] = h
            for v, d in enumerate(dilations):
                if d == 1:
                    o_ref[v, rows, cols] = h.astype(BF16)
        return carry

    lax.fori_loop(0, tm // FILL_ROWS, natural, 0)

    for v, d in enumerate(dilations):
        if d == 1:
            continue
        p = max(b for b in bases if d % b == 0 and b < d)
        f = d // p
        src_slabs = slab_scr.at[bases.index(p)]
        dst_slabs = slab_scr.at[bases.index(d)] if d in bases else None
        per_residue = tm // d
        units_per_trip = max(1, FILL_ROWS // per_residue)

        def gather(it, carry, v=v, d=d, p=p, f=f, per_residue=per_residue,
                   units_per_trip=units_per_trip, src_slabs=src_slabs, dst_slabs=dst_slabs):
            for u in range(units_per_trip):
                r = it * units_per_trip + u
                k = r // p
                rp = r % p
                src = pl.ds(rp * (tm // p) + k, per_residue, stride=f)
                dst = pl.ds(pl.multiple_of(r * per_residue, per_residue), per_residue)
                for s in range(n_slabs):
                    piece = src_slabs[s, src, :]
                    o_ref[v, dst, s * LANES:(s + 1) * LANES] = piece.astype(BF16)
                    if dst_slabs is not None:
                        dst_slabs[s, dst, :] = piece
            return carry

        lax.fori_loop(0, d // units_per_trip, gather, 0)


def _deint_lhs(x2, seq, gnorm, shift, scale, dilations):
    m, d = x2.shape
    tm = DEINT_ROWS
    assert seq % tm == 0 and all(tm % (dl * BF16_SUBLANES) == 0 for dl in dilations)
    tpb = seq // tm
    n_var = len(dilations)
    assert list(dilations) == sorted(dilations) and dilations[0] == 1
    bases = [1]
    for dl in dilations[1:]:
        base = max(b for b in dilations if dl % b == 0 and b < dl)
        if base not in bases:
            bases.append(base)
    bases = tuple(bases)
    return pl.pallas_call(
        functools.partial(_deint_lhs_kernel, dilations=dilations, bases=bases),
        grid=(m // tm,),
        in_specs=[
            pl.BlockSpec((tm, d), lambda i: (i, 0)),
            pl.BlockSpec((1, d), lambda i: (0, 0)),
            pl.BlockSpec((None, 1, d), lambda i: (i // tpb, 0, 0)),
            pl.BlockSpec((None, 1, d), lambda i: (i // tpb, 0, 0)),
        ],
        out_specs=pl.BlockSpec((n_var, tm, d), lambda i: (0, i, 0)),
        out_shape=jax.ShapeDtypeStruct((n_var, m, d), BF16),
        scratch_shapes=[pltpu.VMEM((len(bases), d // LANES, tm, LANES), F32)],
        compiler_params=_compiler_params(("parallel",)),
        name="deint_lhs",
    )(x2, gnorm.reshape(1, d), shift, scale)


def _dot_head_norm(pieces, w_ref, o_ref, acc_scr, gain, cos_ref=None, sin_ref=None):
    tn = acc_scr.shape[1]
    w = w_ref[...]
    done = []
    for rows, lhs in pieces:
        acc_scr[rows, :] = jnp.dot(lhs, w, preferred_element_type=F32)
        done.append(rows)
    for rows in done:
        if cos_ref is not None:
            cos = cos_ref[rows, :]
            sin = sin_ref[rows, :]
        for h in range(tn // HEAD_DIM):
            cols = slice(h * HEAD_DIM, (h + 1) * HEAD_DIM)
            y = _head_rmsnorm(acc_scr[rows, cols], gain)
            if cos_ref is not None:
                y = y * cos + pltpu.roll(y, HEAD_DIM // 2, 1) * sin
            o_ref[rows, cols] = y.astype(o_ref.dtype)


def _qkv_weight_kernel(w_ref, perm_ref, o_ref, *, n_perm_tiles):
    w = w_ref[...].astype(BF16)

    @pl.when(pl.program_id(1) < n_perm_tiles)
    def _():
        for h in range(w.shape[1] // HEAD_DIM):
            cols = slice(h * HEAD_DIM, (h + 1) * HEAD_DIM)
            o_ref[:, cols] = jnp.dot(w[:, cols], perm_ref[...],
                                     preferred_element_type=F32).astype(BF16)

    @pl.when(pl.program_id(1) >= n_perm_tiles)
    def _():
        o_ref[...] = w


def _qkv_weights(w, n_groups):
    n_layers, d, n = w.shape
    tn = PROJ_COLS
    assert n == 3 * n_groups * tn
    order = _rotary_lane_order()
    perm = np.zeros((HEAD_DIM, HEAD_DIM), np.float32)
    perm[order, np.arange(HEAD_DIM)] = 1.0
    return pl.pallas_call(
        functools.partial(_qkv_weight_kernel, n_perm_tiles=2 * n_groups),
        grid=(n_layers, n // tn),
        in_specs=[
            pl.BlockSpec((None, d, tn), lambda l, j: (l, 0, j)),
            pl.BlockSpec((HEAD_DIM, HEAD_DIM), lambda l, j: (0, 0)),
        ],
        out_specs=pl.BlockSpec((None, d, tn), lambda l, j: (l, 0, j)),
        out_shape=jax.ShapeDtypeStruct(w.shape, BF16),
        compiler_params=_compiler_params(("parallel", "parallel")),
        name="qkv_weights",
    )(w, jnp.asarray(perm, BF16))


def _qkv_kernel(h_ref, w_ref, gain_ref, cos_ref, sin_ref, o_ref, acc_scr, *, n_parts):
    part = pl.program_id(1) % n_parts

    @pl.when(part < 2)
    def _():
        gain = gain_ref[...] * jnp.where(part == 0, HEAD_DIM ** -0.5 * LOG2E, 1.0)
        pieces = [(rows, h_ref[rows, :]) for rows in _row_pieces(h_ref.shape[0])]
        _dot_head_norm(pieces, w_ref, o_ref, acc_scr, gain, cos_ref, sin_ref)

    @pl.when(part >= 2)
    def _():
        o_ref[...] = jnp.dot(h_ref[...], w_ref[...],
                             preferred_element_type=F32).astype(o_ref.dtype)


def _qkv_proj(h_var, seq, w, layer, gains, rope_tables):
    n_groups, m, d = h_var.shape
    n = w.shape[-1]
    n_parts = 3
    tm = PROJ_ROWS
    tn = PROJ_COLS
    assert seq % tm == 0 and n == n_parts * n_groups * tn
    tpb = seq // tm

    def col(j):
        return (j % n_parts) * n_groups + j // n_parts

    return pl.pallas_call(
        functools.partial(_qkv_kernel, n_parts=n_parts),
        grid=(m // tm, n_parts * n_groups),
        in_specs=[
            pl.BlockSpec((None, tm, d), lambda i, j: (j // n_parts, i, 0)),
            pl.BlockSpec((None, d, tn), lambda i, j: (layer, 0, col(j))),
            pl.BlockSpec((None, 1, HEAD_DIM),
                         lambda i, j: (jnp.minimum(col(j), 2 * n_groups - 1), 0, 0)),
            pl.BlockSpec((None, tm, HEAD_DIM), lambda i, j: (j // n_parts, i % tpb, 0)),
            pl.BlockSpec((None, tm, HEAD_DIM), lambda i, j: (j // n_parts, i % tpb, 0)),
        ],
        out_specs=pl.BlockSpec((tm, tn), lambda i, j: (i, col(j))),
        out_shape=jax.ShapeDtypeStruct((m, n), BF16),
        scratch_shapes=[pltpu.VMEM((tm, tn), F32)],
        compiler_params=_compiler_params(("parallel", "arbitrary")),
        name="qkv_proj",
    )(h_var, w, gains, *rope_tables)


def _split3_bf16(v):
    hi = v.astype(BF16)
    rem = v - hi.astype(F32)
    mid = rem.astype(BF16)
    lo = (rem - mid.astype(F32)).astype(BF16)
    return hi, mid, lo


def _proj_kernel(*refs, n_norm_tiles, forget, tiles_per_batch, out_scale):
    x_ref, gn_ref, sh_ref, sc_ref, w_ref, gain_ref = refs[:6]
    pos = 6
    if forget:
        wf_ref, bf_ref, tri_ref, place_ref, ones_ref = refs[pos:pos + 5]
        pos += 5
    o_ref = refs[pos]
    pos += 1
    if forget:
        fk_ref, fq_ref = refs[pos:pos + 2]
        pos += 2
    h_scr, acc_scr = refs[pos:pos + 2]
    pos += 2
    if forget:
        carry_scr = refs[pos]

    i = pl.program_id(0)
    j = pl.program_id(1)
    tm = h_scr.shape[0]
    gain = gain_ref[...] * out_scale

    @pl.when(j == 0)
    def _():
        pieces = _lhs_pieces(True, x_ref, gn_ref, sh_ref, sc_ref, h_scr)
        _dot_head_norm(pieces, w_ref, o_ref, acc_scr, gain)
        if forget:
            z = jnp.dot(h_scr[...], wf_ref[...], preferred_element_type=F32) + bf_ref[...]
            log_f = jnp.minimum(z, 0.0) - jnp.log1p(jnp.exp(-jnp.abs(z)))
            @pl.when(i % tiles_per_batch == 0)
            def _():
                carry_scr[...] = jnp.zeros_like(carry_scr)

            tri = tri_ref[...]
            parts = _split3_bf16(log_f)
            running = carry_scr[...]
            blocks = []
            for r0 in range(0, tm, CUMSUM_ROWS):
                blk_cum = running
                for part in parts:
                    blk_cum = blk_cum + jnp.dot(tri, part[r0:r0 + CUMSUM_ROWS, :],
                                                preferred_element_type=F32)
                running = blk_cum[CUMSUM_ROWS - 1:CUMSUM_ROWS, :]
                blocks.append(blk_cum)
            carry_scr[...] = running
            cum = jnp.concatenate(blocks, axis=0)
            bias = ones_ref[...]
            for p, part in enumerate(_split3_bf16(cum * (-LOG2E))):
                bias = bias + jnp.dot(part, place_ref[p], preferred_element_type=F32)
            fk_ref[...] = bias[:, :LANES].astype(BF16)
            fq_ref[...] = bias[:, LANES:].astype(BF16)

    @pl.when((j > 0) & (j < n_norm_tiles))
    def _():
        pieces = _lhs_pieces(False, x_ref, gn_ref, sh_ref, sc_ref, h_scr)
        _dot_head_norm(pieces, w_ref, o_ref, acc_scr, gain)

    @pl.when(j >= n_norm_tiles)
    def _():
        o_ref[...] = jnp.dot(h_scr[...], w_ref[...],
                             preferred_element_type=F32).astype(o_ref.dtype)


def _fox_bias_lanes(n_heads):
    assert 6 * n_heads <= LANES
    place = np.zeros((3, LANES, 2 * LANES), np.float32)
    ones = np.zeros((1, 2 * LANES), np.float32)
    for h in range(n_heads):
        for p in range(3):
            place[p, h, 3 * h + p] = 1.0
            place[p, h, LANES + 3 * (n_heads + h) + p] = -1.0
    ones[0, 3 * n_heads:6 * n_heads] = 1.0
    ones[0, LANES:LANES + 3 * n_heads] = 1.0
    return jnp.asarray(place, BF16), jnp.asarray(ones, F32)


def _proj(x2, seq, gnorm, shift, scale, w, layer, gain, n_norm_tiles, forget=None,
          out_scale=1.0):
    m, d = x2.shape
    n = w.shape[-1]
    tm = PROJ_ROWS
    tn = PROJ_COLS
    assert seq % tm == 0 and n % tn == 0
    tpb = seq // tm
    with_forget = forget is not None

    in_specs = [
        pl.BlockSpec((tm, d), lambda i, j: (i, 0)),
        pl.BlockSpec((1, d), lambda i, j: (0, 0)),
        pl.BlockSpec((None, 1, d), lambda i, j: (i // tpb, 0, 0)),
        pl.BlockSpec((None, 1, d), lambda i, j: (i // tpb, 0, 0)),
        pl.BlockSpec((None, d, tn), lambda i, j: (layer, 0, j)),
        pl.BlockSpec((1, HEAD_DIM), lambda i, j: (0, 0)),
    ]
    args = [x2, gnorm.reshape(1, d), shift, scale, w, gain]
    out_specs = [pl.BlockSpec((tm, tn), lambda i, j: (i, j))]
    out_shape = [jax.ShapeDtypeStruct((m, n), BF16)]
    scratch = [pltpu.VMEM((tm, d), BF16), pltpu.VMEM((tm, tn), F32)]
    if with_forget:
        wf_pad, bf_pad, n_heads = forget
        tri = (lax.broadcasted_iota(jnp.int32, (CUMSUM_ROWS, CUMSUM_ROWS), 0)
               >= lax.broadcasted_iota(jnp.int32, (CUMSUM_ROWS, CUMSUM_ROWS), 1)).astype(BF16)
        place, ones = _fox_bias_lanes(n_heads)
        in_specs += [
            pl.BlockSpec((d, LANES), lambda i, j: (0, 0)),
            pl.BlockSpec((1, LANES), lambda i, j: (0, 0)),
            pl.BlockSpec((CUMSUM_ROWS, CUMSUM_ROWS), lambda i, j: (0, 0)),
            pl.BlockSpec((3, LANES, 2 * LANES), lambda i, j: (0, 0, 0)),
            pl.BlockSpec((1, 2 * LANES), lambda i, j: (0, 0)),
        ]
        args += [wf_pad, bf_pad, tri, place, ones]
        out_specs += [
            pl.BlockSpec((tm, LANES), lambda i, j: (i, 0)),
            pl.BlockSpec((tm, LANES), lambda i, j: (i, 0)),
        ]
        out_shape += [
            jax.ShapeDtypeStruct((m, LANES), BF16),
            jax.ShapeDtypeStruct((m, LANES), BF16),
        ]
        scratch.append(pltpu.VMEM((1, LANES), F32))

    kern = functools.partial(_proj_kernel, n_norm_tiles=n_norm_tiles, forget=with_forget,
                             tiles_per_batch=tpb, out_scale=out_scale)
    outs = pl.pallas_call(
        kern,
        grid=(m // tm, n // tn),
        in_specs=in_specs,
        out_specs=out_specs,
        out_shape=out_shape,
        scratch_shapes=scratch,
        compiler_params=_compiler_params(("arbitrary", "arbitrary")),
        name="proj",
    )(*args)
    return outs if with_forget else outs[0]


def _swiglu_kernel(x_ref, gn_ref, sh_ref, sc_ref, wg_ref, wu_ref, o_ref, h_scr):
    def gated(h):
        g = jnp.dot(h, wg_ref[...], preferred_element_type=F32)
        u = jnp.dot(h, wu_ref[...], preferred_element_type=F32)
        return (_silu(g) * u).astype(o_ref.dtype)

    @pl.when(pl.program_id(1) == 0)
    def _():
        for rows, h in _lhs_pieces(True, x_ref, gn_ref, sh_ref, sc_ref, h_scr):
            o_ref[rows, :] = gated(h)

    @pl.when(pl.program_id(1) > 0)
    def _():
        o_ref[...] = gated(h_scr[...])


def _swiglu_in(x2, seq, gnorm, shift, scale, w, layer):
    m, d = x2.shape
    f = w.shape[-1] // 2
    tm = PROJ_ROWS
    tf = _pick_tile(f, (512, 256, 128))
    assert seq % tm == 0
    tpb = seq // tm
    nf = f // tf
    return pl.pallas_call(
        _swiglu_kernel,
        grid=(m // tm, nf),
        in_specs=[
            pl.BlockSpec((tm, d), lambda i, j: (i, 0)),
            pl.BlockSpec((1, d), lambda i, j: (0, 0)),
            pl.BlockSpec((None, 1, d), lambda i, j: (i // tpb, 0, 0)),
            pl.BlockSpec((None, 1, d), lambda i, j: (i // tpb, 0, 0)),
            pl.BlockSpec((None, d, tf), lambda i, j: (layer, 0, j)),
            pl.BlockSpec((None, d, tf), lambda i, j: (layer, 0, nf + j)),
        ],
        out_specs=pl.BlockSpec((tm, tf), lambda i, j: (i, j)),
        out_shape=jax.ShapeDtypeStruct((m, f), BF16),
        scratch_shapes=[pltpu.VMEM((tm, d), BF16)],
        compiler_params=_compiler_params(("parallel", "arbitrary")),
        name="swiglu_in",
    )(x2, gnorm.reshape(1, d), shift, scale, w, w)


def _mmres_kernel(a_ref, w_ref, x_ref, gt_ref, o_ref):
    y = jnp.dot(a_ref[...], w_ref[...], preferred_element_type=F32)
    o_ref[...] = x_ref[...] + gt_ref[...] * y


def _mm_residual(a, w, layer, x2, seq, gate):
    m, k = a.shape
    n = w.shape[-1]
    tm = RESIDUAL_ROWS
    assert seq % tm == 0
    tpb = seq // tm
    return pl.pallas_call(
        _mmres_kernel,
        grid=(m // tm,),
        in_specs=[
            pl.BlockSpec((tm, k), lambda i: (i, 0)),
            pl.BlockSpec((None, k, n), lambda i: (layer, 0, 0), pipeline_mode=pl.Buffered(1)),
            pl.BlockSpec((tm, n), lambda i: (i, 0)),
            pl.BlockSpec((None, 1, n), lambda i: (i // tpb, 0, 0)),
        ],
        out_specs=pl.BlockSpec((tm, n), lambda i: (i, 0)),
        out_shape=jax.ShapeDtypeStruct((m, n), F32),
        compiler_params=_compiler_params(("parallel",)),
        name="mm_residual",
    )(a, w, x2, gate)


def _dilated_kernel(*refs, n_back, n_heads, dilation, has_prev, units):
    ins = refs[:3 * units]
    o_ref, lse_ref, o_scr = refs[3 * units:3 * units + 3]
    if has_prev:
        kp_scr, vp_scr = refs[3 * units + 3:]
    step_blk = pl.program_id(1)
    step_res = pl.program_id(2)
    width = n_heads * HEAD_DIM
    n_keys = (2 if has_prev else 1) * ATT_BLOCK
    qi = lax.broadcasted_iota(jnp.int32, (ATT_BLOCK, n_keys), 0)
    kj = lax.broadcasted_iota(jnp.int32, (ATT_BLOCK, n_keys), 1)
    dist = qi + (n_keys - ATT_BLOCK) - kj
    band = (dist >= 0) & (dist <= n_back)
    lane = lax.broadcasted_iota(jnp.int32, (ATT_BLOCK, LANES), 1)
    contract_last = (((1,), (1,)), ((), ()))
    head_cols = [slice(h * HEAD_DIM, (h + 1) * HEAD_DIM) for h in range(n_heads)]

    blocks = []
    for u in range(units):
        q_ref, kc_ref, vc_ref = ins[3 * u:3 * u + 3]
        if dilation == 1:
            blk, res = step_blk * units + u, 0
            rows = slice(u * ATT_BLOCK, (u + 1) * ATT_BLOCK)
        else:
            blk, res = step_blk, step_res * units + u
            rows = pl.ds(res, ATT_BLOCK, stride=dilation)
        q = q_ref[...].reshape(ATT_BLOCK, width)
        k = kc_ref[...].reshape(ATT_BLOCK, width)
        v = vc_ref[...].reshape(ATT_BLOCK, width)
        mask = band
        if has_prev:
            kp_ref = kp_scr.at[res]
            vp_ref = vp_scr.at[res]
            if dilation > 1 or u == 0:
                @pl.when(blk == 0)
                def _(kp_ref=kp_ref, vp_ref=vp_ref):
                    kp_ref[...] = jnp.zeros_like(kp_ref)
                    vp_ref[...] = jnp.zeros_like(vp_ref)
            k_prev = kp_ref[...]
            v_prev = vp_ref[...]
            kp_ref[...] = k
            vp_ref[...] = v
            k = jnp.concatenate([k_prev, k], axis=0)
            v = jnp.concatenate([v_prev, v], axis=0)
            mask = band & ((kj >= ATT_BLOCK) | (blk > 0))
        blocks.append((q, k, v, mask, rows))

    scores = [[lax.dot_general(q[:, cols], k[:, cols], contract_last,
                               preferred_element_type=F32) for cols in head_cols]
              for q, k, _, _, _ in blocks]
    probs = []
    for (_, _, _, mask, rows), unit_scores in zip(blocks, scores):
        unit_probs = []
        lse_tile = jnp.zeros((ATT_BLOCK, LANES), F32)
        for h, s2 in enumerate(unit_scores):
            s2 = jnp.where(mask, s2, NEG_INF)
            mx2 = jnp.max(s2, axis=-1, keepdims=True)
            p = jnp.exp2(s2 - mx2)
            den = jnp.sum(p, axis=-1, keepdims=True)
            unit_probs.append((p.astype(BF16), den))
            lse_tile = jnp.where(lane == h, mx2 * LN2 + jnp.log(den), lse_tile)
        lse_ref[rows, :] = lse_tile
        probs.append(unit_probs)
    for (_, _, v, _, rows), unit_probs in zip(blocks, probs):
        for h, (p, den) in enumerate(unit_probs):
            o = jnp.dot(p, v[:, head_cols[h]], preferred_element_type=F32)
            o_scr[h, rows, :] = o / den

    @pl.when(step_res == pl.num_programs(2) - 1)
    def _():
        for h in range(n_heads):
            o_ref[:, h * HEAD_DIM:(h + 1) * HEAD_DIM] = o_scr[h].astype(o_ref.dtype)


def _dilated_group(qkv, bsz, seq, group, n_groups, width, window, dilation):
    n_back = window // dilation
    sub_len = seq // dilation
    assert sub_len % ATT_BLOCK == 0
    nb = sub_len // ATT_BLOCK
    n_cols = qkv.shape[1]
    tpb = seq // DEINT_ROWS
    per_residue = DEINT_ROWS // dilation

    if per_residue >= ATT_BLOCK:
        bpt = per_residue // ATT_BLOCK
        view = qkv.reshape(bsz, tpb, dilation, bpt, ATT_BLOCK, n_cols)
        block = (None, None, None, None, ATT_BLOCK, width)

        def rows_index(b, n, r):
            return (b, n // bpt, r, n % bpt, 0)
    else:
        pieces = ATT_BLOCK // per_residue
        view = qkv.reshape(bsz, tpb, dilation, per_residue, n_cols)
        block = (None, pieces, None, per_residue, width)

        def rows_index(b, n, r):
            return (b, n, r, 0)

    units = DILATED_UNITS if (nb if dilation == 1 else dilation) % DILATED_UNITS == 0 else 1

    def spec(part, u):
        def index(b, n, r):
            if dilation == 1:
                rows = rows_index(b, n * units + u, r)
            else:
                rows = rows_index(b, n, r * units + u)
            return rows + (part * n_groups + group,)
        return pl.BlockSpec(block, index)

    if dilation == 1:
        grid = (bsz, nb // units, 1)
        span = ATT_BLOCK * units
    else:
        grid = (bsz, nb, dilation // units)
        span = ATT_BLOCK * dilation
    has_prev = nb > 1
    in_specs = [spec(part, u) for u in range(units) for part in range(3)]
    scratch = [pltpu.VMEM((width // HEAD_DIM, span, HEAD_DIM), F32)]
    if has_prev:
        scratch += [pltpu.VMEM((dilation, ATT_BLOCK, width), BF16)] * 2
    o, lse = pl.pallas_call(
        functools.partial(_dilated_kernel, n_back=n_back, n_heads=width // HEAD_DIM,
                          dilation=dilation, has_prev=has_prev, units=units),
        grid=grid,
        in_specs=in_specs,
        out_specs=[
            pl.BlockSpec((None, span, width), lambda b, n, r: (b, n, 0)),
            pl.BlockSpec((None, span, LANES), lambda b, n, r: (b, n, 0)),
        ],
        out_shape=[
            jax.ShapeDtypeStruct((bsz, seq, width), BF16),
            jax.ShapeDtypeStruct((bsz, seq, LANES), F32),
        ],
        scratch_shapes=scratch,
        compiler_params=_compiler_params(("parallel", "arbitrary", "arbitrary")),
        name="dilated_attention",
    )(*([view] * len(in_specs)))
    return o.reshape(bsz * seq, width), lse.reshape(bsz * seq, LANES)


def _mix_out_kernel(*refs, n_groups, n_heads, row_chunk):
    o_refs = refs[:n_groups]
    lse_refs = refs[n_groups:2 * n_groups]
    w_ref, x_ref, gt_ref, out_ref, lhs_scr = refs[2 * n_groups:]
    tm = x_ref.shape[0]
    w = w_ref[...]
    gate = gt_ref[...]

    for r in range(0, tm, row_chunk):
        rows = slice(r, r + row_chunk)
        lses = [ref[rows, :] for ref in lse_refs]
        mx = functools.reduce(jnp.maximum, lses)
        es = [jnp.exp(l - mx) for l in lses]
        inv = 1.0 / functools.reduce(lambda a, b: a + b, es)
        alphas = [e * inv for e in es]
        for h in range(n_heads):
            cols = slice(h * HEAD_DIM, (h + 1) * HEAD_DIM)
            mixed = None
            for g in range(n_groups):
                term = alphas[g][:, h:h + 1] * o_refs[g][rows, cols].astype(F32)
                mixed = term if mixed is None else mixed + term
            lhs_scr[rows, cols] = mixed.astype(BF16)
        y = jnp.dot(lhs_scr[rows, :], w, preferred_element_type=F32)
        out_ref[rows, :] = x_ref[rows, :] + gate * y


def _mix_out(o_list, lse_list, w, layer, x2, seq, gate):
    m, width = o_list[0].shape
    n = w.shape[-1]
    n_groups = len(o_list)
    tm = _pick_tile(seq, (512, 256, 128))
    tpb = seq // tm
    return pl.pallas_call(
        functools.partial(_mix_out_kernel, n_groups=n_groups, n_heads=width // HEAD_DIM,
                          row_chunk=128),
        grid=(m // tm,),
        in_specs=(
            [pl.BlockSpec((tm, width), lambda i: (i, 0))] * n_groups
            + [pl.BlockSpec((tm, LANES), lambda i: (i, 0))] * n_groups
            + [
                pl.BlockSpec((None, width, n), lambda i: (layer, 0, 0)),
                pl.BlockSpec((tm, n), lambda i: (i, 0)),
                pl.BlockSpec((None, 1, n), lambda i: (i // tpb, 0, 0)),
            ]
        ),
        out_specs=pl.BlockSpec((tm, n), lambda i: (i, 0)),
        out_shape=jax.ShapeDtypeStruct((m, n), F32),
        scratch_shapes=[pltpu.VMEM((tm, width), BF16)],
        compiler_params=_compiler_params(("parallel",)),
        name="mix_out",
    )(*o_list, *lse_list, w, x2, gate)


def _fox_kernel(q_ref, k_ref, v_ref, fq_ref, fk_ref, o_ref, *, blk, diag_rows, n_heads,
                heads_per_step):
    head0 = pl.program_id(1) * heads_per_step
    qt = pl.program_id(2)
    lane = lax.broadcasted_iota(jnp.int32, (1, LANES), 1)
    fq_all = fq_ref[...]
    contract_last = (((1,), (1,)), ((), ()))

    qs = []
    for hh in range(heads_per_step):
        off = (lane % (3 * n_heads)) - 3 * (head0 + hh)
        keep = jnp.where(off >= 0, jnp.where(off < 3, 1.0, 0.0), 0.0)
        keep = jnp.where(lane < 6 * n_heads, keep, 0.0).astype(BF16)
        qs.append(jnp.concatenate([q_ref[:, hh * HEAD_DIM:(hh + 1) * HEAD_DIM],
                                   fq_all * keep], axis=1))

    def update(kb, items, diagonal):
        scores = []
        for hh, row0, n_rows, n_keys, _ in items:
            rows = pl.ds(pl.multiple_of(kb * blk, blk), n_keys)
            cols = slice(hh * HEAD_DIM, (hh + 1) * HEAD_DIM)
            k = jnp.concatenate([k_ref[rows, cols], fk_ref[rows, :]], axis=1)
            s2 = lax.dot_general(qs[hh][row0:row0 + n_rows, :], k, contract_last,
                                 preferred_element_type=F32)
            if diagonal:
                qi = lax.broadcasted_iota(jnp.int32, (n_rows, n_keys), 0) + row0
                kj = lax.broadcasted_iota(jnp.int32, (n_rows, n_keys), 1)
                s2 = jnp.where(kj <= qi, s2, NEG_INF)
            scores.append((s2, rows, cols))
        probs = []
        for (s2, _, _), (_, _, _, _, (m_run, l_run, acc)) in zip(scores, items):
            m_new = jnp.maximum(m_run, jnp.max(s2, axis=-1, keepdims=True))
            alpha = jnp.exp2(m_run - m_new)
            p = jnp.exp2(s2 - m_new)
            l_new = alpha * l_run + jnp.sum(p, axis=-1, keepdims=True)
            probs.append((p.astype(BF16), m_new, l_new, alpha * acc))
        out = []
        for (p, m_new, l_new, acc_scaled), (_, rows, cols) in zip(probs, scores):
            acc_new = acc_scaled + jnp.dot(p, v_ref[rows, cols], preferred_element_type=F32)
            out.append((m_new, l_new, acc_new))
        return tuple(out)

    def below_diagonal(kb, carry):
        return update(kb, [(hh, 0, blk, blk, carry[hh]) for hh in range(heads_per_step)], False)

    init = tuple((jnp.full((blk, 1), NEG_INF, F32), jnp.zeros((blk, 1), F32),
                  jnp.zeros((blk, HEAD_DIM), F32)) for _ in range(heads_per_step))
    carry = lax.fori_loop(0, qt, below_diagonal, init)

    items = []
    for hh in range(heads_per_step):
        for row0 in range(0, blk, diag_rows):
            piece_carry = tuple(c[row0:row0 + diag_rows] for c in carry[hh])
            items.append((hh, row0, diag_rows, row0 + diag_rows, piece_carry))
    for (hh, row0, _, _, _), (_, l_fin, acc) in zip(items, update(qt, items, True)):
        o_ref[row0:row0 + diag_rows, hh * HEAD_DIM:(hh + 1) * HEAD_DIM] = (
            acc / l_fin).astype(o_ref.dtype)


def _fox_attention(q, kv, f_query, f_keys, bsz, seq, n_heads):
    blk = _pick_tile(seq, (FOX_BLOCK, 256, 128))
    nkb = seq // blk
    width = n_heads * HEAD_DIM
    hps = FOX_HEADS_PER_STEP
    assert n_heads % hps == 0
    n_hg = n_heads // hps
    qv = q.reshape(bsz, seq, width)
    kvv = kv.reshape(bsz, seq, 2 * width)
    o = pl.pallas_call(
        functools.partial(_fox_kernel, blk=blk, diag_rows=min(blk, FOX_DIAG_ROWS), n_heads=n_heads,
                          heads_per_step=hps),
        grid=(bsz, n_hg, nkb),
        in_specs=[
            pl.BlockSpec((None, blk, hps * HEAD_DIM), lambda b, h, t: (b, t, h)),
            pl.BlockSpec((None, seq, hps * HEAD_DIM), lambda b, h, t: (b, 0, h)),
            pl.BlockSpec((None, seq, hps * HEAD_DIM), lambda b, h, t: (b, 0, n_hg + h)),
            pl.BlockSpec((None, blk, LANES), lambda b, h, t: (b, t, 0)),
            pl.BlockSpec((None, seq, LANES), lambda b, h, t: (b, 0, 0)),
        ],
        out_specs=pl.BlockSpec((None, blk, hps * HEAD_DIM), lambda b, h, t: (b, t, h)),
        out_shape=jax.ShapeDtypeStruct((bsz, seq, width), BF16),
        compiler_params=_compiler_params(("parallel", "parallel", "arbitrary")),
        name="fox_attention",
    )(qv, kvv, kvv, f_query.reshape(bsz, seq, LANES), f_keys.reshape(bsz, seq, LANES))
    return o.reshape(bsz * seq, width)


def _rotary_lane_order():
    half = ROT_DIM // 2
    mid = HEAD_DIM // 2
    return np.concatenate([np.arange(0, half), np.arange(ROT_DIM, mid + half),
                           np.arange(half, ROT_DIM), np.arange(mid + half, HEAD_DIM)])


def _rope_tables(seq, dilations):
    half = ROT_DIM // 2
    mid = HEAD_DIM // 2
    inv = ROPE_THETA ** (-jnp.arange(0, ROT_DIM, 2, dtype=F32) / ROT_DIM)
    ang = jnp.arange(seq, dtype=F32)[:, None] * inv[None, :]
    cos, sin = jnp.cos(ang), jnp.sin(ang)
    ones = jnp.ones((seq, mid - half), F32)
    zeros = jnp.zeros((seq, mid - half), F32)
    cos_t = jnp.concatenate([cos, ones, cos, ones], axis=-1)
    sin_t = jnp.concatenate([-sin, zeros, sin, zeros], axis=-1)

    def deinterleave(t, d):
        t = t.reshape(seq // DEINT_ROWS, DEINT_ROWS // d, d, HEAD_DIM)
        return jnp.swapaxes(t, 1, 2).reshape(seq, HEAD_DIM)

    return (jnp.stack([deinterleave(cos_t, d) for d in dilations]),
            jnp.stack([deinterleave(sin_t, d) for d in dilations]))


def _split_mods(mods, parts):
    bsz, n = mods.shape
    d = n // parts
    return [mods[:, p * d:(p + 1) * d].reshape(bsz, 1, d) for p in range(parts)]


def kernel(x, c, w_ada, b_ada, g_norm_attn, g_norm_ffn, w_qkv_a, g_qk_a, w_o_a, w_ada_kv, b_ada_kv, g_norm_kv, w_kv, g_k_b, w_f, b_f, w_q_b, g_q_b, w_o_b, w_ffn_in, w_ffn_out):
    bsz, seq, d = x.shape
    depth = w_ada.shape[0]
    n_a = w_qkv_a.shape[0]
    n_groups = g_qk_a.shape[2]
    width_a = w_o_a.shape[1]
    n_heads_b = w_f.shape[1]
    assert n_groups == len(DIL_CONFIGS) and n_heads_b <= LANES and width_a == PROJ_COLS
    dilations = tuple(dl for _, dl in DIL_CONFIGS)

    x2 = x.reshape(bsz * seq, d)
    rope_tables = _rope_tables(seq, dilations)

    w_qkv = _qkv_weights(w_qkv_a, n_groups)
    g_qk = g_qk_a[..., _rotary_lane_order()]
    w_o_a = w_o_a.astype(BF16)
    w_kv = w_kv.astype(BF16)[None]
    w_q_b = w_q_b.astype(BF16)
    w_o_b = w_o_b.astype(BF16)
    w_ffn_in = w_ffn_in.astype(BF16)
    w_ffn_out = w_ffn_out.astype(BF16)
    wf_pad = jnp.pad(w_f, ((0, 0), (0, LANES - n_heads_b))).astype(BF16)
    bf_pad = jnp.pad(b_f, (0, LANES - n_heads_b)).reshape(1, LANES)

    kv = f_keys = f_query = None
    for layer in range(depth):
        sh_a, sc_a, gt_a, sh_f, sc_f, gt_f = _split_mods(_mods(c, w_ada, b_ada, layer), 6)
        if layer < n_a:
            gains = g_qk[layer].reshape(2 * n_groups, 1, HEAD_DIM)
            h_var = _deint_lhs(x2, seq, g_norm_attn[layer], sh_a, sc_a, dilations)
            qkv = _qkv_proj(h_var, seq, w_qkv, layer, gains, rope_tables)
            outs = [_dilated_group(qkv, bsz, seq, g, n_groups, width_a, window, dilation)
                    for g, (window, dilation) in enumerate(DIL_CONFIGS)]
            x2 = _mix_out([o for o, _ in outs], [l for _, l in outs], w_o_a, layer, x2, seq, gt_a)
        else:
            i = layer - n_a
            q = _proj(x2, seq, g_norm_attn[layer], sh_a, sc_a, w_q_b, i,
                      g_q_b[i].reshape(1, HEAD_DIM),
                      n_norm_tiles=w_q_b.shape[-1] // PROJ_COLS,
                      out_scale=HEAD_DIM ** -0.5 * LOG2E)
            o = _fox_attention(q, kv, f_query, f_keys, bsz, seq, n_heads_b)
            x2 = _mm_residual(o, w_o_b, i, x2, seq, gt_a)
        a = _swiglu_in(x2, seq, g_norm_ffn[layer], sh_f, sc_f, w_ffn_in, layer)
        x2 = _mm_residual(a, w_ffn_out, layer, x2, seq, gt_f)
        if layer == n_a - 1:
            sh_kv, sc_kv = _split_mods(_mods(c, w_ada_kv[None], b_ada_kv[None], 0), 2)
            kv, f_keys, f_query = _proj(
                x2, seq, g_norm_kv, sh_kv, sc_kv, w_kv, 0, g_k_b.reshape(1, HEAD_DIM),
                n_norm_tiles=w_kv.shape[-1] // 2 // PROJ_COLS, forget=(wf_pad, bf_pad, n_heads_b))
    return x2.reshape(bsz, seq, d)
```

```python
import functools
import math

import numpy as np
import jax
import jax.numpy as jnp
from jax import lax
from jax.experimental import pallas as pl
from jax.experimental.pallas import tpu as pltpu

HEAD_DIM = 128
DIL_CONFIGS = ((128, 1), (512, 4), (2048, 16))
ROT_DIM = HEAD_DIM // 4
ROPE_THETA = 500000.0
ATT_BLOCK = 128
DILATED_UNITS = 4
EPS = 1e-6
NEG_INF = -1e30
LOG2E = math.log2(math.e)
LN2 = math.log(2.0)

LANES = 128
BF16_SUBLANES = 16
VMEM_LIMIT_BYTES = 56 * 1024 * 1024

PROJ_ROWS = 1024
PROJ_COLS = 1024
RESIDUAL_ROWS = 512
FILL_ROWS = 128
CUMSUM_ROWS = 256
DEINT_ROWS = 512
PIECE_ROWS = 256
FOX_BLOCK = 512
FOX_DIAG_ROWS = 512
FOX_HEADS_PER_STEP = 8

F32 = jnp.float32
BF16 = jnp.bfloat16


def _compiler_params(semantics):
    return pltpu.CompilerParams(dimension_semantics=semantics, vmem_limit_bytes=VMEM_LIMIT_BYTES)


def _pick_tile(n, candidates):
    for t in candidates:
        if n % t == 0:
            return t
    raise ValueError(f"no tile in {candidates} divides {n}")


def _head_rmsnorm(a, gain):
    return a * lax.rsqrt(jnp.mean(a * a, axis=-1, keepdims=True) + EPS) * gain


def _silu(v):
    return v * jax.nn.sigmoid(v)


def _mods_kernel(c_ref, w_ref, b_ref, o_ref):
    c_act = _silu(c_ref[...])
    o_ref[...] = jnp.dot(c_act.astype(BF16), w_ref[...].astype(BF16),
                         preferred_element_type=F32) + b_ref[...]


def _mods(c, w, b, layer):
    bsz, d = c.shape
    n = w.shape[-1]
    tn = _pick_tile(n, (512, 256, 128))
    return pl.pallas_call(
        _mods_kernel,
        grid=(n // tn,),
        in_specs=[
            pl.BlockSpec((bsz, d), lambda j: (0, 0)),
            pl.BlockSpec((None, d, tn), lambda j: (layer, 0, j)),
            pl.BlockSpec((None, 1, tn), lambda j: (layer, 0, j)),
        ],
        out_specs=pl.BlockSpec((bsz, tn), lambda j: (0, j)),
        out_shape=jax.ShapeDtypeStruct((bsz, n), F32),
        compiler_params=_compiler_params(("parallel",)),
        name="mods",
    )(c, w, b.reshape(b.shape[0], 1, n))


def _normmod_rows(x_ref, rows, gain, mul, shift):
    xf = x_ref[rows, :]
    y = xf * lax.rsqrt(jnp.mean(xf * xf, axis=-1, keepdims=True) + EPS) * gain
    return y * mul + shift


def _row_pieces(tm):
    return [slice(r, r + PIECE_ROWS) for r in range(0, tm, PIECE_ROWS)]


def _lhs_pieces(first, x_ref, gn_ref, sh_ref, sc_ref, h_scr):
    gain = gn_ref[...]
    mul = 1.0 + sc_ref[...]
    shift = sh_ref[...]
    for rows in _row_pieces(h_scr.shape[0]):
        if first:
            h_scr[rows, :] = _normmod_rows(x_ref, rows, gain, mul, shift).astype(BF16)
        yield rows, h_scr[rows, :]


def _deint_lhs_kernel(x_ref, gn_ref, sh_ref, sc_ref, o_ref, slab_scr, *, dilations, bases):
    tm, d_model = x_ref.shape
    n_slabs = d_model // LANES
    gain = gn_ref[...]
    mul = 1.0 + sc_ref[...]
    shift = sh_ref[...]

    def natural(it, carry):
        rows = pl.ds(pl.multiple_of(it * FILL_ROWS, FILL_ROWS), FILL_ROWS)
        ssq = None
        for s in range(n_slabs):
            xs = x_ref[rows, s * LANES:(s + 1) * LANES]
            ssq = xs * xs if ssq is None else ssq + xs * xs
        inv = lax.rsqrt(jnp.sum(ssq, axis=-1, keepdims=True) * (1.0 / d_model) + EPS)
        inv = jnp.broadcast_to(inv, (FILL_ROWS, LANES))
        for s in range(n_slabs):
            cols = slice(s * LANES, (s + 1) * LANES)
            h = x_ref[rows, cols] * inv * gain[:, cols] * mul[:, cols] + shift[:, cols]
            slab_scr[0, s, rows, :] = h
            for v, d in enumerate(dilations):
                if d == 1:
                    o_ref[v, rows, cols] = h.astype(BF16)
        return carry

    lax.fori_loop(0, tm // FILL_ROWS, natural, 0)

    for v, d in enumerate(dilations):
        if d == 1:
            continue
        p = max(b for b in bases if d % b == 0 and b < d)
        f = d // p
        src_slabs = slab_scr.at[bases.index(p)]
        dst_slabs = slab_scr.at[bases.index(d)] if d in bases else None
        per_residue = tm // d
        units_per_trip = max(1, FILL_ROWS // per_residue)

        def gather(it, carry, v=v, d=d, p=p, f=f, per_residue=per_residue,
                   units_per_trip=units_per_trip, src_slabs=src_slabs, dst_slabs=dst_slabs):
            for u in range(units_per_trip):
                r = it * units_per_trip + u
                k = r // p
                rp = r % p
                src = pl.ds(rp * (tm // p) + k, per_residue, stride=f)
                dst = pl.ds(pl.multiple_of(r * per_residue, per_residue), per_residue)
                for s in range(n_slabs):
                    piece = src_slabs[s, src, :]
                    o_ref[v, dst, s * LANES:(s + 1) * LANES] = piece.astype(BF16)
                    if dst_slabs is not None:
                        dst_slabs[s, dst, :] = piece
            return carry

        lax.fori_loop(0, d // units_per_trip, gather, 0)


def _deint_lhs(x2, seq, gnorm, shift, scale, dilations):
    m, d = x2.shape
    tm = DEINT_ROWS
    assert seq % tm == 0 and all(tm % (dl * BF16_SUBLANES) == 0 for dl in dilations)
    tpb = seq // tm
    n_var = len(dilations)
    assert list(dilations) == sorted(dilations) and dilations[0] == 1
    bases = [1]
    for dl in dilations[1:]:
        base = max(b for b in dilations if dl % b == 0 and b < dl)
        if base not in bases:
            bases.append(base)
    bases = tuple(bases)
    return pl.pallas_call(
        functools.partial(_deint_lhs_kernel, dilations=dilations, bases=bases),
        grid=(m // tm,),
        in_specs=[
            pl.BlockSpec((tm, d), lambda i: (i, 0)),
            pl.BlockSpec((1, d), lambda i: (0, 0)),
            pl.BlockSpec((None, 1, d), lambda i: (i // tpb, 0, 0)),
            pl.BlockSpec((None, 1, d), lambda i: (i // tpb, 0, 0)),
        ],
        out_specs=pl.BlockSpec((n_var, tm, d), lambda i: (0, i, 0)),
        out_shape=jax.ShapeDtypeStruct((n_var, m, d), BF16),
        scratch_shapes=[pltpu.VMEM((len(bases), d // LANES, tm, LANES), F32)],
        compiler_params=_compiler_params(("parallel",)),
        name="deint_lhs",
    )(x2, gnorm.reshape(1, d), shift, scale)


def _dot_head_norm(pieces, w_ref, o_ref, acc_scr, gain, cos_ref=None, sin_ref=None):
    tn = acc_scr.shape[1]
    w = w_ref[...]
    done = []
    for rows, lhs in pieces:
        acc_scr[rows, :] = jnp.dot(lhs, w, preferred_element_type=F32)
        done.append(rows)
    for rows in done:
        if cos_ref is not None:
            cos = cos_ref[rows, :]
            sin = sin_ref[rows, :]
        for h in range(tn // HEAD_DIM):
            cols = slice(h * HEAD_DIM, (h + 1) * HEAD_DIM)
            y = _head_rmsnorm(acc_scr[rows, cols], gain)
            if cos_ref is not None:
                y = y * cos + pltpu.roll(y, HEAD_DIM // 2, 1) * sin
            o_ref[rows, cols] = y.astype(o_ref.dtype)


def _qkv_weight_kernel(w_ref, perm_ref, o_ref, *, n_perm_tiles):
    w = w_ref[...].astype(BF16)

    @pl.when(pl.program_id(1) < n_perm_tiles)
    def _():
        for h in range(w.shape[1] // HEAD_DIM):
            cols = slice(h * HEAD_DIM, (h + 1) * HEAD_DIM)
            o_ref[:, cols] = jnp.dot(w[:, cols], perm_ref[...],
                                     preferred_element_type=F32).astype(BF16)

    @pl.when(pl.program_id(1) >= n_perm_tiles)
    def _():
        o_ref[...] = w


def _qkv_weights(w, n_groups):
    n_layers, d, n = w.shape
    tn = PROJ_COLS
    assert n == 3 * n_groups * tn
    order = _rotary_lane_order()
    perm = np.zeros((HEAD_DIM, HEAD_DIM), np.float32)
    perm[order, np.arange(HEAD_DIM)] = 1.0
    return pl.pallas_call(
        functools.partial(_qkv_weight_kernel, n_perm_tiles=2 * n_groups),
        grid=(n_layers, n // tn),
        in_specs=[
            pl.BlockSpec((None, d, tn), lambda l, j: (l, 0, j)),
            pl.BlockSpec((HEAD_DIM, HEAD_DIM), lambda l, j: (0, 0)),
        ],
        out_specs=pl.BlockSpec((None, d, tn), lambda l, j: (l, 0, j)),
        out_shape=jax.ShapeDtypeStruct(w.shape, BF16),
        compiler_params=_compiler_params(("parallel", "parallel")),
        name="qkv_weights",
    )(w, jnp.asarray(perm, BF16))


def _qkv_kernel(h_ref, w_ref, gain_ref, cos_ref, sin_ref, o_ref, acc_scr, *, n_parts):
    part = pl.program_id(1) % n_parts

    @pl.when(part < 2)
    def _():
        gain = gain_ref[...] * jnp.where(part == 0, HEAD_DIM ** -0.5 * LOG2E, 1.0)
        pieces = [(rows, h_ref[rows, :]) for rows in _row_pieces(h_ref.shape[0])]
        _dot_head_norm(pieces, w_ref, o_ref, acc_scr, gain, cos_ref, sin_ref)

    @pl.when(part >= 2)
    def _():
        o_ref[...] = jnp.dot(h_ref[...], w_ref[...],
                             preferred_element_type=F32).astype(o_ref.dtype)


def _qkv_proj(h_var, seq, w, layer, gains, rope_tables):
    n_groups, m, d = h_var.shape
    n = w.shape[-1]
    n_parts = 3
    tm = PROJ_ROWS
    tn = PROJ_COLS
    assert seq % tm == 0 and n == n_parts * n_groups * tn
    tpb = seq // tm

    def col(j):
        return (j % n_parts) * n_groups + j // n_parts

    return pl.pallas_call(
        functools.partial(_qkv_kernel, n_parts=n_parts),
        grid=(m // tm, n_parts * n_groups),
        in_specs=[
            pl.BlockSpec((None, tm, d), lambda i, j: (j // n_parts, i, 0)),
            pl.BlockSpec((None, d, tn), lambda i, j: (layer, 0, col(j))),
            pl.BlockSpec((None, 1, HEAD_DIM),
                         lambda i, j: (jnp.minimum(col(j), 2 * n_groups - 1), 0, 0)),
            pl.BlockSpec((None, tm, HEAD_DIM), lambda i, j: (j // n_parts, i % tpb, 0)),
            pl.BlockSpec((None, tm, HEAD_DIM), lambda i, j: (j // n_parts, i % tpb, 0)),
        ],
        out_specs=pl.BlockSpec((tm, tn), lambda i, j: (i, col(j))),
        out_shape=jax.ShapeDtypeStruct((m, n), BF16),
        scratch_shapes=[pltpu.VMEM((tm, tn), F32)],
        compiler_params=_compiler_params(("parallel", "arbitrary")),
        name="qkv_proj",
    )(h_var, w, gains, *rope_tables)


def _split3_bf16(v):
    hi = v.astype(BF16)
    rem = v - hi.astype(F32)
    mid = rem.astype(BF16)
    lo = (rem - mid.astype(F32)).astype(BF16)
    return hi, mid, lo


def _proj_kernel(*refs, n_norm_tiles, forget, tiles_per_batch, out_scale):
    x_ref, gn_ref, sh_ref, sc_ref, w_ref, gain_ref = refs[:6]
    pos = 6
    if forget:
        wf_ref, bf_ref, tri_ref, place_ref, ones_ref = refs[pos:pos + 5]
        pos += 5
    o_ref = refs[pos]
    pos += 1
    if forget:
        fk_ref, fq_ref = refs[pos:pos + 2]
        pos += 2
    h_scr, acc_scr = refs[pos:pos + 2]
    pos += 2
    if forget:
        carry_scr = refs[pos]

    i = pl.program_id(0)
    j = pl.program_id(1)
    tm = h_scr.shape[0]
    gain = gain_ref[...] * out_scale

    @pl.when(j == 0)
    def _():
        pieces = _lhs_pieces(True, x_ref, gn_ref, sh_ref, sc_ref, h_scr)
        _dot_head_norm(pieces, w_ref, o_ref, acc_scr, gain)
        if forget:
            z = jnp.dot(h_scr[...], wf_ref[...], preferred_element_type=F32) + bf_ref[...]
            log_f = jnp.minimum(z, 0.0) - jnp.log1p(jnp.exp(-jnp.abs(z)))
            @pl.when(i % tiles_per_batch == 0)
            def _():
                carry_scr[...] = jnp.zeros_like(carry_scr)

            tri = tri_ref[...]
            parts = _split3_bf16(log_f)
            running = carry_scr[...]
            blocks = []
            for r0 in range(0, tm, CUMSUM_ROWS):
                blk_cum = running
                for part in parts:
                    blk_cum = blk_cum + jnp.dot(tri, part[r0:r0 + CUMSUM_ROWS, :],
                                                preferred_element_type=F32)
                running = blk_cum[CUMSUM_ROWS - 1:CUMSUM_ROWS, :]
                blocks.append(blk_cum)
            carry_scr[...] = running
            cum = jnp.concatenate(blocks, axis=0)
            bias = ones_ref[...]
            for p, part in enumerate(_split3_bf16(cum * (-LOG2E))):
                bias = bias + jnp.dot(part, place_ref[p], preferred_element_type=F32)
            fk_ref[...] = bias[:, :LANES].astype(BF16)
            fq_ref[...] = bias[:, LANES:].astype(BF16)

    @pl.when((j > 0) & (j < n_norm_tiles))
    def _():
        pieces = _lhs_pieces(False, x_ref, gn_ref, sh_ref, sc_ref, h_scr)
        _dot_head_norm(pieces, w_ref, o_ref, acc_scr, gain)

    @pl.when(j >= n_norm_tiles)
    def _():
        o_ref[...] = jnp.dot(h_scr[...], w_ref[...],
                             preferred_element_type=F32).astype(o_ref.dtype)


def _fox_bias_lanes(n_heads):
    assert 6 * n_heads <= LANES
    place = np.zeros((3, LANES, 2 * LANES), np.float32)
    ones = np.zeros((1, 2 * LANES), np.float32)
    for h in range(n_heads):
        for p in range(3):
            place[p, h, 3 * h + p] = 1.0
            place[p, h, LANES + 3 * (n_heads + h) + p] = -1.0
    ones[0, 3 * n_heads:6 * n_heads] = 1.0
    ones[0, LANES:LANES + 3 * n_heads] = 1.0
    return jnp.asarray(place, BF16), jnp.asarray(ones, F32)


def _proj(x2, seq, gnorm, shift, scale, w, layer, gain, n_norm_tiles, forget=None,
          out_scale=1.0):
    m, d = x2.shape
    n = w.shape[-1]
    tm = PROJ_ROWS
    tn = PROJ_COLS
    assert seq % tm == 0 and n % tn == 0
    tpb = seq // tm
    with_forget = forget is not None

    in_specs = [
        pl.BlockSpec((tm, d), lambda i, j: (i, 0)),
        pl.BlockSpec((1, d), lambda i, j: (0, 0)),
        pl.BlockSpec((None, 1, d), lambda i, j: (i // tpb, 0, 0)),
        pl.BlockSpec((None, 1, d), lambda i, j: (i // tpb, 0, 0)),
        pl.BlockSpec((None, d, tn), lambda i, j: (layer, 0, j)),
        pl.BlockSpec((1, HEAD_DIM), lambda i, j: (0, 0)),
    ]
    args = [x2, gnorm.reshape(1, d), shift, scale, w, gain]
    out_specs = [pl.BlockSpec((tm, tn), lambda i, j: (i, j))]
    out_shape = [jax.ShapeDtypeStruct((m, n), BF16)]
    scratch = [pltpu.VMEM((tm, d), BF16), pltpu.VMEM((tm, tn), F32)]
    if with_forget:
        wf_pad, bf_pad, n_heads = forget
        tri = (lax.broadcasted_iota(jnp.int32, (CUMSUM_ROWS, CUMSUM_ROWS), 0)
               >= lax.broadcasted_iota(jnp.int32, (CUMSUM_ROWS, CUMSUM_ROWS), 1)).astype(BF16)
        place, ones = _fox_bias_lanes(n_heads)
        in_specs += [
            pl.BlockSpec((d, LANES), lambda i, j: (0, 0)),
            pl.BlockSpec((1, LANES), lambda i, j: (0, 0)),
            pl.BlockSpec((CUMSUM_ROWS, CUMSUM_ROWS), lambda i, j: (0, 0)),
            pl.BlockSpec((3, LANES, 2 * LANES), lambda i, j: (0, 0, 0)),
            pl.BlockSpec((1, 2 * LANES), lambda i, j: (0, 0)),
        ]
        args += [wf_pad, bf_pad, tri, place, ones]
        out_specs += [
            pl.BlockSpec((tm, LANES), lambda i, j: (i, 0)),
            pl.BlockSpec((tm, LANES), lambda i, j: (i, 0)),
        ]
        out_shape += [
            jax.ShapeDtypeStruct((m, LANES), BF16),
            jax.ShapeDtypeStruct((m, LANES), BF16),
        ]
        scratch.append(pltpu.VMEM((1, LANES), F32))

    kern = functools.partial(_proj_kernel, n_norm_tiles=n_norm_tiles, forget=with_forget,
                             tiles_per_batch=tpb, out_scale=out_scale)
    outs = pl.pallas_call(
        kern,
        grid=(m // tm, n // tn),
        in_specs=in_specs,
        out_specs=out_specs,
        out_shape=out_shape,
        scratch_shapes=scratch,
        compiler_params=_compiler_params(("arbitrary", "arbitrary")),
        name="proj",
    )(*args)
    return outs if with_forget else outs[0]


def _swiglu_kernel(x_ref, gn_ref, sh_ref, sc_ref, wg_ref, wu_ref, o_ref, h_scr):
    def gated(h):
        g = jnp.dot(h, wg_ref[...], preferred_element_type=F32)
        u = jnp.dot(h, wu_ref[...], preferred_element_type=F32)
        return (_silu(g) * u).astype(o_ref.dtype)

    @pl.when(pl.program_id(1) == 0)
    def _():
        for rows, h in _lhs_pieces(True, x_ref, gn_ref, sh_ref, sc_ref, h_scr):
            o_ref[rows, :] = gated(h)

    @pl.when(pl.program_id(1) > 0)
    def _():
        o_ref[...] = gated(h_scr[...])


def _swiglu_in(x2, seq, gnorm, shift, scale, w, layer):
    m, d = x2.shape
    f = w.shape[-1] // 2
    tm = PROJ_ROWS
    tf = _pick_tile(f, (512, 256, 128))
    assert seq % tm == 0
    tpb = seq // tm
    nf = f // tf
    return pl.pallas_call(
        _swiglu_kernel,
        grid=(m // tm, nf),
        in_specs=[
            pl.BlockSpec((tm, d), lambda i, j: (i, 0)),
            pl.BlockSpec((1, d), lambda i, j: (0, 0)),
            pl.BlockSpec((None, 1, d), lambda i, j: (i // tpb, 0, 0)),
            pl.BlockSpec((None, 1, d), lambda i, j: (i // tpb, 0, 0)),
            pl.BlockSpec((None, d, tf), lambda i, j: (layer, 0, j)),
            pl.BlockSpec((None, d, tf), lambda i, j: (layer, 0, nf + j)),
        ],
        out_specs=pl.BlockSpec((tm, tf), lambda i, j: (i, j)),
        out_shape=jax.ShapeDtypeStruct((m, f), BF16),
        scratch_shapes=[pltpu.VMEM((tm, d), BF16)],
        compiler_params=_compiler_params(("parallel", "arbitrary")),
        name="swiglu_in",
    )(x2, gnorm.reshape(1, d), shift, scale, w, w)


def _mmres_kernel(a_ref, w_ref, x_ref, gt_ref, o_ref):
    y = jnp.dot(a_ref[...], w_ref[...], preferred_element_type=F32)
    o_ref[...] = x_ref[...] + gt_ref[...] * y


def _mm_residual(a, w, layer, x2, seq, gate):
    m, k = a.shape
    n = w.shape[-1]
    tm = RESIDUAL_ROWS
    assert seq % tm == 0
    tpb = seq // tm
    return pl.pallas_call(
        _mmres_kernel,
        grid=(m // tm,),
        in_specs=[
            pl.BlockSpec((tm, k), lambda i: (i, 0)),
            pl.BlockSpec((None, k, n), lambda i: (layer, 0, 0), pipeline_mode=pl.Buffered(1)),
            pl.BlockSpec((tm, n), lambda i: (i, 0)),
            pl.BlockSpec((None, 1, n), lambda i: (i // tpb, 0, 0)),
        ],
        out_specs=pl.BlockSpec((tm, n), lambda i: (i, 0)),
        out_shape=jax.ShapeDtypeStruct((m, n), F32),
        compiler_params=_compiler_params(("parallel",)),
        name="mm_residual",
    )(a, w, x2, gate)


def _dilated_kernel(*refs, n_back, n_heads, dilation, has_prev, units):
    ins = refs[:3 * units]
    o_ref, lse_ref, o_scr = refs[3 * units:3 * units + 3]
    if has_prev:
        kp_scr, vp_scr = refs[3 * units + 3:]
    step_blk = pl.program_id(1)
    step_res = pl.program_id(2)
    width = n_heads * HEAD_DIM
    n_keys = (2 if has_prev else 1) * ATT_BLOCK
    qi = lax.broadcasted_iota(jnp.int32, (ATT_BLOCK, n_keys), 0)
    kj = lax.broadcasted_iota(jnp.int32, (ATT_BLOCK, n_keys), 1)
    dist = qi + (n_keys - ATT_BLOCK) - kj
    band = (dist >= 0) & (dist <= n_back)
    lane = lax.broadcasted_iota(jnp.int32, (ATT_BLOCK, LANES), 1)
    contract_last = (((1,), (1,)), ((), ()))
    head_cols = [slice(h * HEAD_DIM, (h + 1) * HEAD_DIM) for h in range(n_heads)]

    blocks = []
    for u in range(units):
        q_ref, kc_ref, vc_ref = ins[3 * u:3 * u + 3]
        if dilation == 1:
            blk, res = step_blk * units + u, 0
            rows = slice(u * ATT_BLOCK, (u + 1) * ATT_BLOCK)
        else:
            blk, res = step_blk, step_res * units + u
            rows = pl.ds(res, ATT_BLOCK, stride=dilation)
        q = q_ref[...].reshape(ATT_BLOCK, width)
        k = kc_ref[...].reshape(ATT_BLOCK, width)
        v = vc_ref[...].reshape(ATT_BLOCK, width)
        mask = band
        if has_prev:
            kp_ref = kp_scr.at[res]
            vp_ref = vp_scr.at[res]
            if dilation > 1 or u == 0:
                @pl.when(blk == 0)
                def _(kp_ref=kp_ref, vp_ref=vp_ref):
                    kp_ref[...] = jnp.zeros_like(kp_ref)
                    vp_ref[...] = jnp.zeros_like(vp_ref)
            k_prev = kp_ref[...]
            v_prev = vp_ref[...]
            kp_ref[...] = k
            vp_ref[...] = v
            k = jnp.concatenate([k_prev, k], axis=0)
            v = jnp.concatenate([v_prev, v], axis=0)
            mask = band & ((kj >= ATT_BLOCK) | (blk > 0))
        blocks.append((q, k, v, mask, rows))

    scores = [[lax.dot_general(q[:, cols], k[:, cols], contract_last,
                               preferred_element_type=F32) for cols in head_cols]
              for q, k, _, _, _ in blocks]
    probs = []
    for (_, _, _, mask, rows), unit_scores in zip(blocks, scores):
        unit_probs = []
        lse_tile = jnp.zeros((ATT_BLOCK, LANES), F32)
        for h, s2 in enumerate(unit_scores):
            s2 = jnp.where(mask, s2, NEG_INF)
            mx2 = jnp.max(s2, axis=-1, keepdims=True)
            p = jnp.exp2(s2 - mx2)
            den = jnp.sum(p, axis=-1, keepdims=True)
            unit_probs.append((p.astype(BF16), den))
            lse_tile = jnp.where(lane == h, mx2 * LN2 + jnp.log(den), lse_tile)
        lse_ref[rows, :] = lse_tile
        probs.append(unit_probs)
    for (_, _, v, _, rows), unit_probs in zip(blocks, probs):
        for h, (p, den) in enumerate(unit_probs):
            o = jnp.dot(p, v[:, head_cols[h]], preferred_element_type=F32)
            o_scr[h, rows, :] = o / den

    @pl.when(step_res == pl.num_programs(2) - 1)
    def _():
        for h in range(n_heads):
            o_ref[:, h * HEAD_DIM:(h + 1) * HEAD_DIM] = o_scr[h].astype(o_ref.dtype)


def _dilated_group(qkv, bsz, seq, group, n_groups, width, window, dilation):
    n_back = window // dilation
    sub_len = seq // dilation
    assert sub_len % ATT_BLOCK == 0
    nb = sub_len // ATT_BLOCK
    n_cols = qkv.shape[1]
    tpb = seq // DEINT_ROWS
    per_residue = DEINT_ROWS // dilation

    if per_residue >= ATT_BLOCK:
        bpt = per_residue // ATT_BLOCK
        view = qkv.reshape(bsz, tpb, dilation, bpt, ATT_BLOCK, n_cols)
        block = (None, None, None, None, ATT_BLOCK, width)

        def rows_index(b, n, r):
            return (b, n // bpt, r, n % bpt, 0)
    else:
        pieces = ATT_BLOCK // per_residue
        view = qkv.reshape(bsz, tpb, dilation, per_residue, n_cols)
        block = (None, pieces, None, per_residue, width)

        def rows_index(b, n, r):
            return (b, n, r, 0)

    units = DILATED_UNITS if (nb if dilation == 1 else dilation) % DILATED_UNITS == 0 else 1

    def spec(part, u):
        def index(b, n, r):
            if dilation == 1:
                rows = rows_index(b, n * units + u, r)
            else:
                rows = rows_index(b, n, r * units + u)
            return rows + (part * n_groups + group,)
        return pl.BlockSpec(block, index)

    if dilation == 1:
        grid = (bsz, nb // units, 1)
        span = ATT_BLOCK * units
    else:
        grid = (bsz, nb, dilation // units)
        span = ATT_BLOCK * dilation
    has_prev = nb > 1
    in_specs = [spec(part, u) for u in range(units) for part in range(3)]
    scratch = [pltpu.VMEM((width // HEAD_DIM, span, HEAD_DIM), F32)]
    if has_prev:
        scratch += [pltpu.VMEM((dilation, ATT_BLOCK, width), BF16)] * 2
    o, lse = pl.pallas_call(
        functools.partial(_dilated_kernel, n_back=n_back, n_heads=width // HEAD_DIM,
                          dilation=dilation, has_prev=has_prev, units=units),
        grid=grid,
        in_specs=in_specs,
        out_specs=[
            pl.BlockSpec((None, span, width), lambda b, n, r: (b, n, 0)),
            pl.BlockSpec((None, span, LANES), lambda b, n, r: (b, n, 0)),
        ],
        out_shape=[
            jax.ShapeDtypeStruct((bsz, seq, width), BF16),
            jax.ShapeDtypeStruct((bsz, seq, LANES), F32),
        ],
        scratch_shapes=scratch,
        compiler_params=_compiler_params(("parallel", "arbitrary", "arbitrary")),
        name="dilated_attention",
    )(*([view] * len(in_specs)))
    return o.reshape(bsz * seq, width), lse.reshape(bsz * seq, LANES)


def _mix_out_kernel(*refs, n_groups, n_heads, row_chunk):
    o_refs = refs[:n_groups]
    lse_refs = refs[n_groups:2 * n_groups]
    w_ref, x_ref, gt_ref, out_ref, lhs_scr = refs[2 * n_groups:]
    tm = x_ref.shape[0]
    w = w_ref[...]
    gate = gt_ref[...]

    for r in range(0, tm, row_chunk):
        rows = slice(r, r + row_chunk)
        lses = [ref[rows, :] for ref in lse_refs]
        mx = functools.reduce(jnp.maximum, lses)
        es = [jnp.exp(l - mx) for l in lses]
        inv = 1.0 / functools.reduce(lambda a, b: a + b, es)
        alphas = [e * inv for e in es]
        for h in range(n_heads):
            cols = slice(h * HEAD_DIM, (h + 1) * HEAD_DIM)
            mixed = None
            for g in range(n_groups):
                term = alphas[g][:, h:h + 1] * o_refs[g][rows, cols].astype(F32)
                mixed = term if mixed is None else mixed + term
            lhs_scr[rows, cols] = mixed.astype(BF16)
        y = jnp.dot(lhs_scr[rows, :], w, preferred_element_type=F32)
        out_ref[rows, :] = x_ref[rows, :] + gate * y


def _mix_out(o_list, lse_list, w, layer, x2, seq, gate):
    m, width = o_list[0].shape
    n = w.shape[-1]
    n_groups = len(o_list)
    tm = _pick_tile(seq, (512, 256, 128))
    tpb = seq // tm
    return pl.pallas_call(
        functools.partial(_mix_out_kernel, n_groups=n_groups, n_heads=width // HEAD_DIM,
                          row_chunk=128),
        grid=(m // tm,),
        in_specs=(
            [pl.BlockSpec((tm, width), lambda i: (i, 0))] * n_groups
            + [pl.BlockSpec((tm, LANES), lambda i: (i, 0))] * n_groups
            + [
                pl.BlockSpec((None, width, n), lambda i: (layer, 0, 0)),
                pl.BlockSpec((tm, n), lambda i: (i, 0)),
                pl.BlockSpec((None, 1, n), lambda i: (i // tpb, 0, 0)),
            ]
        ),
        out_specs=pl.BlockSpec((tm, n), lambda i: (i, 0)),
        out_shape=jax.ShapeDtypeStruct((m, n), F32),
        scratch_shapes=[pltpu.VMEM((tm, width), BF16)],
        compiler_params=_compiler_params(("parallel",)),
        name="mix_out",
    )(*o_list, *lse_list, w, x2, gate)


def _fox_kernel(q_ref, k_ref, v_ref, fq_ref, fk_ref, o_ref, *, blk, diag_rows, n_heads,
                heads_per_step):
    head0 = pl.program_id(1) * heads_per_step
    qt = pl.program_id(2)
    lane = lax.broadcasted_iota(jnp.int32, (1, LANES), 1)
    fq_all = fq_ref[...]
    contract_last = (((1,), (1,)), ((), ()))

    qs = []
    for hh in range(heads_per_step):
        off = (lane % (3 * n_heads)) - 3 * (head0 + hh)
        keep = jnp.where(off >= 0, jnp.where(off < 3, 1.0, 0.0), 0.0)
        keep = jnp.where(lane < 6 * n_heads, keep, 0.0).astype(BF16)
        qs.append(jnp.concatenate([q_ref[:, hh * HEAD_DIM:(hh + 1) * HEAD_DIM],
                                   fq_all * keep], axis=1))

    def update(kb, items, diagonal):
        scores = []
        for hh, row0, n_rows, n_keys, _ in items:
            rows = pl.ds(pl.multiple_of(kb * blk, blk), n_keys)
            cols = slice(hh * HEAD_DIM, (hh + 1) * HEAD_DIM)
            k = jnp.concatenate([k_ref[rows, cols], fk_ref[rows, :]], axis=1)
            s2 = lax.dot_general(qs[hh][row0:row0 + n_rows, :], k, contract_last,
                                 preferred_element_type=F32)
            if diagonal:
                qi = lax.broadcasted_iota(jnp.int32, (n_rows, n_keys), 0) + row0
                kj = lax.broadcasted_iota(jnp.int32, (n_rows, n_keys), 1)
                s2 = jnp.where(kj <= qi, s2, NEG_INF)
            scores.append((s2, rows, cols))
        probs = []
        for (s2, _, _), (_, _, _, _, (m_run, l_run, acc)) in zip(scores, items):
            m_new = jnp.maximum(m_run, jnp.max(s2, axis=-1, keepdims=True))
            alpha = jnp.exp2(m_run - m_new)
            p = jnp.exp2(s2 - m_new)
            l_new = alpha * l_run + jnp.sum(p, axis=-1, keepdims=True)
            probs.append((p.astype(BF16), m_new, l_new, alpha * acc))
        out = []
        for (p, m_new, l_new, acc_scaled), (_, rows, cols) in zip(probs, scores):
            acc_new = acc_scaled + jnp.dot(p, v_ref[rows, cols], preferred_element_type=F32)
            out.append((m_new, l_new, acc_new))
        return tuple(out)

    def below_diagonal(kb, carry):
        return update(kb, [(hh, 0, blk, blk, carry[hh]) for hh in range(heads_per_step)], False)

    init = tuple((jnp.full((blk, 1), NEG_INF, F32), jnp.zeros((blk, 1), F32),
                  jnp.zeros((blk, HEAD_DIM), F32)) for _ in range(heads_per_step))
    carry = lax.fori_loop(0, qt, below_diagonal, init)

    items = []
    for hh in range(heads_per_step):
        for row0 in range(0, blk, diag_rows):
            piece_carry = tuple(c[row0:row0 + diag_rows] for c in carry[hh])
            items.append((hh, row0, diag_rows, row0 + diag_rows, piece_carry))
    for (hh, row0, _, _, _), (_, l_fin, acc) in zip(items, update(qt, items, True)):
        o_ref[row0:row0 + diag_rows, hh * HEAD_DIM:(hh + 1) * HEAD_DIM] = (
            acc / l_fin).astype(o_ref.dtype)


def _fox_attention(q, kv, f_query, f_keys, bsz, seq, n_heads):
    blk = _pick_tile(seq, (FOX_BLOCK, 256, 128))
    nkb = seq // blk
    width = n_heads * HEAD_DIM
    hps = FOX_HEADS_PER_STEP
    assert n_heads % hps == 0
    n_hg = n_heads // hps
    qv = q.reshape(bsz, seq, width)
    kvv = kv.reshape(bsz, seq, 2 * width)
    o = pl.pallas_call(
        functools.partial(_fox_kernel, blk=blk, diag_rows=min(blk, FOX_DIAG_ROWS), n_heads=n_heads,
                          heads_per_step=hps),
        grid=(bsz, n_hg, nkb),
        in_specs=[
            pl.BlockSpec((None, blk, hps * HEAD_DIM), lambda b, h, t: (b, t, h)),
            pl.BlockSpec((None, seq, hps * HEAD_DIM), lambda b, h, t: (b, 0, h)),
            pl.BlockSpec((None, seq, hps * HEAD_DIM), lambda b, h, t: (b, 0, n_hg + h)),
            pl.BlockSpec((None, blk, LANES), lambda b, h, t: (b, t, 0)),
            pl.BlockSpec((None, seq, LANES), lambda b, h, t: (b, 0, 0)),
        ],
        out_specs=pl.BlockSpec((None, blk, hps * HEAD_DIM), lambda b, h, t: (b, t, h)),
        out_shape=jax.ShapeDtypeStruct((bsz, seq, width), BF16),
        compiler_params=_compiler_params(("parallel", "parallel", "arbitrary")),
        name="fox_attention",
    )(qv, kvv, kvv, f_query.reshape(bsz, seq, LANES), f_keys.reshape(bsz, seq, LANES))
    return o.reshape(bsz * seq, width)


def _rotary_lane_order():
    half = ROT_DIM // 2
    mid = HEAD_DIM // 2
    return np.concatenate([np.arange(0, half), np.arange(ROT_DIM, mid + half),
                           np.arange(half, ROT_DIM), np.arange(mid + half, HEAD_DIM)])


def _rope_tables(seq, dilations):
    half = ROT_DIM // 2
    mid = HEAD_DIM // 2
    inv = ROPE_THETA ** (-jnp.arange(0, ROT_DIM, 2, dtype=F32) / ROT_DIM)
    ang = jnp.arange(seq, dtype=F32)[:, None] * inv[None, :]
    cos, sin = jnp.cos(ang), jnp.sin(ang)
    ones = jnp.ones((seq, mid - half), F32)
    zeros = jnp.zeros((seq, mid - half), F32)
    cos_t = jnp.concatenate([cos, ones, cos, ones], axis=-1)
    sin_t = jnp.concatenate([-sin, zeros, sin, zeros], axis=-1)

    def deinterleave(t, d):
        t = t.reshape(seq // DEINT_ROWS, DEINT_ROWS // d, d, HEAD_DIM)
        return jnp.swapaxes(t, 1, 2).reshape(seq, HEAD_DIM)

    return (jnp.stack([deinterleave(cos_t, d) for d in dilations]),
            jnp.stack([deinterleave(sin_t, d) for d in dilations]))


def _split_mods(mods, parts):
    bsz, n = mods.shape
    d = n // parts
    return [mods[:, p * d:(p + 1) * d].reshape(bsz, 1, d) for p in range(parts)]


def kernel(x, c, w_ada, b_ada, g_norm_attn, g_norm_ffn, w_qkv_a, g_qk_a, w_o_a, w_ada_kv, b_ada_kv, g_norm_kv, w_kv, g_k_b, w_f, b_f, w_q_b, g_q_b, w_o_b, w_ffn_in, w_ffn_out):
    bsz, seq, d = x.shape
    depth = w_ada.shape[0]
    n_a = w_qkv_a.shape[0]
    n_groups = g_qk_a.shape[2]
    width_a = w_o_a.shape[1]
    n_heads_b = w_f.shape[1]
    assert n_groups == len(DIL_CONFIGS) and n_heads_b <= LANES and width_a == PROJ_COLS
    dilations = tuple(dl for _, dl in DIL_CONFIGS)

    x2 = x.reshape(bsz * seq, d)
    rope_tables = _rope_tables(seq, dilations)

    w_qkv = _qkv_weights(w_qkv_a, n_groups)
    g_qk = g_qk_a[..., _rotary_lane_order()]
    w_o_a = w_o_a.astype(BF16)
    w_kv = w_kv.astype(BF16)[None]
    w_q_b = w_q_b.astype(BF16)
    w_o_b = w_o_b.astype(BF16)
    w_ffn_in = w_ffn_in.astype(BF16)
    w_ffn_out = w_ffn_out.astype(BF16)
    wf_pad = jnp.pad(w_f, ((0, 0), (0, LANES - n_heads_b))).astype(BF16)
    bf_pad = jnp.pad(b_f, (0, LANES - n_heads_b)).reshape(1, LANES)

    kv = f_keys = f_query = None
    for layer in range(depth):
        sh_a, sc_a, gt_a, sh_f, sc_f, gt_f = _split_mods(_mods(c, w_ada, b_ada, layer), 6)
        if layer < n_a:
            gains = g_qk[layer].reshape(2 * n_groups, 1, HEAD_DIM)
            h_var = _deint_lhs(x2, seq, g_norm_attn[layer], sh_a, sc_a, dilations)
            qkv = _qkv_proj(h_var, seq, w_qkv, layer, gains, rope_tables)
            outs = [_dilated_group(qkv, bsz, seq, g, n_groups, width_a, window, dilation)
                    for g, (window, dilation) in enumerate(DIL_CONFIGS)]
            x2 = _mix_out([o for o, _ in outs], [l for _, l in outs], w_o_a, layer, x2, seq, gt_a)
        else:
            i = layer - n_a
            q = _proj(x2, seq, g_norm_attn[layer], sh_a, sc_a, w_q_b, i,
                      g_q_b[i].reshape(1, HEAD_DIM),
                      n_norm_tiles=w_q_b.shape[-1] // PROJ_COLS,
                      out_scale=HEAD_DIM ** -0.5 * LOG2E)
            o = _fox_attention(q, kv, f_query, f_keys, bsz, seq, n_heads_b)
            x2 = _mm_residual(o, w_o_b, i, x2, seq, gt_a)
        a = _swiglu_in(x2, seq, g_norm_ffn[layer], sh_f, sc_f, w_ffn_in, layer)
        x2 = _mm_residual(a, w_ffn_out, layer, x2, seq, gt_f)
        if layer == n_a - 1:
            sh_kv, sc_kv = _split_mods(_mods(c, w_ada_kv[None], b_ada_kv[None], 0), 2)
            kv, f_keys, f_query = _proj(
                x2, seq, g_norm_kv, sh_kv, sc_kv, w_kv, 0, g_k_b.reshape(1, HEAD_DIM),
                n_norm_tiles=w_kv.shape[-1] // 2 // PROJ_COLS, forget=(wf_pad, bf_pad, n_heads_b))
    return x2.reshape(bsz, seq, d)
```

```python
import functools
import math

import numpy as np
import jax
import jax.numpy as jnp
from jax import lax
from jax.experimental import pallas as pl
from jax.experimental.pallas import tpu as pltpu

HEAD_DIM = 128
DIL_CONFIGS = ((128, 1), (512, 4), (2048, 16))
ROT_DIM = HEAD_DIM // 4
ROPE_THETA = 500000.0
ATT_BLOCK = 128
DILATED_UNITS = 4
EPS = 1e-6
NEG_INF = -1e30
LOG2E = math.log2(math.e)
LN2 = math.log(2.0)

LANES = 128
BF16_SUBLANES = 16
VMEM_LIMIT_BYTES = 56 * 1024 * 1024

PROJ_ROWS = 1024
QKV_ROWS = 2048
PROJ_COLS = 1024
RESIDUAL_ROWS = 512
FILL_ROWS = 128
CUMSUM_ROWS = 256
DEINT_ROWS = 512
PIECE_ROWS = 256
FOX_BLOCK = 512
FOX_DIAG_ROWS = 512
FOX_HEADS_PER_STEP = 8

F32 = jnp.float32
BF16 = jnp.bfloat16


def _compiler_params(semantics):
    return pltpu.CompilerParams(dimension_semantics=semantics, vmem_limit_bytes=VMEM_LIMIT_BYTES)


def _pick_tile(n, candidates):
    for t in candidates:
        if n % t == 0:
            return t
    raise ValueError(f"no tile in {candidates} divides {n}")


def _head_rmsnorm(a, gain):
    return a * lax.rsqrt(jnp.mean(a * a, axis=-1, keepdims=True) + EPS) * gain


def _silu(v):
    return v * jax.nn.sigmoid(v)


def _mods_kernel(c_ref, w_ref, b_ref, o_ref):
    c_act = _silu(c_ref[...])
    o_ref[...] = jnp.dot(c_act.astype(BF16), w_ref[...].astype(BF16),
                         preferred_element_type=F32) + b_ref[...]


def _mods(c, w, b, layer):
    bsz, d = c.shape
    n = w.shape[-1]
    tn = _pick_tile(n, (512, 256, 128))
    return pl.pallas_call(
        _mods_kernel,
        grid=(n // tn,),
        in_specs=[
            pl.BlockSpec((bsz, d), lambda j: (0, 0)),
            pl.BlockSpec((None, d, tn), lambda j: (layer, 0, j)),
            pl.BlockSpec((None, 1, tn), lambda j: (layer, 0, j)),
        ],
        out_specs=pl.BlockSpec((bsz, tn), lambda j: (0, j)),
        out_shape=jax.ShapeDtypeStruct((bsz, n), F32),
        compiler_params=_compiler_params(("parallel",)),
        name="mods",
    )(c, w, b.reshape(b.shape[0], 1, n))


def _normmod_rows(x_ref, rows, gain, mul, shift):
    xf = x_ref[rows, :]
    y = xf * lax.rsqrt(jnp.mean(xf * xf, axis=-1, keepdims=True) + EPS) * gain
    return y * mul + shift


def _row_pieces(tm):
    return [slice(r, r + PIECE_ROWS) for r in range(0, tm, PIECE_ROWS)]


def _lhs_pieces(first, x_ref, gn_ref, sh_ref, sc_ref, h_scr):
    gain = gn_ref[...]
    mul = 1.0 + sc_ref[...]
    shift = sh_ref[...]
    for rows in _row_pieces(h_scr.shape[0]):
        if first:
            h_scr[rows, :] = _normmod_rows(x_ref, rows, gain, mul, shift).astype(BF16)
        yield rows, h_scr[rows, :]


def _deint_lhs_kernel(x_ref, gn_ref, sh_ref, sc_ref, o_ref, slab_scr, *, dilations, bases):
    tm, d_model = x_ref.shape
    n_slabs = d_model // LANES
    gain = gn_ref[...]
    mul = 1.0 + sc_ref[...]
    shift = sh_ref[...]

    def natural(it, carry):
        rows = pl.ds(pl.multiple_of(it * FILL_ROWS, FILL_ROWS), FILL_ROWS)
        ssq = None
        for s in range(n_slabs):
            xs = x_ref[rows, s * LANES:(s + 1) * LANES]
            ssq = xs * xs if ssq is None else ssq + xs * xs
        inv = lax.rsqrt(jnp.sum(ssq, axis=-1, keepdims=True) * (1.0 / d_model) + EPS)
        inv = jnp.broadcast_to(inv, (FILL_ROWS, LANES))
        for s in range(n_slabs):
            cols = slice(s * LANES, (s + 1) * LANES)
            h = x_ref[rows, cols] * inv * gain[:, cols] * mul[:, cols] + shift[:, cols]
            slab_scr[0, s, rows, :] = h
            for v, d in enumerate(dilations):
                if d == 1:
                    o_ref[v, rows, cols] = h.astype(BF16)
        return carry

    lax.fori_loop(0, tm // FILL_ROWS, natural, 0)

    for v, d in enumerate(dilations):
        if d == 1:
            continue
        p = max(b for b in bases if d % b == 0 and b < d)
        f = d // p
        src_slabs = slab_scr.at[bases.index(p)]
        dst_slabs = slab_scr.at[bases.index(d)] if d in bases else None
        per_residue = tm // d
        units_per_trip = max(1, FILL_ROWS // per_residue)

        def gather(it, carry, v=v, d=d, p=p, f=f, per_residue=per_residue,
                   units_per_trip=units_per_trip, src_slabs=src_slabs, dst_slabs=dst_slabs):
            for u in range(units_per_trip):
                r = it * units_per_trip + u
                k = r // p
                rp = r % p
                src = pl.ds(rp * (tm // p) + k, per_residue, stride=f)
                dst = pl.ds(pl.multiple_of(r * per_residue, per_residue), per_residue)
                for s in range(n_slabs):
                    piece = src_slabs[s, src, :]
                    o_ref[v, dst, s * LANES:(s + 1) * LANES] = piece.astype(BF16)
                    if dst_slabs is not None:
                        dst_slabs[s, dst, :] = piece
            return carry

        lax.fori_loop(0, d // units_per_trip, gather, 0)


def _deint_lhs(x2, seq, gnorm, shift, scale, dilations):
    m, d = x2.shape
    tm = DEINT_ROWS
    assert seq % tm == 0 and all(tm % (dl * BF16_SUBLANES) == 0 for dl in dilations)
    tpb = seq // tm
    n_var = len(dilations)
    assert list(dilations) == sorted(dilations) and dilations[0] == 1
    bases = [1]
    for dl in dilations[1:]:
        base = max(b for b in dilations if dl % b == 0 and b < dl)
        if base not in bases:
            bases.append(base)
    bases = tuple(bases)
    return pl.pallas_call(
        functools.partial(_deint_lhs_kernel, dilations=dilations, bases=bases),
        grid=(m // tm,),
        in_specs=[
            pl.BlockSpec((tm, d), lambda i: (i, 0)),
            pl.BlockSpec((1, d), lambda i: (0, 0)),
            pl.BlockSpec((None, 1, d), lambda i: (i // tpb, 0, 0)),
            pl.BlockSpec((None, 1, d), lambda i: (i // tpb, 0, 0)),
        ],
        out_specs=pl.BlockSpec((n_var, tm, d), lambda i: (0, i, 0)),
        out_shape=jax.ShapeDtypeStruct((n_var, m, d), BF16),
        scratch_shapes=[pltpu.VMEM((len(bases), d // LANES, tm, LANES), F32)],
        compiler_params=_compiler_params(("parallel",)),
        name="deint_lhs",
    )(x2, gnorm.reshape(1, d), shift, scale)


def _dot_head_norm(pieces, w_ref, o_ref, acc_scr, gain, cos_ref=None, sin_ref=None):
    tn = acc_scr.shape[1]
    w = w_ref[...]
    done = []
    for rows, lhs in pieces:
        acc_scr[rows, :] = jnp.dot(lhs, w, preferred_element_type=F32)
        done.append(rows)
    for rows in done:
        if cos_ref is not None:
            cos = cos_ref[rows, :]
            sin = sin_ref[rows, :]
        for h in range(tn // HEAD_DIM):
            cols = slice(h * HEAD_DIM, (h + 1) * HEAD_DIM)
            y = _head_rmsnorm(acc_scr[rows, cols], gain)
            if cos_ref is not None:
                y = y * cos + pltpu.roll(y, HEAD_DIM // 2, 1) * sin
            o_ref[rows, cols] = y.astype(o_ref.dtype)


def _qkv_weight_kernel(w_ref, perm_ref, o_ref, *, n_perm_tiles):
    w = w_ref[...].astype(BF16)

    @pl.when(pl.program_id(1) < n_perm_tiles)
    def _():
        for h in range(w.shape[1] // HEAD_DIM):
            cols = slice(h * HEAD_DIM, (h + 1) * HEAD_DIM)
            o_ref[:, cols] = jnp.dot(w[:, cols], perm_ref[...],
                                     preferred_element_type=F32).astype(BF16)

    @pl.when(pl.program_id(1) >= n_perm_tiles)
    def _():
        o_ref[...] = w


def _qkv_weights(w, n_groups):
    n_layers, d, n = w.shape
    tn = PROJ_COLS
    assert n == 3 * n_groups * tn
    order = _rotary_lane_order()
    perm = np.zeros((HEAD_DIM, HEAD_DIM), np.float32)
    perm[order, np.arange(HEAD_DIM)] = 1.0
    return pl.pallas_call(
        functools.partial(_qkv_weight_kernel, n_perm_tiles=2 * n_groups),
        grid=(n_layers, n // tn),
        in_specs=[
            pl.BlockSpec((None, d, tn), lambda l, j: (l, 0, j)),
            pl.BlockSpec((HEAD_DIM, HEAD_DIM), lambda l, j: (0, 0)),
        ],
        out_specs=pl.BlockSpec((None, d, tn), lambda l, j: (l, 0, j)),
        out_shape=jax.ShapeDtypeStruct(w.shape, BF16),
        compiler_params=_compiler_params(("parallel", "parallel")),
        name="qkv_weights",
    )(w, jnp.asarray(perm, BF16))


def _qkv_kernel(h_ref, w_ref, gain_ref, cos_ref, sin_ref, o_ref, acc_scr, *, n_parts):
    part = pl.program_id(1) % n_parts

    @pl.when(part < 2)
    def _():
        gain = gain_ref[...] * jnp.where(part == 0, HEAD_DIM ** -0.5 * LOG2E, 1.0)
        pieces = [(rows, h_ref[rows, :]) for rows in _row_pieces(h_ref.shape[0])]
        _dot_head_norm(pieces, w_ref, o_ref, acc_scr, gain, cos_ref, sin_ref)

    @pl.when(part >= 2)
    def _():
        o_ref[...] = jnp.dot(h_ref[...], w_ref[...],
                             preferred_element_type=F32).astype(o_ref.dtype)


def _qkv_proj(h_var, seq, w, layer, gains, rope_tables):
    n_groups, m, d = h_var.shape
    n = w.shape[-1]
    n_parts = 3
    tm = _pick_tile(seq, (QKV_ROWS, PROJ_ROWS))
    tn = PROJ_COLS
    assert n == n_parts * n_groups * tn
    tpb = seq // tm

    def col(j):
        return (j % n_parts) * n_groups + j // n_parts

    return pl.pallas_call(
        functools.partial(_qkv_kernel, n_parts=n_parts),
        grid=(m // tm, n_parts * n_groups),
        in_specs=[
            pl.BlockSpec((None, tm, d), lambda i, j: (j // n_parts, i, 0)),
            pl.BlockSpec((None, d, tn), lambda i, j: (layer, 0, col(j))),
            pl.BlockSpec((None, 1, HEAD_DIM),
                         lambda i, j: (jnp.minimum(col(j), 2 * n_groups - 1), 0, 0)),
            pl.BlockSpec((None, tm, HEAD_DIM), lambda i, j: (j // n_parts, i % tpb, 0)),
            pl.BlockSpec((None, tm, HEAD_DIM), lambda i, j: (j // n_parts, i % tpb, 0)),
        ],
        out_specs=pl.BlockSpec((tm, tn), lambda i, j: (i, col(j))),
        out_shape=jax.ShapeDtypeStruct((m, n), BF16),
        scratch_shapes=[pltpu.VMEM((tm, tn), F32)],
        compiler_params=_compiler_params(("parallel", "arbitrary")),
        name="qkv_proj",
    )(h_var, w, gains, *rope_tables)


def _split3_bf16(v):
    hi = v.astype(BF16)
    rem = v - hi.astype(F32)
    mid = rem.astype(BF16)
    lo = (rem - mid.astype(F32)).astype(BF16)
    return hi, mid, lo


def _proj_kernel(*refs, n_norm_tiles, forget, tiles_per_batch, out_scale):
    x_ref, gn_ref, sh_ref, sc_ref, w_ref, gain_ref = refs[:6]
    pos = 6
    if forget:
        wf_ref, bf_ref, tri_ref, place_ref, ones_ref = refs[pos:pos + 5]
        pos += 5
    o_ref = refs[pos]
    pos += 1
    if forget:
        fk_ref, fq_ref = refs[pos:pos + 2]
        pos += 2
    h_scr, acc_scr = refs[pos:pos + 2]
    pos += 2
    if forget:
        carry_scr = refs[pos]

    i = pl.program_id(0)
    j = pl.program_id(1)
    tm = h_scr.shape[0]
    gain = gain_ref[...] * out_scale

    @pl.when(j == 0)
    def _():
        pieces = _lhs_pieces(True, x_ref, gn_ref, sh_ref, sc_ref, h_scr)
        _dot_head_norm(pieces, w_ref, o_ref, acc_scr, gain)
        if forget:
            z = jnp.dot(h_scr[...], wf_ref[...], preferred_element_type=F32) + bf_ref[...]
            log_f = jnp.minimum(z, 0.0) - jnp.log1p(jnp.exp(-jnp.abs(z)))
            @pl.when(i % tiles_per_batch == 0)
            def _():
                carry_scr[...] = jnp.zeros_like(carry_scr)

            tri = tri_ref[...]
            parts = _split3_bf16(log_f)
            running = carry_scr[...]
            blocks = []
            for r0 in range(0, tm, CUMSUM_ROWS):
                blk_cum = running
                for part in parts:
                    blk_cum = blk_cum + jnp.dot(tri, part[r0:r0 + CUMSUM_ROWS, :],
                                                preferred_element_type=F32)
                running = blk_cum[CUMSUM_ROWS - 1:CUMSUM_ROWS, :]
                blocks.append(blk_cum)
            carry_scr[...] = running
            cum = jnp.concatenate(blocks, axis=0)
            bias = ones_ref[...]
            for p, part in enumerate(_split3_bf16(cum * (-LOG2E))):
                bias = bias + jnp.dot(part, place_ref[p], preferred_element_type=F32)
            fk_ref[...] = bias[:, :LANES].astype(BF16)
            fq_ref[...] = bias[:, LANES:].astype(BF16)

    @pl.when((j > 0) & (j < n_norm_tiles))
    def _():
        pieces = _lhs_pieces(False, x_ref, gn_ref, sh_ref, sc_ref, h_scr)
        _dot_head_norm(pieces, w_ref, o_ref, acc_scr, gain)

    @pl.when(j >= n_norm_tiles)
    def _():
        o_ref[...] = jnp.dot(h_scr[...], w_ref[...],
                             preferred_element_type=F32).astype(o_ref.dtype)


def _fox_bias_lanes(n_heads):
    assert 6 * n_heads <= LANES
    place = np.zeros((3, LANES, 2 * LANES), np.float32)
    ones = np.zeros((1, 2 * LANES), np.float32)
    for h in range(n_heads):
        for p in range(3):
            place[p, h, 3 * h + p] = 1.0
            place[p, h, LANES + 3 * (n_heads + h) + p] = -1.0
    ones[0, 3 * n_heads:6 * n_heads] = 1.0
    ones[0, LANES:LANES + 3 * n_heads] = 1.0
    return jnp.asarray(place, BF16), jnp.asarray(ones, F32)


def _proj(x2, seq, gnorm, shift, scale, w, layer, gain, n_norm_tiles, forget=None,
          out_scale=1.0):
    m, d = x2.shape
    n = w.shape[-1]
    tm = PROJ_ROWS
    tn = PROJ_COLS
    assert seq % tm == 0 and n % tn == 0
    tpb = seq // tm
    with_forget = forget is not None

    in_specs = [
        pl.BlockSpec((tm, d), lambda i, j: (i, 0)),
        pl.BlockSpec((1, d), lambda i, j: (0, 0)),
        pl.BlockSpec((None, 1, d), lambda i, j: (i // tpb, 0, 0)),
        pl.BlockSpec((None, 1, d), lambda i, j: (i // tpb, 0, 0)),
        pl.BlockSpec((None, d, tn), lambda i, j: (layer, 0, j)),
        pl.BlockSpec((1, HEAD_DIM), lambda i, j: (0, 0)),
    ]
    args = [x2, gnorm.reshape(1, d), shift, scale, w, gain]
    out_specs = [pl.BlockSpec((tm, tn), lambda i, j: (i, j))]
    out_shape = [jax.ShapeDtypeStruct((m, n), BF16)]
    scratch = [pltpu.VMEM((tm, d), BF16), pltpu.VMEM((tm, tn), F32)]
    if with_forget:
        wf_pad, bf_pad, n_heads = forget
        tri = (lax.broadcasted_iota(jnp.int32, (CUMSUM_ROWS, CUMSUM_ROWS), 0)
               >= lax.broadcasted_iota(jnp.int32, (CUMSUM_ROWS, CUMSUM_ROWS), 1)).astype(BF16)
        place, ones = _fox_bias_lanes(n_heads)
        in_specs += [
            pl.BlockSpec((d, LANES), lambda i, j: (0, 0)),
            pl.BlockSpec((1, LANES), lambda i, j: (0, 0)),
            pl.BlockSpec((CUMSUM_ROWS, CUMSUM_ROWS), lambda i, j: (0, 0)),
            pl.BlockSpec((3, LANES, 2 * LANES), lambda i, j: (0, 0, 0)),
            pl.BlockSpec((1, 2 * LANES), lambda i, j: (0, 0)),
        ]
        args += [wf_pad, bf_pad, tri, place, ones]
        out_specs += [
            pl.BlockSpec((tm, LANES), lambda i, j: (i, 0)),
            pl.BlockSpec((tm, LANES), lambda i, j: (i, 0)),
        ]
        out_shape += [
            jax.ShapeDtypeStruct((m, LANES), BF16),
            jax.ShapeDtypeStruct((m, LANES), BF16),
        ]
        scratch.append(pltpu.VMEM((1, LANES), F32))

    kern = functools.partial(_proj_kernel, n_norm_tiles=n_norm_tiles, forget=with_forget,
                             tiles_per_batch=tpb, out_scale=out_scale)
    outs = pl.pallas_call(
        kern,
        grid=(m // tm, n // tn),
        in_specs=in_specs,
        out_specs=out_specs,
        out_shape=out_shape,
        scratch_shapes=scratch,
        compiler_params=_compiler_params(("arbitrary", "arbitrary")),
        name="proj",
    )(*args)
    return outs if with_forget else outs[0]


def _swiglu_kernel(x_ref, gn_ref, sh_ref, sc_ref, wg_ref, wu_ref, o_ref, h_scr):
    def gated(h):
        g = jnp.dot(h, wg_ref[...], preferred_element_type=F32)
        u = jnp.dot(h, wu_ref[...], preferred_element_type=F32)
        return (_silu(g) * u).astype(o_ref.dtype)

    @pl.when(pl.program_id(1) == 0)
    def _():
        for rows, h in _lhs_pieces(True, x_ref, gn_ref, sh_ref, sc_ref, h_scr):
            o_ref[rows, :] = gated(h)

    @pl.when(pl.program_id(1) > 0)
    def _():
        o_ref[...] = gated(h_scr[...])


def _swiglu_in(x2, seq, gnorm, shift, scale, w, layer):
    m, d = x2.shape
    f = w.shape[-1] // 2
    tm = PROJ_ROWS
    tf = _pick_tile(f, (512, 256, 128))
    assert seq % tm == 0
    tpb = seq // tm
    nf = f // tf
    return pl.pallas_call(
        _swiglu_kernel,
        grid=(m // tm, nf),
        in_specs=[
            pl.BlockSpec((tm, d), lambda i, j: (i, 0)),
            pl.BlockSpec((1, d), lambda i, j: (0, 0)),
            pl.BlockSpec((None, 1, d), lambda i, j: (i // tpb, 0, 0)),
            pl.BlockSpec((None, 1, d), lambda i, j: (i // tpb, 0, 0)),
            pl.BlockSpec((None, d, tf), lambda i, j: (layer, 0, j)),
            pl.BlockSpec((None, d, tf), lambda i, j: (layer, 0, nf + j)),
        ],
        out_specs=pl.BlockSpec((tm, tf), lambda i, j: (i, j)),
        out_shape=jax.ShapeDtypeStruct((m, f), BF16),
        scratch_shapes=[pltpu.VMEM((tm, d), BF16)],
        compiler_params=_compiler_params(("parallel", "arbitrary")),
        name="swiglu_in",
    )(x2, gnorm.reshape(1, d), shift, scale, w, w)


def _mmres_kernel(a_ref, w_ref, x_ref, gt_ref, o_ref):
    y = jnp.dot(a_ref[...], w_ref[...], preferred_element_type=F32)
    o_ref[...] = x_ref[...] + gt_ref[...] * y


def _mm_residual(a, w, layer, x2, seq, gate):
    m, k = a.shape
    n = w.shape[-1]
    tm = RESIDUAL_ROWS
    assert seq % tm == 0
    tpb = seq // tm
    return pl.pallas_call(
        _mmres_kernel,
        grid=(m // tm,),
        in_specs=[
            pl.BlockSpec((tm, k), lambda i: (i, 0)),
            pl.BlockSpec((None, k, n), lambda i: (layer, 0, 0), pipeline_mode=pl.Buffered(1)),
            pl.BlockSpec((tm, n), lambda i: (i, 0)),
            pl.BlockSpec((None, 1, n), lambda i: (i // tpb, 0, 0)),
        ],
        out_specs=pl.BlockSpec((tm, n), lambda i: (i, 0)),
        out_shape=jax.ShapeDtypeStruct((m, n), F32),
        compiler_params=_compiler_params(("parallel",)),
        name="mm_residual",
    )(a, w, x2, gate)


def _dilated_kernel(*refs, n_back, n_heads, dilation, has_prev, units):
    ins = refs[:3 * units]
    o_ref, lse_ref, o_scr = refs[3 * units:3 * units + 3]
    if has_prev:
        kp_scr, vp_scr = refs[3 * units + 3:]
    step_blk = pl.program_id(1)
    step_res = pl.program_id(2)
    width = n_heads * HEAD_DIM
    n_keys = (2 if has_prev else 1) * ATT_BLOCK
    qi = lax.broadcasted_iota(jnp.int32, (ATT_BLOCK, n_keys), 0)
    kj = lax.broadcasted_iota(jnp.int32, (ATT_BLOCK, n_keys), 1)
    dist = qi + (n_keys - ATT_BLOCK) - kj
    band = (dist >= 0) & (dist <= n_back)
    lane = lax.broadcasted_iota(jnp.int32, (ATT_BLOCK, LANES), 1)
    contract_last = (((1,), (1,)), ((), ()))
    head_cols = [slice(h * HEAD_DIM, (h + 1) * HEAD_DIM) for h in range(n_heads)]

    blocks = []
    for u in range(units):
        q_ref, kc_ref, vc_ref = ins[3 * u:3 * u + 3]
        if dilation == 1:
            blk, res = step_blk * units + u, 0
            rows = slice(u * ATT_BLOCK, (u + 1) * ATT_BLOCK)
        else:
            blk, res = step_blk, step_res * units + u
            rows = pl.ds(res, ATT_BLOCK, stride=dilation)
        q = q_ref[...].reshape(ATT_BLOCK, width)
        k = kc_ref[...].reshape(ATT_BLOCK, width)
        v = vc_ref[...].reshape(ATT_BLOCK, width)
        mask = band
        if has_prev:
            kp_ref = kp_scr.at[res]
            vp_ref = vp_scr.at[res]
            if dilation > 1 or u == 0:
                @pl.when(blk == 0)
                def _(kp_ref=kp_ref, vp_ref=vp_ref):
                    kp_ref[...] = jnp.zeros_like(kp_ref)
                    vp_ref[...] = jnp.zeros_like(vp_ref)
            k_prev = kp_ref[...]
            v_prev = vp_ref[...]
            kp_ref[...] = k
            vp_ref[...] = v
            k = jnp.concatenate([k_prev, k], axis=0)
            v = jnp.concatenate([v_prev, v], axis=0)
            mask = band & ((kj >= ATT_BLOCK) | (blk > 0))
        blocks.append((q, k, v, mask, rows))

    scores = [[lax.dot_general(q[:, cols], k[:, cols], contract_last,
                               preferred_element_type=F32) for cols in head_cols]
              for q, k, _, _, _ in blocks]
    probs = []
    for (_, _, _, mask, rows), unit_scores in zip(blocks, scores):
        unit_probs = []
        lse_tile = jnp.zeros((ATT_BLOCK, LANES), F32)
        for h, s2 in enumerate(unit_scores):
            s2 = jnp.where(mask, s2, NEG_INF)
            mx2 = jnp.max(s2, axis=-1, keepdims=True)
            p = jnp.exp2(s2 - mx2)
            den = jnp.sum(p, axis=-1, keepdims=True)
            unit_probs.append((p.astype(BF16), den))
            lse_tile = jnp.where(lane == h, mx2 * LN2 + jnp.log(den), lse_tile)
        lse_ref[rows, :] = lse_tile
        probs.append(unit_probs)
    for (_, _, v, _, rows), unit_probs in zip(blocks, probs):
        for h, (p, den) in enumerate(unit_probs):
            o = jnp.dot(p, v[:, head_cols[h]], preferred_element_type=F32)
            o_scr[h, rows, :] = o / den

    @pl.when(step_res == pl.num_programs(2) - 1)
    def _():
        for h in range(n_heads):
            o_ref[:, h * HEAD_DIM:(h + 1) * HEAD_DIM] = o_scr[h].astype(o_ref.dtype)


def _dilated_group(qkv, bsz, seq, group, n_groups, width, window, dilation):
    n_back = window // dilation
    sub_len = seq // dilation
    assert sub_len % ATT_BLOCK == 0
    nb = sub_len // ATT_BLOCK
    n_cols = qkv.shape[1]
    tpb = seq // DEINT_ROWS
    per_residue = DEINT_ROWS // dilation

    if per_residue >= ATT_BLOCK:
        bpt = per_residue // ATT_BLOCK
        view = qkv.reshape(bsz, tpb, dilation, bpt, ATT_BLOCK, n_cols)
        block = (None, None, None, None, ATT_BLOCK, width)

        def rows_index(b, n, r):
            return (b, n // bpt, r, n % bpt, 0)
    else:
        pieces = ATT_BLOCK // per_residue
        view = qkv.reshape(bsz, tpb, dilation, per_residue, n_cols)
        block = (None, pieces, None, per_residue, width)

        def rows_index(b, n, r):
            return (b, n, r, 0)

    units = DILATED_UNITS if (nb if dilation == 1 else dilation) % DILATED_UNITS == 0 else 1

    def spec(part, u):
        def index(b, n, r):
            if dilation == 1:
                rows = rows_index(b, n * units + u, r)
            else:
                rows = rows_index(b, n, r * units + u)
            return rows + (part * n_groups + group,)
        return pl.BlockSpec(block, index)

    if dilation == 1:
        grid = (bsz, nb // units, 1)
        span = ATT_BLOCK * units
    else:
        grid = (bsz, nb, dilation // units)
        span = ATT_BLOCK * dilation
    has_prev = nb > 1
    in_specs = [spec(part, u) for u in range(units) for part in range(3)]
    scratch = [pltpu.VMEM((width // HEAD_DIM, span, HEAD_DIM), F32)]
    if has_prev:
        scratch += [pltpu.VMEM((dilation, ATT_BLOCK, width), BF16)] * 2
    o, lse = pl.pallas_call(
        functools.partial(_dilated_kernel, n_back=n_back, n_heads=width // HEAD_DIM,
                          dilation=dilation, has_prev=has_prev, units=units),
        grid=grid,
        in_specs=in_specs,
        out_specs=[
            pl.BlockSpec((None, span, width), lambda b, n, r: (b, n, 0)),
            pl.BlockSpec((None, span, LANES), lambda b, n, r: (b, n, 0)),
        ],
        out_shape=[
            jax.ShapeDtypeStruct((bsz, seq, width), BF16),
            jax.ShapeDtypeStruct((bsz, seq, LANES), F32),
        ],
        scratch_shapes=scratch,
        compiler_params=_compiler_params(("parallel", "arbitrary", "arbitrary")),
        name="dilated_attention",
    )(*([view] * len(in_specs)))
    return o.reshape(bsz * seq, width), lse.reshape(bsz * seq, LANES)


def _mix_out_kernel(*refs, n_groups, n_heads, row_chunk):
    o_refs = refs[:n_groups]
    lse_refs = refs[n_groups:2 * n_groups]
    w_ref, x_ref, gt_ref, out_ref, lhs_scr = refs[2 * n_groups:]
    tm = x_ref.shape[0]
    w = w_ref[...]
    gate = gt_ref[...]

    for r in range(0, tm, row_chunk):
        rows = slice(r, r + row_chunk)
        lses = [ref[rows, :] for ref in lse_refs]
        mx = functools.reduce(jnp.maximum, lses)
        es = [jnp.exp(l - mx) for l in lses]
        inv = 1.0 / functools.reduce(lambda a, b: a + b, es)
        alphas = [e * inv for e in es]
        for h in range(n_heads):
            cols = slice(h * HEAD_DIM, (h + 1) * HEAD_DIM)
            mixed = None
            for g in range(n_groups):
                term = alphas[g][:, h:h + 1] * o_refs[g][rows, cols].astype(F32)
                mixed = term if mixed is None else mixed + term
            lhs_scr[rows, cols] = mixed.astype(BF16)
        y = jnp.dot(lhs_scr[rows, :], w, preferred_element_type=F32)
        out_ref[rows, :] = x_ref[rows, :] + gate * y


def _mix_out(o_list, lse_list, w, layer, x2, seq, gate):
    m, width = o_list[0].shape
    n = w.shape[-1]
    n_groups = len(o_list)
    tm = _pick_tile(seq, (512, 256, 128))
    tpb = seq // tm
    return pl.pallas_call(
        functools.partial(_mix_out_kernel, n_groups=n_groups, n_heads=width // HEAD_DIM,
                          row_chunk=128),
        grid=(m // tm,),
        in_specs=(
            [pl.BlockSpec((tm, width), lambda i: (i, 0))] * n_groups
            + [pl.BlockSpec((tm, LANES), lambda i: (i, 0))] * n_groups
            + [
                pl.BlockSpec((None, width, n), lambda i: (layer, 0, 0)),
                pl.BlockSpec((tm, n), lambda i: (i, 0)),
                pl.BlockSpec((None, 1, n), lambda i: (i // tpb, 0, 0)),
            ]
        ),
        out_specs=pl.BlockSpec((tm, n), lambda i: (i, 0)),
        out_shape=jax.ShapeDtypeStruct((m, n), F32),
        scratch_shapes=[pltpu.VMEM((tm, width), BF16)],
        compiler_params=_compiler_params(("parallel",)),
        name="mix_out",
    )(*o_list, *lse_list, w, x2, gate)


def _fox_kernel(q_ref, k_ref, v_ref, fq_ref, fk_ref, o_ref, *, blk, diag_rows, n_heads,
                heads_per_step):
    head0 = pl.program_id(1) * heads_per_step
    qt = pl.program_id(2)
    lane = lax.broadcasted_iota(jnp.int32, (1, LANES), 1)
    fq_all = fq_ref[...]
    contract_last = (((1,), (1,)), ((), ()))

    qs = []
    for hh in range(heads_per_step):
        off = (lane % (3 * n_heads)) - 3 * (head0 + hh)
        keep = jnp.where(off >= 0, jnp.where(off < 3, 1.0, 0.0), 0.0)
        keep = jnp.where(lane < 6 * n_heads, keep, 0.0).astype(BF16)
        qs.append(jnp.concatenate([q_ref[:, hh * HEAD_DIM:(hh + 1) * HEAD_DIM],
                                   fq_all * keep], axis=1))

    def update(kb, items, diagonal):
        scores = []
        for hh, row0, n_rows, n_keys, _ in items:
            rows = pl.ds(pl.multiple_of(kb * blk, blk), n_keys)
            cols = slice(hh * HEAD_DIM, (hh + 1) * HEAD_DIM)
            k = jnp.concatenate([k_ref[rows, cols], fk_ref[rows, :]], axis=1)
            s2 = lax.dot_general(qs[hh][row0:row0 + n_rows, :], k, contract_last,
                                 preferred_element_type=F32)
            if diagonal:
                qi = lax.broadcasted_iota(jnp.int32, (n_rows, n_keys), 0) + row0
                kj = lax.broadcasted_iota(jnp.int32, (n_rows, n_keys), 1)
                s2 = jnp.where(kj <= qi, s2, NEG_INF)
            scores.append((s2, rows, cols))
        probs = []
        for (s2, _, _), (_, _, _, _, (m_run, l_run, acc)) in zip(scores, items):
            m_new = jnp.maximum(m_run, jnp.max(s2, axis=-1, keepdims=True))
            alpha = jnp.exp2(m_run - m_new)
            p = jnp.exp2(s2 - m_new)
            l_new = alpha * l_run + jnp.sum(p, axis=-1, keepdims=True)
            probs.append((p.astype(BF16), m_new, l_new, alpha * acc))
        out = []
        for (p, m_new, l_new, acc_scaled), (_, rows, cols) in zip(probs, scores):
            acc_new = acc_scaled + jnp.dot(p, v_ref[rows, cols], preferred_element_type=F32)
            out.append((m_new, l_new, acc_new))
        return tuple(out)

    def below_diagonal(kb, carry):
        return update(kb, [(hh, 0, blk, blk, carry[hh]) for hh in range(heads_per_step)], False)

    init = tuple((jnp.full((blk, 1), NEG_INF, F32), jnp.zeros((blk, 1), F32),
                  jnp.zeros((blk, HEAD_DIM), F32)) for _ in range(heads_per_step))
    carry = lax.fori_loop(0, qt, below_diagonal, init)

    items = []
    for hh in range(heads_per_step):
        for row0 in range(0, blk, diag_rows):
            piece_carry = tuple(c[row0:row0 + diag_rows] for c in carry[hh])
            items.append((hh, row0, diag_rows, row0 + diag_rows, piece_carry))
    for (hh, row0, _, _, _), (_, l_fin, acc) in zip(items, update(qt, items, True)):
        o_ref[row0:row0 + diag_rows, hh * HEAD_DIM:(hh + 1) * HEAD_DIM] = (
            acc / l_fin).astype(o_ref.dtype)


def _fox_attention(q, kv, f_query, f_keys, bsz, seq, n_heads):
    blk = _pick_tile(seq, (FOX_BLOCK, 256, 128))
    nkb = seq // blk
    width = n_heads * HEAD_DIM
    hps = FOX_HEADS_PER_STEP
    assert n_heads % hps == 0
    n_hg = n_heads // hps
    qv = q.reshape(bsz, seq, width)
    kvv = kv.reshape(bsz, seq, 2 * width)
    o = pl.pallas_call(
        functools.partial(_fox_kernel, blk=blk, diag_rows=min(blk, FOX_DIAG_ROWS), n_heads=n_heads,
                          heads_per_step=hps),
        grid=(bsz, n_hg, nkb),
        in_specs=[
            pl.BlockSpec((None, blk, hps * HEAD_DIM), lambda b, h, t: (b, t, h)),
            pl.BlockSpec((None, seq, hps * HEAD_DIM), lambda b, h, t: (b, 0, h)),
            pl.BlockSpec((None, seq, hps * HEAD_DIM), lambda b, h, t: (b, 0, n_hg + h)),
            pl.BlockSpec((None, blk, LANES), lambda b, h, t: (b, t, 0)),
            pl.BlockSpec((None, seq, LANES), lambda b, h, t: (b, 0, 0)),
        ],
        out_specs=pl.BlockSpec((None, blk, hps * HEAD_DIM), lambda b, h, t: (b, t, h)),
        out_shape=jax.ShapeDtypeStruct((bsz, seq, width), BF16),
        compiler_params=_compiler_params(("parallel", "parallel", "arbitrary")),
        name="fox_attention",
    )(qv, kvv, kvv, f_query.reshape(bsz, seq, LANES), f_keys.reshape(bsz, seq, LANES))
    return o.reshape(bsz * seq, width)


def _rotary_lane_order():
    half = ROT_DIM // 2
    mid = HEAD_DIM // 2
    return np.concatenate([np.arange(0, half), np.arange(ROT_DIM, mid + half),
                           np.arange(half, ROT_DIM), np.arange(mid + half, HEAD_DIM)])


def _rope_tables(seq, dilations):
    half = ROT_DIM // 2
    mid = HEAD_DIM // 2
    inv = ROPE_THETA ** (-jnp.arange(0, ROT_DIM, 2, dtype=F32) / ROT_DIM)
    ang = jnp.arange(seq, dtype=F32)[:, None] * inv[None, :]
    cos, sin = jnp.cos(ang), jnp.sin(ang)
    ones = jnp.ones((seq, mid - half), F32)
    zeros = jnp.zeros((seq, mid - half), F32)
    cos_t = jnp.concatenate([cos, ones, cos, ones], axis=-1)
    sin_t = jnp.concatenate([-sin, zeros, sin, zeros], axis=-1)

    def deinterleave(t, d):
        t = t.reshape(seq // DEINT_ROWS, DEINT_ROWS // d, d, HEAD_DIM)
        return jnp.swapaxes(t, 1, 2).reshape(seq, HEAD_DIM)

    return (jnp.stack([deinterleave(cos_t, d) for d in dilations]),
            jnp.stack([deinterleave(sin_t, d) for d in dilations]))


def _split_mods(mods, parts):
    bsz, n = mods.shape
    d = n // parts
    return [mods[:, p * d:(p + 1) * d].reshape(bsz, 1, d) for p in range(parts)]


def kernel(x, c, w_ada, b_ada, g_norm_attn, g_norm_ffn, w_qkv_a, g_qk_a, w_o_a, w_ada_kv, b_ada_kv, g_norm_kv, w_kv, g_k_b, w_f, b_f, w_q_b, g_q_b, w_o_b, w_ffn_in, w_ffn_out):
    bsz, seq, d = x.shape
    depth = w_ada.shape[0]
    n_a = w_qkv_a.shape[0]
    n_groups = g_qk_a.shape[2]
    width_a = w_o_a.shape[1]
    n_heads_b = w_f.shape[1]
    assert n_groups == len(DIL_CONFIGS) and n_heads_b <= LANES and width_a == PROJ_COLS
    dilations = tuple(dl for _, dl in DIL_CONFIGS)

    x2 = x.reshape(bsz * seq, d)
    rope_tables = _rope_tables(seq, dilations)

    w_qkv = _qkv_weights(w_qkv_a, n_groups)
    g_qk = g_qk_a[..., _rotary_lane_order()]
    w_o_a = w_o_a.astype(BF16)
    w_kv = w_kv.astype(BF16)[None]
    w_q_b = w_q_b.astype(BF16)
    w_o_b = w_o_b.astype(BF16)
    w_ffn_in = w_ffn_in.astype(BF16)
    w_ffn_out = w_ffn_out.astype(BF16)
    wf_pad = jnp.pad(w_f, ((0, 0), (0, LANES - n_heads_b))).astype(BF16)
    bf_pad = jnp.pad(b_f, (0, LANES - n_heads_b)).reshape(1, LANES)

    kv = f_keys = f_query = None
    for layer in range(depth):
        sh_a, sc_a, gt_a, sh_f, sc_f, gt_f = _split_mods(_mods(c, w_ada, b_ada, layer), 6)
        if layer < n_a:
            gains = g_qk[layer].reshape(2 * n_groups, 1, HEAD_DIM)
            h_var = _deint_lhs(x2, seq, g_norm_attn[layer], sh_a, sc_a, dilations)
            qkv = _qkv_proj(h_var, seq, w_qkv, layer, gains, rope_tables)
            outs = [_dilated_group(qkv, bsz, seq, g, n_groups, width_a, window, dilation)
                    for g, (window, dilation) in enumerate(DIL_CONFIGS)]
            x2 = _mix_out([o for o, _ in outs], [l for _, l in outs], w_o_a, layer, x2, seq, gt_a)
        else:
            i = layer - n_a
            q = _proj(x2, seq, g_norm_attn[layer], sh_a, sc_a, w_q_b, i,
                      g_q_b[i].reshape(1, HEAD_DIM),
                      n_norm_tiles=w_q_b.shape[-1] // PROJ_COLS,
                      out_scale=HEAD_DIM ** -0.5 * LOG2E)
            o = _fox_attention(q, kv, f_query, f_keys, bsz, seq, n_heads_b)
            x2 = _mm_residual(o, w_o_b, i, x2, seq, gt_a)
        a = _swiglu_in(x2, seq, g_norm_ffn[layer], sh_f, sc_f, w_ffn_in, layer)
        x2 = _mm_residual(a, w_ffn_out, layer, x2, seq, gt_f)
        if layer == n_a - 1:
            sh_kv, sc_kv = _split_mods(_mods(c, w_ada_kv[None], b_ada_kv[None], 0), 2)
            kv, f_keys, f_query = _proj(
                x2, seq, g_norm_kv, sh_kv, sc_kv, w_kv, 0, g_k_b.reshape(1, HEAD_DIM),
                n_norm_tiles=w_kv.shape[-1] // 2 // PROJ_COLS, forget=(wf_pad, bf_pad, n_heads_b))
    return x2.reshape(bsz, seq, d)
```

```python
import functools
import math

import numpy as np
import jax
import jax.numpy as jnp
from jax import lax
from jax.experimental import pallas as pl
from jax.experimental.pallas import tpu as pltpu

HEAD_DIM = 128
DIL_CONFIGS = ((128, 1), (512, 4), (2048, 16))
ROT_DIM = HEAD_DIM // 4
ROPE_THETA = 500000.0
ATT_BLOCK = 128
DILATED_UNITS = 4
EPS = 1e-6
NEG_INF = -1e30
LOG2E = math.log2(math.e)
LN2 = math.log(2.0)

LANES = 128
BF16_SUBLANES = 16
VMEM_LIMIT_BYTES = 56 * 1024 * 1024

PROJ_ROWS = 1024
QKV_ROWS = 2048
PROJ_COLS = 1024
RESIDUAL_ROWS = 512
FILL_ROWS = 128
CUMSUM_ROWS = 256
DEINT_ROWS = 512
PIECE_ROWS = 256
FOX_BLOCK = 512
FOX_DIAG_ROWS = 512
FOX_HEADS_PER_STEP = 8

F32 = jnp.float32
BF16 = jnp.bfloat16


def _compiler_params(semantics):
    return pltpu.CompilerParams(dimension_semantics=semantics, vmem_limit_bytes=VMEM_LIMIT_BYTES)


def _pick_tile(n, candidates):
    for t in candidates:
        if n % t == 0:
            return t
    raise ValueError(f"no tile in {candidates} divides {n}")


def _head_rmsnorm(a, gain):
    return a * lax.rsqrt(jnp.mean(a * a, axis=-1, keepdims=True) + EPS) * gain


def _silu(v):
    return v * jax.nn.sigmoid(v)


def _mods_kernel(c_ref, w_ref, b_ref, o_ref):
    c_act = _silu(c_ref[...])
    o_ref[...] = jnp.dot(c_act.astype(BF16), w_ref[...].astype(BF16),
                         preferred_element_type=F32) + b_ref[...]


def _mods(c, w, b, layer):
    bsz, d = c.shape
    n = w.shape[-1]
    tn = _pick_tile(n, (512, 256, 128))
    return pl.pallas_call(
        _mods_kernel,
        grid=(n // tn,),
        in_specs=[
            pl.BlockSpec((bsz, d), lambda j: (0, 0)),
            pl.BlockSpec((None, d, tn), lambda j: (layer, 0, j)),
            pl.BlockSpec((None, 1, tn), lambda j: (layer, 0, j)),
        ],
        out_specs=pl.BlockSpec((bsz, tn), lambda j: (0, j)),
        out_shape=jax.ShapeDtypeStruct((bsz, n), F32),
        compiler_params=_compiler_params(("parallel",)),
        name="mods",
    )(c, w, b.reshape(b.shape[0], 1, n))


def _normmod_rows(x_ref, rows, gain, mul, shift):
    xf = x_ref[rows, :]
    y = xf * lax.rsqrt(jnp.mean(xf * xf, axis=-1, keepdims=True) + EPS) * gain
    return y * mul + shift


def _row_pieces(tm):
    return [slice(r, r + PIECE_ROWS) for r in range(0, tm, PIECE_ROWS)]


def _lhs_pieces(first, x_ref, gn_ref, sh_ref, sc_ref, h_scr):
    gain = gn_ref[...]
    mul = 1.0 + sc_ref[...]
    shift = sh_ref[...]
    for rows in _row_pieces(h_scr.shape[0]):
        if first:
            h_scr[rows, :] = _normmod_rows(x_ref, rows, gain, mul, shift).astype(BF16)
        yield rows, h_scr[rows, :]


def _deint_lhs_kernel(x_ref, gn_ref, sh_ref, sc_ref, o_ref, slab_scr, *, dilations, bases):
    tm, d_model = x_ref.shape
    n_slabs = d_model // LANES
    gain = gn_ref[...]
    mul = 1.0 + sc_ref[...]
    shift = sh_ref[...]

    def natural(it, carry):
        rows = pl.ds(pl.multiple_of(it * FILL_ROWS, FILL_ROWS), FILL_ROWS)
        ssq = None
        for s in range(n_slabs):
            xs = x_ref[rows, s * LANES:(s + 1) * LANES]
            ssq = xs * xs if ssq is None else ssq + xs * xs
        inv = lax.rsqrt(jnp.sum(ssq, axis=-1, keepdims=True) * (1.0 / d_model) + EPS)
        inv = jnp.broadcast_to(inv, (FILL_ROWS, LANES))
        for s in range(n_slabs):
            cols = slice(s * LANES, (s + 1) * LANES)
            h = x_ref[rows, cols] * inv * gain[:, cols] * mul[:, cols] + shift[:, cols]
            slab_scr[0, s, rows, :] = h
            for v, d in enumerate(dilations):
                if d == 1:
                    o_ref[v, rows, cols] = h.astype(BF16)
        return carry

    lax.fori_loop(0, tm // FILL_ROWS, natural, 0)

    for v, d in enumerate(dilations):
        if d == 1:
            continue
        p = max(b for b in bases if d % b == 0 and b < d)
        f = d // p
        src_slabs = slab_scr.at[bases.index(p)]
        dst_slabs = slab_scr.at[bases.index(d)] if d in bases else None
        per_residue = tm // d
        units_per_trip = max(1, FILL_ROWS // per_residue)

        def gather(it, carry, v=v, d=d, p=p, f=f, per_residue=per_residue,
                   units_per_trip=units_per_trip, src_slabs=src_slabs, dst_slabs=dst_slabs):
            for u in range(units_per_trip):
                r = it * units_per_trip + u
                k = r // p
                rp = r % p
                src = pl.ds(rp * (tm // p) + k, per_residue, stride=f)
                dst = pl.ds(pl.multiple_of(r * per_residue, per_residue), per_residue)
                for s in range(n_slabs):
                    piece = src_slabs[s, src, :]
                    o_ref[v, dst, s * LANES:(s + 1) * LANES] = piece.astype(BF16)
                    if dst_slabs is not None:
                        dst_slabs[s, dst, :] = piece
            return carry

        lax.fori_loop(0, d // units_per_trip, gather, 0)


def _deint_lhs(x2, seq, gnorm, shift, scale, dilations):
    m, d = x2.shape
    tm = DEINT_ROWS
    assert seq % tm == 0 and all(tm % (dl * BF16_SUBLANES) == 0 for dl in dilations)
    tpb = seq // tm
    n_var = len(dilations)
    assert list(dilations) == sorted(dilations) and dilations[0] == 1
    bases = [1]
    for dl in dilations[1:]:
        base = max(b for b in dilations if dl % b == 0 and b < dl)
        if base not in bases:
            bases.append(base)
    bases = tuple(bases)
    return pl.pallas_call(
        functools.partial(_deint_lhs_kernel, dilations=dilations, bases=bases),
        grid=(m // tm,),
        in_specs=[
            pl.BlockSpec((tm, d), lambda i: (i, 0)),
            pl.BlockSpec((1, d), lambda i: (0, 0)),
            pl.BlockSpec((None, 1, d), lambda i: (i // tpb, 0, 0)),
            pl.BlockSpec((None, 1, d), lambda i: (i // tpb, 0, 0)),
        ],
        out_specs=pl.BlockSpec((n_var, tm, d), lambda i: (0, i, 0)),
        out_shape=jax.ShapeDtypeStruct((n_var, m, d), BF16),
        scratch_shapes=[pltpu.VMEM((len(bases), d // LANES, tm, LANES), F32)],
        compiler_params=_compiler_params(("parallel",)),
        name="deint_lhs",
    )(x2, gnorm.reshape(1, d), shift, scale)


def _dot_head_norm(pieces, w_ref, o_ref, acc_scr, gain, cos_ref=None, sin_ref=None,
                   between=None):
    tn = acc_scr.shape[1]
    w = w_ref[...]
    done = []
    for rows, lhs in pieces:
        acc_scr[rows, :] = jnp.dot(lhs, w, preferred_element_type=F32)
        done.append(rows)
    if between is not None:
        between()
    for rows in done:
        if cos_ref is not None:
            cos = cos_ref[rows, :]
            sin = sin_ref[rows, :]
        for h in range(tn // HEAD_DIM):
            cols = slice(h * HEAD_DIM, (h + 1) * HEAD_DIM)
            y = _head_rmsnorm(acc_scr[rows, cols], gain)
            if cos_ref is not None:
                y = y * cos + pltpu.roll(y, HEAD_DIM // 2, 1) * sin
            o_ref[rows, cols] = y.astype(o_ref.dtype)


def _qkv_weight_kernel(w_ref, perm_ref, o_ref, *, n_perm_tiles):
    w = w_ref[...].astype(BF16)

    @pl.when(pl.program_id(1) < n_perm_tiles)
    def _():
        for h in range(w.shape[1] // HEAD_DIM):
            cols = slice(h * HEAD_DIM, (h + 1) * HEAD_DIM)
            o_ref[:, cols] = jnp.dot(w[:, cols], perm_ref[...],
                                     preferred_element_type=F32).astype(BF16)

    @pl.when(pl.program_id(1) >= n_perm_tiles)
    def _():
        o_ref[...] = w


def _qkv_weights(w, n_groups):
    n_layers, d, n = w.shape
    tn = PROJ_COLS
    assert n == 3 * n_groups * tn
    order = _rotary_lane_order()
    perm = np.zeros((HEAD_DIM, HEAD_DIM), np.float32)
    perm[order, np.arange(HEAD_DIM)] = 1.0
    return pl.pallas_call(
        functools.partial(_qkv_weight_kernel, n_perm_tiles=2 * n_groups),
        grid=(n_layers, n // tn),
        in_specs=[
            pl.BlockSpec((None, d, tn), lambda l, j: (l, 0, j)),
            pl.BlockSpec((HEAD_DIM, HEAD_DIM), lambda l, j: (0, 0)),
        ],
        out_specs=pl.BlockSpec((None, d, tn), lambda l, j: (l, 0, j)),
        out_shape=jax.ShapeDtypeStruct(w.shape, BF16),
        compiler_params=_compiler_params(("parallel", "parallel")),
        name="qkv_weights",
    )(w, jnp.asarray(perm, BF16))


def _qkv_kernel(h_ref, w_ref, gain_ref, cos_ref, sin_ref, o_ref, acc_scr, *, n_parts):
    part = pl.program_id(1) % n_parts

    @pl.when(part < 2)
    def _():
        gain = gain_ref[...] * jnp.where(part == 0, HEAD_DIM ** -0.5 * LOG2E, 1.0)
        pieces = [(rows, h_ref[rows, :]) for rows in _row_pieces(h_ref.shape[0])]
        _dot_head_norm(pieces, w_ref, o_ref, acc_scr, gain, cos_ref, sin_ref)

    @pl.when(part >= 2)
    def _():
        o_ref[...] = jnp.dot(h_ref[...], w_ref[...],
                             preferred_element_type=F32).astype(o_ref.dtype)


def _qkv_proj(h_var, seq, w, layer, gains, rope_tables):
    n_groups, m, d = h_var.shape
    n = w.shape[-1]
    n_parts = 3
    tm = _pick_tile(seq, (QKV_ROWS, PROJ_ROWS))
    tn = PROJ_COLS
    assert n == n_parts * n_groups * tn
    tpb = seq // tm

    def col(j):
        return (j % n_parts) * n_groups + j // n_parts

    return pl.pallas_call(
        functools.partial(_qkv_kernel, n_parts=n_parts),
        grid=(m // tm, n_parts * n_groups),
        in_specs=[
            pl.BlockSpec((None, tm, d), lambda i, j: (j // n_parts, i, 0)),
            pl.BlockSpec((None, d, tn), lambda i, j: (layer, 0, col(j))),
            pl.BlockSpec((None, 1, HEAD_DIM),
                         lambda i, j: (jnp.minimum(col(j), 2 * n_groups - 1), 0, 0)),
            pl.BlockSpec((None, tm, HEAD_DIM), lambda i, j: (j // n_parts, i % tpb, 0)),
            pl.BlockSpec((None, tm, HEAD_DIM), lambda i, j: (j // n_parts, i % tpb, 0)),
        ],
        out_specs=pl.BlockSpec((tm, tn), lambda i, j: (i, col(j))),
        out_shape=jax.ShapeDtypeStruct((m, n), BF16),
        scratch_shapes=[pltpu.VMEM((tm, tn), F32)],
        compiler_params=_compiler_params(("parallel", "arbitrary")),
        name="qkv_proj",
    )(h_var, w, gains, *rope_tables)


def _split3_bf16(v):
    hi = v.astype(BF16)
    rem = v - hi.astype(F32)
    mid = rem.astype(BF16)
    lo = (rem - mid.astype(F32)).astype(BF16)
    return hi, mid, lo


def _proj_kernel(*refs, n_norm_tiles, forget, tiles_per_batch, out_scale):
    x_ref, gn_ref, sh_ref, sc_ref, w_ref, gain_ref = refs[:6]
    pos = 6
    if forget:
        wf_ref, bf_ref, tri_ref, place_ref, ones_ref = refs[pos:pos + 5]
        pos += 5
    o_ref = refs[pos]
    pos += 1
    if forget:
        fk_ref, fq_ref = refs[pos:pos + 2]
        pos += 2
    h_scr, acc_scr = refs[pos:pos + 2]
    pos += 2
    if forget:
        carry_scr = refs[pos]

    i = pl.program_id(0)
    j = pl.program_id(1)
    tm = h_scr.shape[0]
    gain = gain_ref[...] * out_scale

    def forget_stream():
        z = jnp.dot(h_scr[...], wf_ref[...], preferred_element_type=F32) + bf_ref[...]
        log_f = jnp.minimum(z, 0.0) - jnp.log1p(jnp.exp(-jnp.abs(z)))
        tri = tri_ref[...]
        parts = _split3_bf16(log_f)
        running = carry_scr[...]
        blocks = []
        for r0 in range(0, tm, CUMSUM_ROWS):
            blk_cum = running
            for part in parts:
                blk_cum = blk_cum + jnp.dot(tri, part[r0:r0 + CUMSUM_ROWS, :],
                                            preferred_element_type=F32)
            running = blk_cum[CUMSUM_ROWS - 1:CUMSUM_ROWS, :]
            blocks.append(blk_cum)
        carry_scr[...] = running
        cum = jnp.concatenate(blocks, axis=0)
        bias = ones_ref[...]
        for p, part in enumerate(_split3_bf16(cum * (-LOG2E))):
            bias = bias + jnp.dot(part, place_ref[p], preferred_element_type=F32)
        fk_ref[...] = bias[:, :LANES].astype(BF16)
        fq_ref[...] = bias[:, LANES:].astype(BF16)

    @pl.when(j == 0)
    def _():
        if forget:
            @pl.when(i % tiles_per_batch == 0)
            def _():
                carry_scr[...] = jnp.zeros_like(carry_scr)

        pieces = _lhs_pieces(True, x_ref, gn_ref, sh_ref, sc_ref, h_scr)
        _dot_head_norm(pieces, w_ref, o_ref, acc_scr, gain,
                       between=forget_stream if forget else None)

    @pl.when((j > 0) & (j < n_norm_tiles))
    def _():
        pieces = _lhs_pieces(False, x_ref, gn_ref, sh_ref, sc_ref, h_scr)
        _dot_head_norm(pieces, w_ref, o_ref, acc_scr, gain)

    @pl.when(j >= n_norm_tiles)
    def _():
        o_ref[...] = jnp.dot(h_scr[...], w_ref[...],
                             preferred_element_type=F32).astype(o_ref.dtype)


def _fox_bias_lanes(n_heads):
    assert 6 * n_heads <= LANES
    place = np.zeros((3, LANES, 2 * LANES), np.float32)
    ones = np.zeros((1, 2 * LANES), np.float32)
    for h in range(n_heads):
        for p in range(3):
            place[p, h, 3 * h + p] = 1.0
            place[p, h, LANES + 3 * (n_heads + h) + p] = -1.0
    ones[0, 3 * n_heads:6 * n_heads] = 1.0
    ones[0, LANES:LANES + 3 * n_heads] = 1.0
    return jnp.asarray(place, BF16), jnp.asarray(ones, F32)


def _proj(x2, seq, gnorm, shift, scale, w, layer, gain, n_norm_tiles, forget=None,
          out_scale=1.0):
    m, d = x2.shape
    n = w.shape[-1]
    tm = PROJ_ROWS
    tn = PROJ_COLS
    assert seq % tm == 0 and n % tn == 0
    tpb = seq // tm
    with_forget = forget is not None

    in_specs = [
        pl.BlockSpec((tm, d), lambda i, j: (i, 0)),
        pl.BlockSpec((1, d), lambda i, j: (0, 0)),
        pl.BlockSpec((None, 1, d), lambda i, j: (i // tpb, 0, 0)),
        pl.BlockSpec((None, 1, d), lambda i, j: (i // tpb, 0, 0)),
        pl.BlockSpec((None, d, tn), lambda i, j: (layer, 0, j)),
        pl.BlockSpec((1, HEAD_DIM), lambda i, j: (0, 0)),
    ]
    args = [x2, gnorm.reshape(1, d), shift, scale, w, gain]
    out_specs = [pl.BlockSpec((tm, tn), lambda i, j: (i, j))]
    out_shape = [jax.ShapeDtypeStruct((m, n), BF16)]
    scratch = [pltpu.VMEM((tm, d), BF16), pltpu.VMEM((tm, tn), F32)]
    if with_forget:
        wf_pad, bf_pad, n_heads = forget
        tri = (lax.broadcasted_iota(jnp.int32, (CUMSUM_ROWS, CUMSUM_ROWS), 0)
               >= lax.broadcasted_iota(jnp.int32, (CUMSUM_ROWS, CUMSUM_ROWS), 1)).astype(BF16)
        place, ones = _fox_bias_lanes(n_heads)
        in_specs += [
            pl.BlockSpec((d, LANES), lambda i, j: (0, 0)),
            pl.BlockSpec((1, LANES), lambda i, j: (0, 0)),
            pl.BlockSpec((CUMSUM_ROWS, CUMSUM_ROWS), lambda i, j: (0, 0)),
            pl.BlockSpec((3, LANES, 2 * LANES), lambda i, j: (0, 0, 0)),
            pl.BlockSpec((1, 2 * LANES), lambda i, j: (0, 0)),
        ]
        args += [wf_pad, bf_pad, tri, place, ones]
        out_specs += [
            pl.BlockSpec((tm, LANES), lambda i, j: (i, 0)),
            pl.BlockSpec((tm, LANES), lambda i, j: (i, 0)),
        ]
        out_shape += [
            jax.ShapeDtypeStruct((m, LANES), BF16),
            jax.ShapeDtypeStruct((m, LANES), BF16),
        ]
        scratch.append(pltpu.VMEM((1, LANES), F32))

    kern = functools.partial(_proj_kernel, n_norm_tiles=n_norm_tiles, forget=with_forget,
                             tiles_per_batch=tpb, out_scale=out_scale)
    outs = pl.pallas_call(
        kern,
        grid=(m // tm, n // tn),
        in_specs=in_specs,
        out_specs=out_specs,
        out_shape=out_shape,
        scratch_shapes=scratch,
        compiler_params=_compiler_params(("arbitrary", "arbitrary")),
        name="proj",
    )(*args)
    return outs if with_forget else outs[0]


def _swiglu_kernel(x_ref, gn_ref, sh_ref, sc_ref, wg_ref, wu_ref, o_ref, h_scr):
    def gated(h):
        g = jnp.dot(h, wg_ref[...], preferred_element_type=F32)
        u = jnp.dot(h, wu_ref[...], preferred_element_type=F32)
        return (_silu(g) * u).astype(o_ref.dtype)

    @pl.when(pl.program_id(1) == 0)
    def _():
        for rows, h in _lhs_pieces(True, x_ref, gn_ref, sh_ref, sc_ref, h_scr):
            o_ref[rows, :] = gated(h)

    @pl.when(pl.program_id(1) > 0)
    def _():
        o_ref[...] = gated(h_scr[...])


def _swiglu_in(x2, seq, gnorm, shift, scale, w, layer):
    m, d = x2.shape
    f = w.shape[-1] // 2
    tm = PROJ_ROWS
    tf = _pick_tile(f, (512, 256, 128))
    assert seq % tm == 0
    tpb = seq // tm
    nf = f // tf
    return pl.pallas_call(
        _swiglu_kernel,
        grid=(m // tm, nf),
        in_specs=[
            pl.BlockSpec((tm, d), lambda i, j: (i, 0)),
            pl.BlockSpec((1, d), lambda i, j: (0, 0)),
            pl.BlockSpec((None, 1, d), lambda i, j: (i // tpb, 0, 0)),
            pl.BlockSpec((None, 1, d), lambda i, j: (i // tpb, 0, 0)),
            pl.BlockSpec((None, d, tf), lambda i, j: (layer, 0, j)),
            pl.BlockSpec((None, d, tf), lambda i, j: (layer, 0, nf + j)),
        ],
        out_specs=pl.BlockSpec((tm, tf), lambda i, j: (i, j)),
        out_shape=jax.ShapeDtypeStruct((m, f), BF16),
        scratch_shapes=[pltpu.VMEM((tm, d), BF16)],
        compiler_params=_compiler_params(("parallel", "arbitrary")),
        name="swiglu_in",
    )(x2, gnorm.reshape(1, d), shift, scale, w, w)


def _mmres_kernel(a_ref, w_ref, x_ref, gt_ref, o_ref):
    y = jnp.dot(a_ref[...], w_ref[...], preferred_element_type=F32)
    o_ref[...] = x_ref[...] + gt_ref[...] * y


def _mm_residual(a, w, layer, x2, seq, gate):
    m, k = a.shape
    n = w.shape[-1]
    tm = RESIDUAL_ROWS
    assert seq % tm == 0
    tpb = seq // tm
    return pl.pallas_call(
        _mmres_kernel,
        grid=(m // tm,),
        in_specs=[
            pl.BlockSpec((tm, k), lambda i: (i, 0)),
            pl.BlockSpec((None, k, n), lambda i: (layer, 0, 0), pipeline_mode=pl.Buffered(1)),
            pl.BlockSpec((tm, n), lambda i: (i, 0)),
            pl.BlockSpec((None, 1, n), lambda i: (i // tpb, 0, 0)),
        ],
        out_specs=pl.BlockSpec((tm, n), lambda i: (i, 0)),
        out_shape=jax.ShapeDtypeStruct((m, n), F32),
        compiler_params=_compiler_params(("parallel",)),
        name="mm_residual",
    )(a, w, x2, gate)


def _dilated_kernel(*refs, n_back, n_heads, dilation, has_prev, units):
    ins = refs[:3 * units]
    o_ref, lse_ref, o_scr = refs[3 * units:3 * units + 3]
    if has_prev:
        kp_scr, vp_scr = refs[3 * units + 3:]
    step_blk = pl.program_id(1)
    step_res = pl.program_id(2)
    width = n_heads * HEAD_DIM
    n_keys = (2 if has_prev else 1) * ATT_BLOCK
    qi = lax.broadcasted_iota(jnp.int32, (ATT_BLOCK, n_keys), 0)
    kj = lax.broadcasted_iota(jnp.int32, (ATT_BLOCK, n_keys), 1)
    dist = qi + (n_keys - ATT_BLOCK) - kj
    band = (dist >= 0) & (dist <= n_back)
    lane = lax.broadcasted_iota(jnp.int32, (ATT_BLOCK, LANES), 1)
    contract_last = (((1,), (1,)), ((), ()))
    head_cols = [slice(h * HEAD_DIM, (h + 1) * HEAD_DIM) for h in range(n_heads)]

    blocks = []
    for u in range(units):
        q_ref, kc_ref, vc_ref = ins[3 * u:3 * u + 3]
        if dilation == 1:
            blk, res = step_blk * units + u, 0
            rows = slice(u * ATT_BLOCK, (u + 1) * ATT_BLOCK)
        else:
            blk, res = step_blk, step_res * units + u
            rows = pl.ds(res, ATT_BLOCK, stride=dilation)
        q = q_ref[...].reshape(ATT_BLOCK, width)
        k = kc_ref[...].reshape(ATT_BLOCK, width)
        v = vc_ref[...].reshape(ATT_BLOCK, width)
        mask = band
        if has_prev:
            kp_ref = kp_scr.at[res]
            vp_ref = vp_scr.at[res]
            if dilation > 1 or u == 0:
                @pl.when(blk == 0)
                def _(kp_ref=kp_ref, vp_ref=vp_ref):
                    kp_ref[...] = jnp.zeros_like(kp_ref)
                    vp_ref[...] = jnp.zeros_like(vp_ref)
            k_prev = kp_ref[...]
            v_prev = vp_ref[...]
            kp_ref[...] = k
            vp_ref[...] = v
            k = jnp.concatenate([k_prev, k], axis=0)
            v = jnp.concatenate([v_prev, v], axis=0)
            mask = band & ((kj >= ATT_BLOCK) | (blk > 0))
        blocks.append((q, k, v, mask, rows))

    scores = [[lax.dot_general(q[:, cols], k[:, cols], contract_last,
                               preferred_element_type=F32) for cols in head_cols]
              for q, k, _, _, _ in blocks]
    probs = []
    for (_, _, _, mask, rows), unit_scores in zip(blocks, scores):
        unit_probs = []
        lse_tile = jnp.zeros((ATT_BLOCK, LANES), F32)
        for h, s2 in enumerate(unit_scores):
            s2 = jnp.where(mask, s2, NEG_INF)
            mx2 = jnp.max(s2, axis=-1, keepdims=True)
            p = jnp.exp2(s2 - mx2)
            den = jnp.sum(p, axis=-1, keepdims=True)
            unit_probs.append((p.astype(BF16), den))
            lse_tile = jnp.where(lane == h, mx2 * LN2 + jnp.log(den), lse_tile)
        lse_ref[rows, :] = lse_tile
        probs.append(unit_probs)
    for (_, _, v, _, rows), unit_probs in zip(blocks, probs):
        for h, (p, den) in enumerate(unit_probs):
            o = jnp.dot(p, v[:, head_cols[h]], preferred_element_type=F32)
            o_scr[h, rows, :] = o / den

    @pl.when(step_res == pl.num_programs(2) - 1)
    def _():
        for h in range(n_heads):
            o_ref[:, h * HEAD_DIM:(h + 1) * HEAD_DIM] = o_scr[h].astype(o_ref.dtype)


def _dilated_group(qkv, bsz, seq, group, n_groups, width, window, dilation):
    n_back = window // dilation
    sub_len = seq // dilation
    assert sub_len % ATT_BLOCK == 0
    nb = sub_len // ATT_BLOCK
    n_cols = qkv.shape[1]
    tpb = seq // DEINT_ROWS
    per_residue = DEINT_ROWS // dilation

    if per_residue >= ATT_BLOCK:
        bpt = per_residue // ATT_BLOCK
        view = qkv.reshape(bsz, tpb, dilation, bpt, ATT_BLOCK, n_cols)
        block = (None, None, None, None, ATT_BLOCK, width)

        def rows_index(b, n, r):
            return (b, n // bpt, r, n % bpt, 0)
    else:
        pieces = ATT_BLOCK // per_residue
        view = qkv.reshape(bsz, tpb, dilation, per_residue, n_cols)
        block = (None, pieces, None, per_residue, width)

        def rows_index(b, n, r):
            return (b, n, r, 0)

    units = DILATED_UNITS if (nb if dilation == 1 else dilation) % DILATED_UNITS == 0 else 1

    def spec(part, u):
        def index(b, n, r):
            if dilation == 1:
                rows = rows_index(b, n * units + u, r)
            else:
                rows = rows_index(b, n, r * units + u)
            return rows + (part * n_groups + group,)
        return pl.BlockSpec(block, index)

    if dilation == 1:
        grid = (bsz, nb // units, 1)
        span = ATT_BLOCK * units
    else:
        grid = (bsz, nb, dilation // units)
        span = ATT_BLOCK * dilation
    has_prev = nb > 1
    in_specs = [spec(part, u) for u in range(units) for part in range(3)]
    scratch = [pltpu.VMEM((width // HEAD_DIM, span, HEAD_DIM), F32)]
    if has_prev:
        scratch += [pltpu.VMEM((dilation, ATT_BLOCK, width), BF16)] * 2
    o, lse = pl.pallas_call(
        functools.partial(_dilated_kernel, n_back=n_back, n_heads=width // HEAD_DIM,
                          dilation=dilation, has_prev=has_prev, units=units),
        grid=grid,
        in_specs=in_specs,
        out_specs=[
            pl.BlockSpec((None, span, width), lambda b, n, r: (b, n, 0)),
            pl.BlockSpec((None, span, LANES), lambda b, n, r: (b, n, 0)),
        ],
        out_shape=[
            jax.ShapeDtypeStruct((bsz, seq, width), BF16),
            jax.ShapeDtypeStruct((bsz, seq, LANES), F32),
        ],
        scratch_shapes=scratch,
        compiler_params=_compiler_params(("parallel", "arbitrary", "arbitrary")),
        name="dilated_attention",
    )(*([view] * len(in_specs)))
    return o.reshape(bsz * seq, width), lse.reshape(bsz * seq, LANES)


def _mix_out_kernel(*refs, n_groups, n_heads, row_chunk):
    o_refs = refs[:n_groups]
    lse_refs = refs[n_groups:2 * n_groups]
    w_ref, x_ref, gt_ref, out_ref, lhs_scr = refs[2 * n_groups:]
    tm = x_ref.shape[0]
    w = w_ref[...]
    gate = gt_ref[...]

    for r in range(0, tm, row_chunk):
        rows = slice(r, r + row_chunk)
        lses = [ref[rows, :] for ref in lse_refs]
        mx = functools.reduce(jnp.maximum, lses)
        es = [jnp.exp(l - mx) for l in lses]
        inv = 1.0 / functools.reduce(lambda a, b: a + b, es)
        alphas = [e * inv for e in es]
        for h in range(n_heads):
            cols = slice(h * HEAD_DIM, (h + 1) * HEAD_DIM)
            mixed = None
            for g in range(n_groups):
                term = alphas[g][:, h:h + 1] * o_refs[g][rows, cols].astype(F32)
                mixed = term if mixed is None else mixed + term
            lhs_scr[rows, cols] = mixed.astype(BF16)
        y = jnp.dot(lhs_scr[rows, :], w, preferred_element_type=F32)
        out_ref[rows, :] = x_ref[rows, :] + gate * y


def _mix_out(o_list, lse_list, w, layer, x2, seq, gate):
    m, width = o_list[0].shape
    n = w.shape[-1]
    n_groups = len(o_list)
    tm = _pick_tile(seq, (512, 256, 128))
    tpb = seq // tm
    return pl.pallas_call(
        functools.partial(_mix_out_kernel, n_groups=n_groups, n_heads=width // HEAD_DIM,
                          row_chunk=128),
        grid=(m // tm,),
        in_specs=(
            [pl.BlockSpec((tm, width), lambda i: (i, 0))] * n_groups
            + [pl.BlockSpec((tm, LANES), lambda i: (i, 0))] * n_groups
            + [
                pl.BlockSpec((None, width, n), lambda i: (layer, 0, 0)),
                pl.BlockSpec((tm, n), lambda i: (i, 0)),
                pl.BlockSpec((None, 1, n), lambda i: (i // tpb, 0, 0)),
            ]
        ),
        out_specs=pl.BlockSpec((tm, n), lambda i: (i, 0)),
        out_shape=jax.ShapeDtypeStruct((m, n), F32),
        scratch_shapes=[pltpu.VMEM((tm, width), BF16)],
        compiler_params=_compiler_params(("parallel",)),
        name="mix_out",
    )(*o_list, *lse_list, w, x2, gate)


def _fox_kernel(q_ref, k_ref, v_ref, fq_ref, fk_ref, o_ref, *, blk, diag_rows, n_heads,
                heads_per_step):
    head0 = pl.program_id(1) * heads_per_step
    qt = pl.program_id(2)
    lane = lax.broadcasted_iota(jnp.int32, (1, LANES), 1)
    fq_all = fq_ref[...]
    contract_last = (((1,), (1,)), ((), ()))

    qs = []
    for hh in range(heads_per_step):
        off = (lane % (3 * n_heads)) - 3 * (head0 + hh)
        keep = jnp.where(off >= 0, jnp.where(off < 3, 1.0, 0.0), 0.0)
        keep = jnp.where(lane < 6 * n_heads, keep, 0.0).astype(BF16)
        qs.append(jnp.concatenate([q_ref[:, hh * HEAD_DIM:(hh + 1) * HEAD_DIM],
                                   fq_all * keep], axis=1))

    def update(kb, items, diagonal):
        scores = []
        for hh, row0, n_rows, n_keys, _ in items:
            rows = pl.ds(pl.multiple_of(kb * blk, blk), n_keys)
            cols = slice(hh * HEAD_DIM, (hh + 1) * HEAD_DIM)
            k = jnp.concatenate([k_ref[rows, cols], fk_ref[rows, :]], axis=1)
            s2 = lax.dot_general(qs[hh][row0:row0 + n_rows, :], k, contract_last,
                                 preferred_element_type=F32)
            if diagonal:
                qi = lax.broadcasted_iota(jnp.int32, (n_rows, n_keys), 0) + row0
                kj = lax.broadcasted_iota(jnp.int32, (n_rows, n_keys), 1)
                s2 = jnp.where(kj <= qi, s2, NEG_INF)
            scores.append((s2, rows, cols))
        probs = []
        for (s2, _, _), (_, _, _, _, (m_run, l_run, acc)) in zip(scores, items):
            m_new = jnp.maximum(m_run, jnp.max(s2, axis=-1, keepdims=True))
            alpha = jnp.exp2(m_run - m_new)
            p = jnp.exp2(s2 - m_new)
            l_new = alpha * l_run + jnp.sum(p, axis=-1, keepdims=True)
            probs.append((p.astype(BF16), m_new, l_new, alpha * acc))
        out = []
        for (p, m_new, l_new, acc_scaled), (_, rows, cols) in zip(probs, scores):
            acc_new = acc_scaled + jnp.dot(p, v_ref[rows, cols], preferred_element_type=F32)
            out.append((m_new, l_new, acc_new))
        return tuple(out)

    def below_diagonal(kb, carry):
        return update(kb, [(hh, 0, blk, blk, carry[hh]) for hh in range(heads_per_step)], False)

    init = tuple((jnp.full((blk, 1), NEG_INF, F32), jnp.zeros((blk, 1), F32),
                  jnp.zeros((blk, HEAD_DIM), F32)) for _ in range(heads_per_step))
    carry = lax.fori_loop(0, qt, below_diagonal, init)

    items = []
    for hh in range(heads_per_step):
        for row0 in range(0, blk, diag_rows):
            piece_carry = tuple(c[row0:row0 + diag_rows] for c in carry[hh])
            items.append((hh, row0, diag_rows, row0 + diag_rows, piece_carry))
    for (hh, row0, _, _, _), (_, l_fin, acc) in zip(items, update(qt, items, True)):
        o_ref[row0:row0 + diag_rows, hh * HEAD_DIM:(hh + 1) * HEAD_DIM] = (
            acc / l_fin).astype(o_ref.dtype)


def _fox_attention(q, kv, f_query, f_keys, bsz, seq, n_heads):
    blk = _pick_tile(seq, (FOX_BLOCK, 256, 128))
    nkb = seq // blk
    width = n_heads * HEAD_DIM
    hps = FOX_HEADS_PER_STEP
    assert n_heads % hps == 0
    n_hg = n_heads // hps
    qv = q.reshape(bsz, seq, width)
    kvv = kv.reshape(bsz, seq, 2 * width)
    o = pl.pallas_call(
        functools.partial(_fox_kernel, blk=blk, diag_rows=min(blk, FOX_DIAG_ROWS), n_heads=n_heads,
                          heads_per_step=hps),
        grid=(bsz, n_hg, nkb),
        in_specs=[
            pl.BlockSpec((None, blk, hps * HEAD_DIM), lambda b, h, t: (b, t, h)),
            pl.BlockSpec((None, seq, hps * HEAD_DIM), lambda b, h, t: (b, 0, h)),
            pl.BlockSpec((None, seq, hps * HEAD_DIM), lambda b, h, t: (b, 0, n_hg + h)),
            pl.BlockSpec((None, blk, LANES), lambda b, h, t: (b, t, 0)),
            pl.BlockSpec((None, seq, LANES), lambda b, h, t: (b, 0, 0)),
        ],
        out_specs=pl.BlockSpec((None, blk, hps * HEAD_DIM), lambda b, h, t: (b, t, h)),
        out_shape=jax.ShapeDtypeStruct((bsz, seq, width), BF16),
        compiler_params=_compiler_params(("parallel", "parallel", "arbitrary")),
        name="fox_attention",
    )(qv, kvv, kvv, f_query.reshape(bsz, seq, LANES), f_keys.reshape(bsz, seq, LANES))
    return o.reshape(bsz * seq, width)


def _rotary_lane_order():
    half = ROT_DIM // 2
    mid = HEAD_DIM // 2
    return np.concatenate([np.arange(0, half), np.arange(ROT_DIM, mid + half),
                           np.arange(half, ROT_DIM), np.arange(mid + half, HEAD_DIM)])


def _rope_tables(seq, dilations):
    half = ROT_DIM // 2
    mid = HEAD_DIM // 2
    inv = ROPE_THETA ** (-jnp.arange(0, ROT_DIM, 2, dtype=F32) / ROT_DIM)
    ang = jnp.arange(seq, dtype=F32)[:, None] * inv[None, :]
    cos, sin = jnp.cos(ang), jnp.sin(ang)
    ones = jnp.ones((seq, mid - half), F32)
    zeros = jnp.zeros((seq, mid - half), F32)
    cos_t = jnp.concatenate([cos, ones, cos, ones], axis=-1)
    sin_t = jnp.concatenate([-sin, zeros, sin, zeros], axis=-1)

    def deinterleave(t, d):
        t = t.reshape(seq // DEINT_ROWS, DEINT_ROWS // d, d, HEAD_DIM)
        return jnp.swapaxes(t, 1, 2).reshape(seq, HEAD_DIM)

    return (jnp.stack([deinterleave(cos_t, d) for d in dilations]),
            jnp.stack([deinterleave(sin_t, d) for d in dilations]))


def _split_mods(mods, parts):
    bsz, n = mods.shape
    d = n // parts
    return [mods[:, p * d:(p + 1) * d].reshape(bsz, 1, d) for p in range(parts)]


def kernel(x, c, w_ada, b_ada, g_norm_attn, g_norm_ffn, w_qkv_a, g_qk_a, w_o_a, w_ada_kv, b_ada_kv, g_norm_kv, w_kv, g_k_b, w_f, b_f, w_q_b, g_q_b, w_o_b, w_ffn_in, w_ffn_out):
    bsz, seq, d = x.shape
    depth = w_ada.shape[0]
    n_a = w_qkv_a.shape[0]
    n_groups = g_qk_a.shape[2]
    width_a = w_o_a.shape[1]
    n_heads_b = w_f.shape[1]
    assert n_groups == len(DIL_CONFIGS) and n_heads_b <= LANES and width_a == PROJ_COLS
    dilations = tuple(dl for _, dl in DIL_CONFIGS)

    x2 = x.reshape(bsz * seq, d)
    rope_tables = _rope_tables(seq, dilations)

    w_qkv = _qkv_weights(w_qkv_a, n_groups)
    g_qk = g_qk_a[..., _rotary_lane_order()]
    w_o_a = w_o_a.astype(BF16)
    w_kv = w_kv.astype(BF16)[None]
    w_q_b = w_q_b.astype(BF16)
    w_o_b = w_o_b.astype(BF16)
    w_ffn_in = w_ffn_in.astype(BF16)
    w_ffn_out = w_ffn_out.astype(BF16)
    wf_pad = jnp.pad(w_f, ((0, 0), (0, LANES - n_heads_b))).astype(BF16)
    bf_pad = jnp.pad(b_f, (0, LANES - n_heads_b)).reshape(1, LANES)

    kv = f_keys = f_query = None
    for layer in range(depth):
        sh_a, sc_a, gt_a, sh_f, sc_f, gt_f = _split_mods(_mods(c, w_ada, b_ada, layer), 6)
        if layer < n_a:
            gains = g_qk[layer].reshape(2 * n_groups, 1, HEAD_DIM)
            h_var = _deint_lhs(x2, seq, g_norm_attn[layer], sh_a, sc_a, dilations)
            qkv = _qkv_proj(h_var, seq, w_qkv, layer, gains, rope_tables)
            outs = [_dilated_group(qkv, bsz, seq, g, n_groups, width_a, window, dilation)
                    for g, (window, dilation) in enumerate(DIL_CONFIGS)]
            x2 = _mix_out([o for o, _ in outs], [l for _, l in outs], w_o_a, layer, x2, seq, gt_a)
        else:
            i = layer - n_a
            q = _proj(x2, seq, g_norm_attn[layer], sh_a, sc_a, w_q_b, i,
                      g_q_b[i].reshape(1, HEAD_DIM),
                      n_norm_tiles=w_q_b.shape[-1] // PROJ_COLS,
                      out_scale=HEAD_DIM ** -0.5 * LOG2E)
            o = _fox_attention(q, kv, f_query, f_keys, bsz, seq, n_heads_b)
            x2 = _mm_residual(o, w_o_b, i, x2, seq, gt_a)
        a = _swiglu_in(x2, seq, g_norm_ffn[layer], sh_f, sc_f, w_ffn_in, layer)
        x2 = _mm_residual(a, w_ffn_out, layer, x2, seq, gt_f)
        if layer == n_a - 1:
            sh_kv, sc_kv = _split_mods(_mods(c, w_ada_kv[None], b_ada_kv[None], 0), 2)
            kv, f_keys, f_query = _proj(
                x2, seq, g_norm_kv, sh_kv, sc_kv, w_kv, 0, g_k_b.reshape(1, HEAD_DIM),
                n_norm_tiles=w_kv.shape[-1] // 2 // PROJ_COLS, forget=(wf_pad, bf_pad, n_heads_b))
    return x2.reshape(bsz, seq, d)
```

```python
import functools
import math

import numpy as np
import jax
import jax.numpy as jnp
from jax import lax
from jax.experimental import pallas as pl
from jax.experimental.pallas import tpu as pltpu

HEAD_DIM = 128
DIL_CONFIGS = ((128, 1), (512, 4), (2048, 16))
ROT_DIM = HEAD_DIM // 4
ROPE_THETA = 500000.0
ATT_BLOCK = 128
DILATED_UNITS = 4
EPS = 1e-6
NEG_INF = -1e30
LOG2E = math.log2(math.e)
LN2 = math.log(2.0)

LANES = 128
BF16_SUBLANES = 16
VMEM_LIMIT_BYTES = 56 * 1024 * 1024

PROJ_ROWS = 1024
QKV_ROWS = 2048
PROJ_COLS = 1024
RESIDUAL_ROWS = 512
FILL_ROWS = 128
CUMSUM_ROWS = 256
DEINT_ROWS = 512
PIECE_ROWS = 256
FOX_BLOCK = 1024
FOX_DIAG_ROWS = 512
FOX_HEADS_PER_STEP = 4

F32 = jnp.float32
BF16 = jnp.bfloat16


def _compiler_params(semantics):
    return pltpu.CompilerParams(dimension_semantics=semantics, vmem_limit_bytes=VMEM_LIMIT_BYTES)


def _pick_tile(n, candidates):
    for t in candidates:
        if n % t == 0:
            return t
    raise ValueError(f"no tile in {candidates} divides {n}")


def _head_rmsnorm(a, gain):
    return a * lax.rsqrt(jnp.mean(a * a, axis=-1, keepdims=True) + EPS) * gain


def _silu(v):
    return v * jax.nn.sigmoid(v)


def _mods_kernel(c_ref, w_ref, b_ref, o_ref):
    c_act = _silu(c_ref[...])
    o_ref[...] = jnp.dot(c_act.astype(BF16), w_ref[...].astype(BF16),
                         preferred_element_type=F32) + b_ref[...]


def _mods(c, w, b, layer):
    bsz, d = c.shape
    n = w.shape[-1]
    tn = _pick_tile(n, (512, 256, 128))
    return pl.pallas_call(
        _mods_kernel,
        grid=(n // tn,),
        in_specs=[
            pl.BlockSpec((bsz, d), lambda j: (0, 0)),
            pl.BlockSpec((None, d, tn), lambda j: (layer, 0, j)),
            pl.BlockSpec((None, 1, tn), lambda j: (layer, 0, j)),
        ],
        out_specs=pl.BlockSpec((bsz, tn), lambda j: (0, j)),
        out_shape=jax.ShapeDtypeStruct((bsz, n), F32),
        compiler_params=_compiler_params(("parallel",)),
        name="mods",
    )(c, w, b.reshape(b.shape[0], 1, n))


def _normmod_rows(x_ref, rows, gain, mul, shift):
    xf = x_ref[rows, :]
    y = xf * lax.rsqrt(jnp.mean(xf * xf, axis=-1, keepdims=True) + EPS) * gain
    return y * mul + shift


def _row_pieces(tm):
    return [slice(r, r + PIECE_ROWS) for r in range(0, tm, PIECE_ROWS)]


def _lhs_pieces(first, x_ref, gn_ref, sh_ref, sc_ref, h_scr):
    gain = gn_ref[...]
    mul = 1.0 + sc_ref[...]
    shift = sh_ref[...]
    for rows in _row_pieces(h_scr.shape[0]):
        if first:
            h_scr[rows, :] = _normmod_rows(x_ref, rows, gain, mul, shift).astype(BF16)
        yield rows, h_scr[rows, :]


def _deint_lhs_kernel(x_ref, gn_ref, sh_ref, sc_ref, o_ref, slab_scr, *, dilations, bases):
    tm, d_model = x_ref.shape
    n_slabs = d_model // LANES
    gain = gn_ref[...]
    mul = 1.0 + sc_ref[...]
    shift = sh_ref[...]

    def natural(it, carry):
        rows = pl.ds(pl.multiple_of(it * FILL_ROWS, FILL_ROWS), FILL_ROWS)
        ssq = None
        for s in range(n_slabs):
            xs = x_ref[rows, s * LANES:(s + 1) * LANES]
            ssq = xs * xs if ssq is None else ssq + xs * xs
        inv = lax.rsqrt(jnp.sum(ssq, axis=-1, keepdims=True) * (1.0 / d_model) + EPS)
        inv = jnp.broadcast_to(inv, (FILL_ROWS, LANES))
        for s in range(n_slabs):
            cols = slice(s * LANES, (s + 1) * LANES)
            h = x_ref[rows, cols] * inv * gain[:, cols] * mul[:, cols] + shift[:, cols]
            slab_scr[0, s, rows, :] = h
            for v, d in enumerate(dilations):
                if d == 1:
                    o_ref[v, rows, cols] = h.astype(BF16)
        return carry

    lax.fori_loop(0, tm // FILL_ROWS, natural, 0)

    for v, d in enumerate(dilations):
        if d == 1:
            continue
        p = max(b for b in bases if d % b == 0 and b < d)
        f = d // p
        src_slabs = slab_scr.at[bases.index(p)]
        dst_slabs = slab_scr.at[bases.index(d)] if d in bases else None
        per_residue = tm // d
        units_per_trip = max(1, FILL_ROWS // per_residue)

        def gather(it, carry, v=v, d=d, p=p, f=f, per_residue=per_residue,
                   units_per_trip=units_per_trip, src_slabs=src_slabs, dst_slabs=dst_slabs):
            for u in range(units_per_trip):
                r = it * units_per_trip + u
                k = r // p
                rp = r % p
                src = pl.ds(rp * (tm // p) + k, per_residue, stride=f)
                dst = pl.ds(pl.multiple_of(r * per_residue, per_residue), per_residue)
                for s in range(n_slabs):
                    piece = src_slabs[s, src, :]
                    o_ref[v, dst, s * LANES:(s + 1) * LANES] = piece.astype(BF16)
                    if dst_slabs is not None:
                        dst_slabs[s, dst, :] = piece
            return carry

        lax.fori_loop(0, d // units_per_trip, gather, 0)


def _deint_lhs(x2, seq, gnorm, shift, scale, dilations):
    m, d = x2.shape
    tm = DEINT_ROWS
    assert seq % tm == 0 and all(tm % (dl * BF16_SUBLANES) == 0 for dl in dilations)
    tpb = seq // tm
    n_var = len(dilations)
    assert list(dilations) == sorted(dilations) and dilations[0] == 1
    bases = [1]
    for dl in dilations[1:]:
        base = max(b for b in dilations if dl % b == 0 and b < dl)
        if base not in bases:
            bases.append(base)
    bases = tuple(bases)
    return pl.pallas_call(
        functools.partial(_deint_lhs_kernel, dilations=dilations, bases=bases),
        grid=(m // tm,),
        in_specs=[
            pl.BlockSpec((tm, d), lambda i: (i, 0)),
            pl.BlockSpec((1, d), lambda i: (0, 0)),
            pl.BlockSpec((None, 1, d), lambda i: (i // tpb, 0, 0)),
            pl.BlockSpec((None, 1, d), lambda i: (i // tpb, 0, 0)),
        ],
        out_specs=pl.BlockSpec((n_var, tm, d), lambda i: (0, i, 0)),
        out_shape=jax.ShapeDtypeStruct((n_var, m, d), BF16),
        scratch_shapes=[pltpu.VMEM((len(bases), d // LANES, tm, LANES), F32)],
        compiler_params=_compiler_params(("parallel",)),
        name="deint_lhs",
    )(x2, gnorm.reshape(1, d), shift, scale)


def _dot_head_norm(pieces, w_ref, o_ref, acc_scr, gain, cos_ref=None, sin_ref=None,
                   between=None):
    tn = acc_scr.shape[1]
    w = w_ref[...]
    done = []
    for rows, lhs in pieces:
        acc_scr[rows, :] = jnp.dot(lhs, w, preferred_element_type=F32)
        done.append(rows)
    if between is not None:
        between()
    for rows in done:
        if cos_ref is not None:
            cos = cos_ref[rows, :]
            sin = sin_ref[rows, :]
        for h in range(tn // HEAD_DIM):
            cols = slice(h * HEAD_DIM, (h + 1) * HEAD_DIM)
            y = _head_rmsnorm(acc_scr[rows, cols], gain)
            if cos_ref is not None:
                y = y * cos + pltpu.roll(y, HEAD_DIM // 2, 1) * sin
            o_ref[rows, cols] = y.astype(o_ref.dtype)


def _qkv_weight_kernel(w_ref, perm_ref, o_ref, *, n_perm_tiles):
    w = w_ref[...].astype(BF16)

    @pl.when(pl.program_id(1) < n_perm_tiles)
    def _():
        for h in range(w.shape[1] // HEAD_DIM):
            cols = slice(h * HEAD_DIM, (h + 1) * HEAD_DIM)
            o_ref[:, cols] = jnp.dot(w[:, cols], perm_ref[...],
                                     preferred_element_type=F32).astype(BF16)

    @pl.when(pl.program_id(1) >= n_perm_tiles)
    def _():
        o_ref[...] = w


def _qkv_weights(w, n_groups):
    n_layers, d, n = w.shape
    tn = PROJ_COLS
    assert n == 3 * n_groups * tn
    order = _rotary_lane_order()
    perm = np.zeros((HEAD_DIM, HEAD_DIM), np.float32)
    perm[order, np.arange(HEAD_DIM)] = 1.0
    return pl.pallas_call(
        functools.partial(_qkv_weight_kernel, n_perm_tiles=2 * n_groups),
        grid=(n_layers, n // tn),
        in_specs=[
            pl.BlockSpec((None, d, tn), lambda l, j: (l, 0, j)),
            pl.BlockSpec((HEAD_DIM, HEAD_DIM), lambda l, j: (0, 0)),
        ],
        out_specs=pl.BlockSpec((None, d, tn), lambda l, j: (l, 0, j)),
        out_shape=jax.ShapeDtypeStruct(w.shape, BF16),
        compiler_params=_compiler_params(("parallel", "parallel")),
        name="qkv_weights",
    )(w, jnp.asarray(perm, BF16))


def _qkv_kernel(h_ref, w_ref, gain_ref, cos_ref, sin_ref, o_ref, acc_scr, *, n_parts):
    part = pl.program_id(1) % n_parts

    @pl.when(part < 2)
    def _():
        gain = gain_ref[...] * jnp.where(part == 0, HEAD_DIM ** -0.5 * LOG2E, 1.0)
        pieces = [(rows, h_ref[rows, :]) for rows in _row_pieces(h_ref.shape[0])]
        _dot_head_norm(pieces, w_ref, o_ref, acc_scr, gain, cos_ref, sin_ref)

    @pl.when(part >= 2)
    def _():
        o_ref[...] = jnp.dot(h_ref[...], w_ref[...],
                             preferred_element_type=F32).astype(o_ref.dtype)


def _qkv_proj(h_var, seq, w, layer, gains, rope_tables):
    n_groups, m, d = h_var.shape
    n = w.shape[-1]
    n_parts = 3
    tm = _pick_tile(seq, (QKV_ROWS, PROJ_ROWS))
    tn = PROJ_COLS
    assert n == n_parts * n_groups * tn
    tpb = seq // tm

    def col(j):
        return (j % n_parts) * n_groups + j // n_parts

    return pl.pallas_call(
        functools.partial(_qkv_kernel, n_parts=n_parts),
        grid=(m // tm, n_parts * n_groups),
        in_specs=[
            pl.BlockSpec((None, tm, d), lambda i, j: (j // n_parts, i, 0)),
            pl.BlockSpec((None, d, tn), lambda i, j: (layer, 0, col(j))),
            pl.BlockSpec((None, 1, HEAD_DIM),
                         lambda i, j: (jnp.minimum(col(j), 2 * n_groups - 1), 0, 0)),
            pl.BlockSpec((None, tm, HEAD_DIM), lambda i, j: (j // n_parts, i % tpb, 0)),
            pl.BlockSpec((None, tm, HEAD_DIM), lambda i, j: (j // n_parts, i % tpb, 0)),
        ],
        out_specs=pl.BlockSpec((tm, tn), lambda i, j: (i, col(j))),
        out_shape=jax.ShapeDtypeStruct((m, n), BF16),
        scratch_shapes=[pltpu.VMEM((tm, tn), F32)],
        compiler_params=_compiler_params(("parallel", "arbitrary")),
        name="qkv_proj",
    )(h_var, w, gains, *rope_tables)


def _split3_bf16(v):
    hi = v.astype(BF16)
    rem = v - hi.astype(F32)
    mid = rem.astype(BF16)
    lo = (rem - mid.astype(F32)).astype(BF16)
    return hi, mid, lo


def _proj_kernel(*refs, n_norm_tiles, forget, tiles_per_batch, out_scale):
    x_ref, gn_ref, sh_ref, sc_ref, w_ref, gain_ref = refs[:6]
    pos = 6
    if forget:
        wf_ref, bf_ref, tri_ref, place_ref, ones_ref = refs[pos:pos + 5]
        pos += 5
    o_ref = refs[pos]
    pos += 1
    if forget:
        fk_ref, fq_ref = refs[pos:pos + 2]
        pos += 2
    h_scr, acc_scr = refs[pos:pos + 2]
    pos += 2
    if forget:
        carry_scr = refs[pos]

    i = pl.program_id(0)
    j = pl.program_id(1)
    tm = h_scr.shape[0]
    gain = gain_ref[...] * out_scale

    def forget_stream():
        z = jnp.dot(h_scr[...], wf_ref[...], preferred_element_type=F32) + bf_ref[...]
        log_f = jnp.minimum(z, 0.0) - jnp.log1p(jnp.exp(-jnp.abs(z)))
        tri = tri_ref[...]
        parts = _split3_bf16(log_f)
        running = carry_scr[...]
        blocks = []
        for r0 in range(0, tm, CUMSUM_ROWS):
            blk_cum = running
            for part in parts:
                blk_cum = blk_cum + jnp.dot(tri, part[r0:r0 + CUMSUM_ROWS, :],
                                            preferred_element_type=F32)
            running = blk_cum[CUMSUM_ROWS - 1:CUMSUM_ROWS, :]
            blocks.append(blk_cum)
        carry_scr[...] = running
        cum = jnp.concatenate(blocks, axis=0)
        bias = ones_ref[...]
        for p, part in enumerate(_split3_bf16(cum * (-LOG2E))):
            bias = bias + jnp.dot(part, place_ref[p], preferred_element_type=F32)
        fk_ref[...] = bias[:, :LANES].astype(BF16)
        fq_ref[...] = bias[:, LANES:].astype(BF16)

    @pl.when(j == 0)
    def _():
        if forget:
            @pl.when(i % tiles_per_batch == 0)
            def _():
                carry_scr[...] = jnp.zeros_like(carry_scr)

        pieces = _lhs_pieces(True, x_ref, gn_ref, sh_ref, sc_ref, h_scr)
        _dot_head_norm(pieces, w_ref, o_ref, acc_scr, gain,
                       between=forget_stream if forget else None)

    @pl.when((j > 0) & (j < n_norm_tiles))
    def _():
        pieces = _lhs_pieces(False, x_ref, gn_ref, sh_ref, sc_ref, h_scr)
        _dot_head_norm(pieces, w_ref, o_ref, acc_scr, gain)

    @pl.when(j >= n_norm_tiles)
    def _():
        o_ref[...] = jnp.dot(h_scr[...], w_ref[...],
                             preferred_element_type=F32).astype(o_ref.dtype)


def _fox_bias_lanes(n_heads):
    assert 6 * n_heads <= LANES
    place = np.zeros((3, LANES, 2 * LANES), np.float32)
    ones = np.zeros((1, 2 * LANES), np.float32)
    for h in range(n_heads):
        for p in range(3):
            place[p, h, 3 * h + p] = 1.0
            place[p, h, LANES + 3 * (n_heads + h) + p] = -1.0
    ones[0, 3 * n_heads:6 * n_heads] = 1.0
    ones[0, LANES:LANES + 3 * n_heads] = 1.0
    return jnp.asarray(place, BF16), jnp.asarray(ones, F32)


def _proj(x2, seq, gnorm, shift, scale, w, layer, gain, n_norm_tiles, forget=None,
          out_scale=1.0):
    m, d = x2.shape
    n = w.shape[-1]
    tm = PROJ_ROWS
    tn = PROJ_COLS
    assert seq % tm == 0 and n % tn == 0
    tpb = seq // tm
    with_forget = forget is not None

    in_specs = [
        pl.BlockSpec((tm, d), lambda i, j: (i, 0)),
        pl.BlockSpec((1, d), lambda i, j: (0, 0)),
        pl.BlockSpec((None, 1, d), lambda i, j: (i // tpb, 0, 0)),
        pl.BlockSpec((None, 1, d), lambda i, j: (i // tpb, 0, 0)),
        pl.BlockSpec((None, d, tn), lambda i, j: (layer, 0, j)),
        pl.BlockSpec((1, HEAD_DIM), lambda i, j: (0, 0)),
    ]
    args = [x2, gnorm.reshape(1, d), shift, scale, w, gain]
    out_specs = [pl.BlockSpec((tm, tn), lambda i, j: (i, j))]
    out_shape = [jax.ShapeDtypeStruct((m, n), BF16)]
    scratch = [pltpu.VMEM((tm, d), BF16), pltpu.VMEM((tm, tn), F32)]
    if with_forget:
        wf_pad, bf_pad, n_heads = forget
        tri = (lax.broadcasted_iota(jnp.int32, (CUMSUM_ROWS, CUMSUM_ROWS), 0)
               >= lax.broadcasted_iota(jnp.int32, (CUMSUM_ROWS, CUMSUM_ROWS), 1)).astype(BF16)
        place, ones = _fox_bias_lanes(n_heads)
        in_specs += [
            pl.BlockSpec((d, LANES), lambda i, j: (0, 0)),
            pl.BlockSpec((1, LANES), lambda i, j: (0, 0)),
            pl.BlockSpec((CUMSUM_ROWS, CUMSUM_ROWS), lambda i, j: (0, 0)),
            pl.BlockSpec((3, LANES, 2 * LANES), lambda i, j: (0, 0, 0)),
            pl.BlockSpec((1, 2 * LANES), lambda i, j: (0, 0)),
        ]
        args += [wf_pad, bf_pad, tri, place, ones]
        out_specs += [
            pl.BlockSpec((tm, LANES), lambda i, j: (i, 0)),
            pl.BlockSpec((tm, LANES), lambda i, j: (i, 0)),
        ]
        out_shape += [
            jax.ShapeDtypeStruct((m, LANES), BF16),
            jax.ShapeDtypeStruct((m, LANES), BF16),
        ]
        scratch.append(pltpu.VMEM((1, LANES), F32))

    kern = functools.partial(_proj_kernel, n_norm_tiles=n_norm_tiles, forget=with_forget,
                             tiles_per_batch=tpb, out_scale=out_scale)
    outs = pl.pallas_call(
        kern,
        grid=(m // tm, n // tn),
        in_specs=in_specs,
        out_specs=out_specs,
        out_shape=out_shape,
        scratch_shapes=scratch,
        compiler_params=_compiler_params(("arbitrary", "arbitrary")),
        name="proj",
    )(*args)
    return outs if with_forget else outs[0]


def _swiglu_kernel(x_ref, gn_ref, sh_ref, sc_ref, wg_ref, wu_ref, o_ref, h_scr):
    def gated(h):
        g = jnp.dot(h, wg_ref[...], preferred_element_type=F32)
        u = jnp.dot(h, wu_ref[...], preferred_element_type=F32)
        return (_silu(g) * u).astype(o_ref.dtype)

    @pl.when(pl.program_id(1) == 0)
    def _():
        for rows, h in _lhs_pieces(True, x_ref, gn_ref, sh_ref, sc_ref, h_scr):
            o_ref[rows, :] = gated(h)

    @pl.when(pl.program_id(1) > 0)
    def _():
        o_ref[...] = gated(h_scr[...])


def _swiglu_in(x2, seq, gnorm, shift, scale, w, layer):
    m, d = x2.shape
    f = w.shape[-1] // 2
    tm = PROJ_ROWS
    tf = _pick_tile(f, (512, 256, 128))
    assert seq % tm == 0
    tpb = seq // tm
    nf = f // tf
    return pl.pallas_call(
        _swiglu_kernel,
        grid=(m // tm, nf),
        in_specs=[
            pl.BlockSpec((tm, d), lambda i, j: (i, 0)),
            pl.BlockSpec((1, d), lambda i, j: (0, 0)),
            pl.BlockSpec((None, 1, d), lambda i, j: (i // tpb, 0, 0)),
            pl.BlockSpec((None, 1, d), lambda i, j: (i // tpb, 0, 0)),
            pl.BlockSpec((None, d, tf), lambda i, j: (layer, 0, j)),
            pl.BlockSpec((None, d, tf), lambda i, j: (layer, 0, nf + j)),
        ],
        out_specs=pl.BlockSpec((tm, tf), lambda i, j: (i, j)),
        out_shape=jax.ShapeDtypeStruct((m, f), BF16),
        scratch_shapes=[pltpu.VMEM((tm, d), BF16)],
        compiler_params=_compiler_params(("parallel", "arbitrary")),
        name="swiglu_in",
    )(x2, gnorm.reshape(1, d), shift, scale, w, w)


def _mmres_kernel(a_ref, w_ref, x_ref, gt_ref, o_ref):
    y = jnp.dot(a_ref[...], w_ref[...], preferred_element_type=F32)
    o_ref[...] = x_ref[...] + gt_ref[...] * y


def _mm_residual(a, w, layer, x2, seq, gate):
    m, k = a.shape
    n = w.shape[-1]
    tm = RESIDUAL_ROWS
    assert seq % tm == 0
    tpb = seq // tm
    return pl.pallas_call(
        _mmres_kernel,
        grid=(m // tm,),
        in_specs=[
            pl.BlockSpec((tm, k), lambda i: (i, 0)),
            pl.BlockSpec((None, k, n), lambda i: (layer, 0, 0), pipeline_mode=pl.Buffered(1)),
            pl.BlockSpec((tm, n), lambda i: (i, 0)),
            pl.BlockSpec((None, 1, n), lambda i: (i // tpb, 0, 0)),
        ],
        out_specs=pl.BlockSpec((tm, n), lambda i: (i, 0)),
        out_shape=jax.ShapeDtypeStruct((m, n), F32),
        compiler_params=_compiler_params(("parallel",)),
        name="mm_residual",
    )(a, w, x2, gate)


def _dilated_kernel(*refs, n_back, n_heads, dilation, has_prev, units):
    ins = refs[:3 * units]
    o_ref, lse_ref, o_scr = refs[3 * units:3 * units + 3]
    if has_prev:
        kp_scr, vp_scr = refs[3 * units + 3:]
    step_blk = pl.program_id(1)
    step_res = pl.program_id(2)
    width = n_heads * HEAD_DIM
    n_keys = (2 if has_prev else 1) * ATT_BLOCK
    qi = lax.broadcasted_iota(jnp.int32, (ATT_BLOCK, n_keys), 0)
    kj = lax.broadcasted_iota(jnp.int32, (ATT_BLOCK, n_keys), 1)
    dist = qi + (n_keys - ATT_BLOCK) - kj
    band = (dist >= 0) & (dist <= n_back)
    lane = lax.broadcasted_iota(jnp.int32, (ATT_BLOCK, LANES), 1)
    contract_last = (((1,), (1,)), ((), ()))
    head_cols = [slice(h * HEAD_DIM, (h + 1) * HEAD_DIM) for h in range(n_heads)]

    blocks = []
    for u in range(units):
        q_ref, kc_ref, vc_ref = ins[3 * u:3 * u + 3]
        if dilation == 1:
            blk, res = step_blk * units + u, 0
            rows = slice(u * ATT_BLOCK, (u + 1) * ATT_BLOCK)
        else:
            blk, res = step_blk, step_res * units + u
            rows = pl.ds(res, ATT_BLOCK, stride=dilation)
        q = q_ref[...].reshape(ATT_BLOCK, width)
        k = kc_ref[...].reshape(ATT_BLOCK, width)
        v = vc_ref[...].reshape(ATT_BLOCK, width)
        mask = band
        if has_prev:
            kp_ref = kp_scr.at[res]
            vp_ref = vp_scr.at[res]
            if dilation > 1 or u == 0:
                @pl.when(blk == 0)
                def _(kp_ref=kp_ref, vp_ref=vp_ref):
                    kp_ref[...] = jnp.zeros_like(kp_ref)
                    vp_ref[...] = jnp.zeros_like(vp_ref)
            k_prev = kp_ref[...]
            v_prev = vp_ref[...]
            kp_ref[...] = k
            vp_ref[...] = v
            k = jnp.concatenate([k_prev, k], axis=0)
            v = jnp.concatenate([v_prev, v], axis=0)
            mask = band & ((kj >= ATT_BLOCK) | (blk > 0))
        blocks.append((q, k, v, mask, rows))

    scores = [[lax.dot_general(q[:, cols], k[:, cols], contract_last,
                               preferred_element_type=F32) for cols in head_cols]
              for q, k, _, _, _ in blocks]
    probs = []
    for (_, _, _, mask, rows), unit_scores in zip(blocks, scores):
        unit_probs = []
        lse_tile = jnp.zeros((ATT_BLOCK, LANES), F32)
        for h, s2 in enumerate(unit_scores):
            s2 = jnp.where(mask, s2, NEG_INF)
            mx2 = jnp.max(s2, axis=-1, keepdims=True)
            p = jnp.exp2(s2 - mx2)
            den = jnp.sum(p, axis=-1, keepdims=True)
            unit_probs.append((p.astype(BF16), den))
            lse_tile = jnp.where(lane == h, mx2 * LN2 + jnp.log(den), lse_tile)
        lse_ref[rows, :] = lse_tile
        probs.append(unit_probs)
    for (_, _, v, _, rows), unit_probs in zip(blocks, probs):
        for h, (p, den) in enumerate(unit_probs):
            o = jnp.dot(p, v[:, head_cols[h]], preferred_element_type=F32)
            o_scr[h, rows, :] = o / den

    @pl.when(step_res == pl.num_programs(2) - 1)
    def _():
        for h in range(n_heads):
            o_ref[:, h * HEAD_DIM:(h + 1) * HEAD_DIM] = o_scr[h].astype(o_ref.dtype)


def _dilated_group(qkv, bsz, seq, group, n_groups, width, window, dilation):
    n_back = window // dilation
    sub_len = seq // dilation
    assert sub_len % ATT_BLOCK == 0
    nb = sub_len // ATT_BLOCK
    n_cols = qkv.shape[1]
    tpb = seq // DEINT_ROWS
    per_residue = DEINT_ROWS // dilation

    if per_residue >= ATT_BLOCK:
        bpt = per_residue // ATT_BLOCK
        view = qkv.reshape(bsz, tpb, dilation, bpt, ATT_BLOCK, n_cols)
        block = (None, None, None, None, ATT_BLOCK, width)

        def rows_index(b, n, r):
            return (b, n // bpt, r, n % bpt, 0)
    else:
        pieces = ATT_BLOCK // per_residue
        view = qkv.reshape(bsz, tpb, dilation, per_residue, n_cols)
        block = (None, pieces, None, per_residue, width)

        def rows_index(b, n, r):
            return (b, n, r, 0)

    units = DILATED_UNITS if (nb if dilation == 1 else dilation) % DILATED_UNITS == 0 else 1

    def spec(part, u):
        def index(b, n, r):
            if dilation == 1:
                rows = rows_index(b, n * units + u, r)
            else:
                rows = rows_index(b, n, r * units + u)
            return rows + (part * n_groups + group,)
        return pl.BlockSpec(block, index)

    if dilation == 1:
        grid = (bsz, nb // units, 1)
        span = ATT_BLOCK * units
    else:
        grid = (bsz, nb, dilation // units)
        span = ATT_BLOCK * dilation
    has_prev = nb > 1
    in_specs = [spec(part, u) for u in range(units) for part in range(3)]
    scratch = [pltpu.VMEM((width // HEAD_DIM, span, HEAD_DIM), F32)]
    if has_prev:
        scratch += [pltpu.VMEM((dilation, ATT_BLOCK, width), BF16)] * 2
    o, lse = pl.pallas_call(
        functools.partial(_dilated_kernel, n_back=n_back, n_heads=width // HEAD_DIM,
                          dilation=dilation, has_prev=has_prev, units=units),
        grid=grid,
        in_specs=in_specs,
        out_specs=[
            pl.BlockSpec((None, span, width), lambda b, n, r: (b, n, 0)),
            pl.BlockSpec((None, span, LANES), lambda b, n, r: (b, n, 0)),
        ],
        out_shape=[
            jax.ShapeDtypeStruct((bsz, seq, width), BF16),
            jax.ShapeDtypeStruct((bsz, seq, LANES), F32),
        ],
        scratch_shapes=scratch,
        compiler_params=_compiler_params(("parallel", "arbitrary", "arbitrary")),
        name="dilated_attention",
    )(*([view] * len(in_specs)))
    return o.reshape(bsz * seq, width), lse.reshape(bsz * seq, LANES)


def _mix_out_kernel(*refs, n_groups, n_heads, row_chunk):
    o_refs = refs[:n_groups]
    lse_refs = refs[n_groups:2 * n_groups]
    w_ref, x_ref, gt_ref, out_ref, lhs_scr = refs[2 * n_groups:]
    tm = x_ref.shape[0]
    w = w_ref[...]
    gate = gt_ref[...]

    for r in range(0, tm, row_chunk):
        rows = slice(r, r + row_chunk)
        lses = [ref[rows, :] for ref in lse_refs]
        mx = functools.reduce(jnp.maximum, lses)
        es = [jnp.exp(l - mx) for l in lses]
        inv = 1.0 / functools.reduce(lambda a, b: a + b, es)
        alphas = [e * inv for e in es]
        for h in range(n_heads):
            cols = slice(h * HEAD_DIM, (h + 1) * HEAD_DIM)
            mixed = None
            for g in range(n_groups):
                term = alphas[g][:, h:h + 1] * o_refs[g][rows, cols].astype(F32)
                mixed = term if mixed is None else mixed + term
            lhs_scr[rows, cols] = mixed.astype(BF16)
        y = jnp.dot(lhs_scr[rows, :], w, preferred_element_type=F32)
        out_ref[rows, :] = x_ref[rows, :] + gate * y


def _mix_out(o_list, lse_list, w, layer, x2, seq, gate):
    m, width = o_list[0].shape
    n = w.shape[-1]
    n_groups = len(o_list)
    tm = _pick_tile(seq, (512, 256, 128))
    tpb = seq // tm
    return pl.pallas_call(
        functools.partial(_mix_out_kernel, n_groups=n_groups, n_heads=width // HEAD_DIM,
                          row_chunk=128),
        grid=(m // tm,),
        in_specs=(
            [pl.BlockSpec((tm, width), lambda i: (i, 0))] * n_groups
            + [pl.BlockSpec((tm, LANES), lambda i: (i, 0))] * n_groups
            + [
                pl.BlockSpec((None, width, n), lambda i: (layer, 0, 0)),
                pl.BlockSpec((tm, n), lambda i: (i, 0)),
                pl.BlockSpec((None, 1, n), lambda i: (i // tpb, 0, 0)),
            ]
        ),
        out_specs=pl.BlockSpec((tm, n), lambda i: (i, 0)),
        out_shape=jax.ShapeDtypeStruct((m, n), F32),
        scratch_shapes=[pltpu.VMEM((tm, width), BF16)],
        compiler_params=_compiler_params(("parallel",)),
        name="mix_out",
    )(*o_list, *lse_list, w, x2, gate)


def _fox_kernel(q_ref, k_ref, v_ref, fq_ref, fk_ref, o_ref, *, blk, diag_rows, n_heads,
                heads_per_step):
    head0 = pl.program_id(1) * heads_per_step
    qt = pl.program_id(2)
    lane = lax.broadcasted_iota(jnp.int32, (1, LANES), 1)
    fq_all = fq_ref[...]
    contract_last = (((1,), (1,)), ((), ()))

    qs = []
    for hh in range(heads_per_step):
        off = (lane % (3 * n_heads)) - 3 * (head0 + hh)
        keep = jnp.where(off >= 0, jnp.where(off < 3, 1.0, 0.0), 0.0)
        keep = jnp.where(lane < 6 * n_heads, keep, 0.0).astype(BF16)
        qs.append(jnp.concatenate([q_ref[:, hh * HEAD_DIM:(hh + 1) * HEAD_DIM],
                                   fq_all * keep], axis=1))

    def update(kb, items, diagonal):
        scores = []
        for hh, row0, n_rows, n_keys, _ in items:
            rows = pl.ds(pl.multiple_of(kb * blk, blk), n_keys)
            cols = slice(hh * HEAD_DIM, (hh + 1) * HEAD_DIM)
            k = jnp.concatenate([k_ref[rows, cols], fk_ref[rows, :]], axis=1)
            s2 = lax.dot_general(qs[hh][row0:row0 + n_rows, :], k, contract_last,
                                 preferred_element_type=F32)
            if diagonal:
                qi = lax.broadcasted_iota(jnp.int32, (n_rows, n_keys), 0) + row0
                kj = lax.broadcasted_iota(jnp.int32, (n_rows, n_keys), 1)
                s2 = jnp.where(kj <= qi, s2, NEG_INF)
            scores.append((s2, rows, cols))
        probs = []
        for (s2, _, _), (_, _, _, _, (m_run, l_run, acc)) in zip(scores, items):
            m_new = jnp.maximum(m_run, jnp.max(s2, axis=-1, keepdims=True))
            alpha = jnp.exp2(m_run - m_new)
            p = jnp.exp2(s2 - m_new)
            l_new = alpha * l_run + jnp.sum(p, axis=-1, keepdims=True)
            probs.append((p.astype(BF16), m_new, l_new, alpha * acc))
        out = []
        for (p, m_new, l_new, acc_scaled), (_, rows, cols) in zip(probs, scores):
            acc_new = acc_scaled + jnp.dot(p, v_ref[rows, cols], preferred_element_type=F32)
            out.append((m_new, l_new, acc_new))
        return tuple(out)

    def below_diagonal(kb, carry):
        return update(kb, [(hh, 0, blk, blk, carry[hh]) for hh in range(heads_per_step)], False)

    init = tuple((jnp.full((blk, 1), NEG_INF, F32), jnp.zeros((blk, 1), F32),
                  jnp.zeros((blk, HEAD_DIM), F32)) for _ in range(heads_per_step))
    carry = lax.fori_loop(0, qt, below_diagonal, init)

    items = []
    for hh in range(heads_per_step):
        for row0 in range(0, blk, diag_rows):
            piece_carry = tuple(c[row0:row0 + diag_rows] for c in carry[hh])
            items.append((hh, row0, diag_rows, row0 + diag_rows, piece_carry))
    for (hh, row0, _, _, _), (_, l_fin, acc) in zip(items, update(qt, items, True)):
        o_ref[row0:row0 + diag_rows, hh * HEAD_DIM:(hh + 1) * HEAD_DIM] = (
            acc / l_fin).astype(o_ref.dtype)


def _fox_attention(q, kv, f_query, f_keys, bsz, seq, n_heads):
    blk = _pick_tile(seq, (FOX_BLOCK, 256, 128))
    nkb = seq // blk
    width = n_heads * HEAD_DIM
    hps = FOX_HEADS_PER_STEP
    assert n_heads % hps == 0
    n_hg = n_heads // hps
    qv = q.reshape(bsz, seq, width)
    kvv = kv.reshape(bsz, seq, 2 * width)
    o = pl.pallas_call(
        functools.partial(_fox_kernel, blk=blk, diag_rows=min(blk, FOX_DIAG_ROWS), n_heads=n_heads,
                          heads_per_step=hps),
        grid=(bsz, n_hg, nkb),
        in_specs=[
            pl.BlockSpec((None, blk, hps * HEAD_DIM), lambda b, h, t: (b, t, h)),
            pl.BlockSpec((None, seq, hps * HEAD_DIM), lambda b, h, t: (b, 0, h)),
            pl.BlockSpec((None, seq, hps * HEAD_DIM), lambda b, h, t: (b, 0, n_hg + h)),
            pl.BlockSpec((None, blk, LANES), lambda b, h, t: (b, t, 0)),
            pl.BlockSpec((None, seq, LANES), lambda b, h, t: (b, 0, 0)),
        ],
        out_specs=pl.BlockSpec((None, blk, hps * HEAD_DIM), lambda b, h, t: (b, t, h)),
        out_shape=jax.ShapeDtypeStruct((bsz, seq, width), BF16),
        compiler_params=_compiler_params(("parallel", "parallel", "arbitrary")),
        name="fox_attention",
    )(qv, kvv, kvv, f_query.reshape(bsz, seq, LANES), f_keys.reshape(bsz, seq, LANES))
    return o.reshape(bsz * seq, width)


def _rotary_lane_order():
    half = ROT_DIM // 2
    mid = HEAD_DIM // 2
    return np.concatenate([np.arange(0, half), np.arange(ROT_DIM, mid + half),
                           np.arange(half, ROT_DIM), np.arange(mid + half, HEAD_DIM)])


def _rope_tables(seq, dilations):
    half = ROT_DIM // 2
    mid = HEAD_DIM // 2
    inv = ROPE_THETA ** (-jnp.arange(0, ROT_DIM, 2, dtype=F32) / ROT_DIM)
    ang = jnp.arange(seq, dtype=F32)[:, None] * inv[None, :]
    cos, sin = jnp.cos(ang), jnp.sin(ang)
    ones = jnp.ones((seq, mid - half), F32)
    zeros = jnp.zeros((seq, mid - half), F32)
    cos_t = jnp.concatenate([cos, ones, cos, ones], axis=-1)
    sin_t = jnp.concatenate([-sin, zeros, sin, zeros], axis=-1)

    def deinterleave(t, d):
        t = t.reshape(seq // DEINT_ROWS, DEINT_ROWS // d, d, HEAD_DIM)
        return jnp.swapaxes(t, 1, 2).reshape(seq, HEAD_DIM)

    return (jnp.stack([deinterleave(cos_t, d) for d in dilations]),
            jnp.stack([deinterleave(sin_t, d) for d in dilations]))


def _split_mods(mods, parts):
    bsz, n = mods.shape
    d = n // parts
    return [mods[:, p * d:(p + 1) * d].reshape(bsz, 1, d) for p in range(parts)]


def kernel(x, c, w_ada, b_ada, g_norm_attn, g_norm_ffn, w_qkv_a, g_qk_a, w_o_a, w_ada_kv, b_ada_kv, g_norm_kv, w_kv, g_k_b, w_f, b_f, w_q_b, g_q_b, w_o_b, w_ffn_in, w_ffn_out):
    bsz, seq, d = x.shape
    depth = w_ada.shape[0]
    n_a = w_qkv_a.shape[0]
    n_groups = g_qk_a.shape[2]
    width_a = w_o_a.shape[1]
    n_heads_b = w_f.shape[1]
    assert n_groups == len(DIL_CONFIGS) and n_heads_b <= LANES and width_a == PROJ_COLS
    dilations = tuple(dl for _, dl in DIL_CONFIGS)

    x2 = x.reshape(bsz * seq, d)
    rope_tables = _rope_tables(seq, dilations)

    w_qkv = _qkv_weights(w_qkv_a, n_groups)
    g_qk = g_qk_a[..., _rotary_lane_order()]
    w_o_a = w_o_a.astype(BF16)
    w_kv = w_kv.astype(BF16)[None]
    w_q_b = w_q_b.astype(BF16)
    w_o_b = w_o_b.astype(BF16)
    w_ffn_in = w_ffn_in.astype(BF16)
    w_ffn_out = w_ffn_out.astype(BF16)
    wf_pad = jnp.pad(w_f, ((0, 0), (0, LANES - n_heads_b))).astype(BF16)
    bf_pad = jnp.pad(b_f, (0, LANES - n_heads_b)).reshape(1, LANES)

    kv = f_keys = f_query = None
    for layer in range(depth):
        sh_a, sc_a, gt_a, sh_f, sc_f, gt_f = _split_mods(_mods(c, w_ada, b_ada, layer), 6)
        if layer < n_a:
            gains = g_qk[layer].reshape(2 * n_groups, 1, HEAD_DIM)
            h_var = _deint_lhs(x2, seq, g_norm_attn[layer], sh_a, sc_a, dilations)
            qkv = _qkv_proj(h_var, seq, w_qkv, layer, gains, rope_tables)
            outs = [_dilated_group(qkv, bsz, seq, g, n_groups, width_a, window, dilation)
                    for g, (window, dilation) in enumerate(DIL_CONFIGS)]
            x2 = _mix_out([o for o, _ in outs], [l for _, l in outs], w_o_a, layer, x2, seq, gt_a)
        else:
            i = layer - n_a
            q = _proj(x2, seq, g_norm_attn[layer], sh_a, sc_a, w_q_b, i,
                      g_q_b[i].reshape(1, HEAD_DIM),
                      n_norm_tiles=w_q_b.shape[-1] // PROJ_COLS,
                      out_scale=HEAD_DIM ** -0.5 * LOG2E)
            o = _fox_attention(q, kv, f_query, f_keys, bsz, seq, n_heads_b)
            x2 = _mm_residual(o, w_o_b, i, x2, seq, gt_a)
        a = _swiglu_in(x2, seq, g_norm_ffn[layer], sh_f, sc_f, w_ffn_in, layer)
        x2 = _mm_residual(a, w_ffn_out, layer, x2, seq, gt_f)
        if layer == n_a - 1:
            sh_kv, sc_kv = _split_mods(_mods(c, w_ada_kv[None], b_ada_kv[None], 0), 2)
            kv, f_keys, f_query = _proj(
                x2, seq, g_norm_kv, sh_kv, sc_kv, w_kv, 0, g_k_b.reshape(1, HEAD_DIM),
                n_norm_tiles=w_kv.shape[-1] // 2 // PROJ_COLS, forget=(wf_pad, bf_pad, n_heads_b))
    return x2.reshape(bsz, seq, d)
```

```python
import functools
import math

import numpy as np
import jax
import jax.numpy as jnp
from jax import lax
from jax.experimental import pallas as pl
from jax.experimental.pallas import tpu as pltpu

HEAD_DIM = 128
DIL_CONFIGS = ((128, 1), (512, 4), (2048, 16))
ROT_DIM = HEAD_DIM // 4
ROPE_THETA = 500000.0
ATT_BLOCK = 128
DILATED_UNITS = 4
EPS = 1e-6
NEG_INF = -1e30
LOG2E = math.log2(math.e)
LN2 = math.log(2.0)

LANES = 128
BF16_SUBLANES = 16
VMEM_LIMIT_BYTES = 56 * 1024 * 1024

PROJ_ROWS = 1024
QKV_ROWS = 2048
PROJ_COLS = 1024
RESIDUAL_ROWS = 512
FILL_ROWS = 128
CUMSUM_ROWS = 256
DEINT_ROWS = 512
PIECE_ROWS = 256
FOX_BLOCK = 2048
FOX_DIAG_ROWS = 512
FOX_HEADS_PER_STEP = 2

F32 = jnp.float32
BF16 = jnp.bfloat16


def _compiler_params(semantics):
    return pltpu.CompilerParams(dimension_semantics=semantics, vmem_limit_bytes=VMEM_LIMIT_BYTES)


def _pick_tile(n, candidates):
    for t in candidates:
        if n % t == 0:
            return t
    raise ValueError(f"no tile in {candidates} divides {n}")


def _head_rmsnorm(a, gain):
    return a * lax.rsqrt(jnp.mean(a * a, axis=-1, keepdims=True) + EPS) * gain


def _silu(v):
    return v * jax.nn.sigmoid(v)


def _mods_kernel(c_ref, w_ref, b_ref, o_ref):
    c_act = _silu(c_ref[...])
    o_ref[...] = jnp.dot(c_act.astype(BF16), w_ref[...].astype(BF16),
                         preferred_element_type=F32) + b_ref[...]


def _mods(c, w, b, layer):
    bsz, d = c.shape
    n = w.shape[-1]
    tn = _pick_tile(n, (512, 256, 128))
    return pl.pallas_call(
        _mods_kernel,
        grid=(n // tn,),
        in_specs=[
            pl.BlockSpec((bsz, d), lambda j: (0, 0)),
            pl.BlockSpec((None, d, tn), lambda j: (layer, 0, j)),
            pl.BlockSpec((None, 1, tn), lambda j: (layer, 0, j)),
        ],
        out_specs=pl.BlockSpec((bsz, tn), lambda j: (0, j)),
        out_shape=jax.ShapeDtypeStruct((bsz, n), F32),
        compiler_params=_compiler_params(("parallel",)),
        name="mods",
    )(c, w, b.reshape(b.shape[0], 1, n))


def _normmod_rows(x_ref, rows, gain, mul, shift):
    xf = x_ref[rows, :]
    y = xf * lax.rsqrt(jnp.mean(xf * xf, axis=-1, keepdims=True) + EPS) * gain
    return y * mul + shift


def _row_pieces(tm):
    return [slice(r, r + PIECE_ROWS) for r in range(0, tm, PIECE_ROWS)]


def _lhs_pieces(first, x_ref, gn_ref, sh_ref, sc_ref, h_scr):
    gain = gn_ref[...]
    mul = 1.0 + sc_ref[...]
    shift = sh_ref[...]
    for rows in _row_pieces(h_scr.shape[0]):
        if first:
            h_scr[rows, :] = _normmod_rows(x_ref, rows, gain, mul, shift).astype(BF16)
        yield rows, h_scr[rows, :]


def _deint_lhs_kernel(x_ref, gn_ref, sh_ref, sc_ref, o_ref, slab_scr, *, dilations, bases):
    tm, d_model = x_ref.shape
    n_slabs = d_model // LANES
    gain = gn_ref[...]
    mul = 1.0 + sc_ref[...]
    shift = sh_ref[...]

    def natural(it, carry):
        rows = pl.ds(pl.multiple_of(it * FILL_ROWS, FILL_ROWS), FILL_ROWS)
        ssq = None
        for s in range(n_slabs):
            xs = x_ref[rows, s * LANES:(s + 1) * LANES]
            ssq = xs * xs if ssq is None else ssq + xs * xs
        inv = lax.rsqrt(jnp.sum(ssq, axis=-1, keepdims=True) * (1.0 / d_model) + EPS)
        inv = jnp.broadcast_to(inv, (FILL_ROWS, LANES))
        for s in range(n_slabs):
            cols = slice(s * LANES, (s + 1) * LANES)
            h = x_ref[rows, cols] * inv * gain[:, cols] * mul[:, cols] + shift[:, cols]
            slab_scr[0, s, rows, :] = h
            for v, d in enumerate(dilations):
                if d == 1:
                    o_ref[v, rows, cols] = h.astype(BF16)
        return carry

    lax.fori_loop(0, tm // FILL_ROWS, natural, 0)

    for v, d in enumerate(dilations):
        if d == 1:
            continue
        p = max(b for b in bases if d % b == 0 and b < d)
        f = d // p
        src_slabs = slab_scr.at[bases.index(p)]
        dst_slabs = slab_scr.at[bases.index(d)] if d in bases else None
        per_residue = tm // d
        units_per_trip = max(1, FILL_ROWS // per_residue)

        def gather(it, carry, v=v, d=d, p=p, f=f, per_residue=per_residue,
                   units_per_trip=units_per_trip, src_slabs=src_slabs, dst_slabs=dst_slabs):
            for u in range(units_per_trip):
                r = it * units_per_trip + u
                k = r // p
                rp = r % p
                src = pl.ds(rp * (tm // p) + k, per_residue, stride=f)
                dst = pl.ds(pl.multiple_of(r * per_residue, per_residue), per_residue)
                for s in range(n_slabs):
                    piece = src_slabs[s, src, :]
                    o_ref[v, dst, s * LANES:(s + 1) * LANES] = piece.astype(BF16)
                    if dst_slabs is not None:
                        dst_slabs[s, dst, :] = piece
            return carry

        lax.fori_loop(0, d // units_per_trip, gather, 0)


def _deint_lhs(x2, seq, gnorm, shift, scale, dilations):
    m, d = x2.shape
    tm = DEINT_ROWS
    assert seq % tm == 0 and all(tm % (dl * BF16_SUBLANES) == 0 for dl in dilations)
    tpb = seq // tm
    n_var = len(dilations)
    assert list(dilations) == sorted(dilations) and dilations[0] == 1
    bases = [1]
    for dl in dilations[1:]:
        base = max(b for b in dilations if dl % b == 0 and b < dl)
        if base not in bases:
            bases.append(base)
    bases = tuple(bases)
    return pl.pallas_call(
        functools.partial(_deint_lhs_kernel, dilations=dilations, bases=bases),
        grid=(m // tm,),
        in_specs=[
            pl.BlockSpec((tm, d), lambda i: (i, 0)),
            pl.BlockSpec((1, d), lambda i: (0, 0)),
            pl.BlockSpec((None, 1, d), lambda i: (i // tpb, 0, 0)),
            pl.BlockSpec((None, 1, d), lambda i: (i // tpb, 0, 0)),
        ],
        out_specs=pl.BlockSpec((n_var, tm, d), lambda i: (0, i, 0)),
        out_shape=jax.ShapeDtypeStruct((n_var, m, d), BF16),
        scratch_shapes=[pltpu.VMEM((len(bases), d // LANES, tm, LANES), F32)],
        compiler_params=_compiler_params(("parallel",)),
        name="deint_lhs",
    )(x2, gnorm.reshape(1, d), shift, scale)


def _dot_head_norm(pieces, w_ref, o_ref, acc_scr, gain, cos_ref=None, sin_ref=None,
                   between=None):
    tn = acc_scr.shape[1]
    w = w_ref[...]
    done = []
    for rows, lhs in pieces:
        acc_scr[rows, :] = jnp.dot(lhs, w, preferred_element_type=F32)
        done.append(rows)
    if between is not None:
        between()
    for rows in done:
        if cos_ref is not None:
            cos = cos_ref[rows, :]
            sin = sin_ref[rows, :]
        for h in range(tn // HEAD_DIM):
            cols = slice(h * HEAD_DIM, (h + 1) * HEAD_DIM)
            y = _head_rmsnorm(acc_scr[rows, cols], gain)
            if cos_ref is not None:
                y = y * cos + pltpu.roll(y, HEAD_DIM // 2, 1) * sin
            o_ref[rows, cols] = y.astype(o_ref.dtype)


def _qkv_weight_kernel(w_ref, perm_ref, o_ref, *, n_perm_tiles):
    w = w_ref[...].astype(BF16)

    @pl.when(pl.program_id(1) < n_perm_tiles)
    def _():
        for h in range(w.shape[1] // HEAD_DIM):
            cols = slice(h * HEAD_DIM, (h + 1) * HEAD_DIM)
            o_ref[:, cols] = jnp.dot(w[:, cols], perm_ref[...],
                                     preferred_element_type=F32).astype(BF16)

    @pl.when(pl.program_id(1) >= n_perm_tiles)
    def _():
        o_ref[...] = w


def _qkv_weights(w, n_groups):
    n_layers, d, n = w.shape
    tn = PROJ_COLS
    assert n == 3 * n_groups * tn
    order = _rotary_lane_order()
    perm = np.zeros((HEAD_DIM, HEAD_DIM), np.float32)
    perm[order, np.arange(HEAD_DIM)] = 1.0
    return pl.pallas_call(
        functools.partial(_qkv_weight_kernel, n_perm_tiles=2 * n_groups),
        grid=(n_layers, n // tn),
        in_specs=[
            pl.BlockSpec((None, d, tn), lambda l, j: (l, 0, j)),
            pl.BlockSpec((HEAD_DIM, HEAD_DIM), lambda l, j: (0, 0)),
        ],
        out_specs=pl.BlockSpec((None, d, tn), lambda l, j: (l, 0, j)),
        out_shape=jax.ShapeDtypeStruct(w.shape, BF16),
        compiler_params=_compiler_params(("parallel", "parallel")),
        name="qkv_weights",
    )(w, jnp.asarray(perm, BF16))


def _qkv_kernel(h_ref, w_ref, gain_ref, cos_ref, sin_ref, o_ref, acc_scr, *, n_parts):
    part = pl.program_id(1) % n_parts

    @pl.when(part < 2)
    def _():
        gain = gain_ref[...] * jnp.where(part == 0, HEAD_DIM ** -0.5 * LOG2E, 1.0)
        pieces = [(rows, h_ref[rows, :]) for rows in _row_pieces(h_ref.shape[0])]
        _dot_head_norm(pieces, w_ref, o_ref, acc_scr, gain, cos_ref, sin_ref)

    @pl.when(part >= 2)
    def _():
        o_ref[...] = jnp.dot(h_ref[...], w_ref[...],
                             preferred_element_type=F32).astype(o_ref.dtype)


def _qkv_proj(h_var, seq, w, layer, gains, rope_tables):
    n_groups, m, d = h_var.shape
    n = w.shape[-1]
    n_parts = 3
    tm = _pick_tile(seq, (QKV_ROWS, PROJ_ROWS))
    tn = PROJ_COLS
    assert n == n_parts * n_groups * tn
    tpb = seq // tm

    def col(j):
        return (j % n_parts) * n_groups + j // n_parts

    return pl.pallas_call(
        functools.partial(_qkv_kernel, n_parts=n_parts),
        grid=(m // tm, n_parts * n_groups),
        in_specs=[
            pl.BlockSpec((None, tm, d), lambda i, j: (j // n_parts, i, 0)),
            pl.BlockSpec((None, d, tn), lambda i, j: (layer, 0, col(j))),
            pl.BlockSpec((None, 1, HEAD_DIM),
                         lambda i, j: (jnp.minimum(col(j), 2 * n_groups - 1), 0, 0)),
            pl.BlockSpec((None, tm, HEAD_DIM), lambda i, j: (j // n_parts, i % tpb, 0)),
            pl.BlockSpec((None, tm, HEAD_DIM), lambda i, j: (j // n_parts, i % tpb, 0)),
        ],
        out_specs=pl.BlockSpec((tm, tn), lambda i, j: (i, col(j))),
        out_shape=jax.ShapeDtypeStruct((m, n), BF16),
        scratch_shapes=[pltpu.VMEM((tm, tn), F32)],
        compiler_params=_compiler_params(("parallel", "arbitrary")),
        name="qkv_proj",
    )(h_var, w, gains, *rope_tables)


def _split3_bf16(v):
    hi = v.astype(BF16)
    rem = v - hi.astype(F32)
    mid = rem.astype(BF16)
    lo = (rem - mid.astype(F32)).astype(BF16)
    return hi, mid, lo


def _proj_kernel(*refs, n_norm_tiles, forget, tiles_per_batch, out_scale):
    x_ref, gn_ref, sh_ref, sc_ref, w_ref, gain_ref = refs[:6]
    pos = 6
    if forget:
        wf_ref, bf_ref, tri_ref, place_ref, ones_ref = refs[pos:pos + 5]
        pos += 5
    o_ref = refs[pos]
    pos += 1
    if forget:
        fk_ref, fq_ref = refs[pos:pos + 2]
        pos += 2
    h_scr, acc_scr = refs[pos:pos + 2]
    pos += 2
    if forget:
        carry_scr = refs[pos]

    i = pl.program_id(0)
    j = pl.program_id(1)
    tm = h_scr.shape[0]
    gain = gain_ref[...] * out_scale

    def forget_stream():
        z = jnp.dot(h_scr[...], wf_ref[...], preferred_element_type=F32) + bf_ref[...]
        log_f = jnp.minimum(z, 0.0) - jnp.log1p(jnp.exp(-jnp.abs(z)))
        tri = tri_ref[...]
        parts = _split3_bf16(log_f)
        running = carry_scr[...]
        blocks = []
        for r0 in range(0, tm, CUMSUM_ROWS):
            blk_cum = running
            for part in parts:
                blk_cum = blk_cum + jnp.dot(tri, part[r0:r0 + CUMSUM_ROWS, :],
                                            preferred_element_type=F32)
            running = blk_cum[CUMSUM_ROWS - 1:CUMSUM_ROWS, :]
            blocks.append(blk_cum)
        carry_scr[...] = running
        cum = jnp.concatenate(blocks, axis=0)
        bias = ones_ref[...]
        for p, part in enumerate(_split3_bf16(cum * (-LOG2E))):
            bias = bias + jnp.dot(part, place_ref[p], preferred_element_type=F32)
        fk_ref[...] = bias[:, :LANES].astype(BF16)
        fq_ref[...] = bias[:, LANES:].astype(BF16)

    @pl.when(j == 0)
    def _():
        if forget:
            @pl.when(i % tiles_per_batch == 0)
            def _():
                carry_scr[...] = jnp.zeros_like(carry_scr)

        pieces = _lhs_pieces(True, x_ref, gn_ref, sh_ref, sc_ref, h_scr)
        _dot_head_norm(pieces, w_ref, o_ref, acc_scr, gain,
                       between=forget_stream if forget else None)

    @pl.when((j > 0) & (j < n_norm_tiles))
    def _():
        pieces = _lhs_pieces(False, x_ref, gn_ref, sh_ref, sc_ref, h_scr)
        _dot_head_norm(pieces, w_ref, o_ref, acc_scr, gain)

    @pl.when(j >= n_norm_tiles)
    def _():
        o_ref[...] = jnp.dot(h_scr[...], w_ref[...],
                             preferred_element_type=F32).astype(o_ref.dtype)


def _fox_bias_lanes(n_heads):
    assert 6 * n_heads <= LANES
    place = np.zeros((3, LANES, 2 * LANES), np.float32)
    ones = np.zeros((1, 2 * LANES), np.float32)
    for h in range(n_heads):
        for p in range(3):
            place[p, h, 3 * h + p] = 1.0
            place[p, h, LANES + 3 * (n_heads + h) + p] = -1.0
    ones[0, 3 * n_heads:6 * n_heads] = 1.0
    ones[0, LANES:LANES + 3 * n_heads] = 1.0
    return jnp.asarray(place, BF16), jnp.asarray(ones, F32)


def _proj(x2, seq, gnorm, shift, scale, w, layer, gain, n_norm_tiles, forget=None,
          out_scale=1.0):
    m, d = x2.shape
    n = w.shape[-1]
    tm = PROJ_ROWS
    tn = PROJ_COLS
    assert seq % tm == 0 and n % tn == 0
    tpb = seq // tm
    with_forget = forget is not None

    in_specs = [
        pl.BlockSpec((tm, d), lambda i, j: (i, 0)),
        pl.BlockSpec((1, d), lambda i, j: (0, 0)),
        pl.BlockSpec((None, 1, d), lambda i, j: (i // tpb, 0, 0)),
        pl.BlockSpec((None, 1, d), lambda i, j: (i // tpb, 0, 0)),
        pl.BlockSpec((None, d, tn), lambda i, j: (layer, 0, j)),
        pl.BlockSpec((1, HEAD_DIM), lambda i, j: (0, 0)),
    ]
    args = [x2, gnorm.reshape(1, d), shift, scale, w, gain]
    out_specs = [pl.BlockSpec((tm, tn), lambda i, j: (i, j))]
    out_shape = [jax.ShapeDtypeStruct((m, n), BF16)]
    scratch = [pltpu.VMEM((tm, d), BF16), pltpu.VMEM((tm, tn), F32)]
    if with_forget:
        wf_pad, bf_pad, n_heads = forget
        tri = (lax.broadcasted_iota(jnp.int32, (CUMSUM_ROWS, CUMSUM_ROWS), 0)
               >= lax.broadcasted_iota(jnp.int32, (CUMSUM_ROWS, CUMSUM_ROWS), 1)).astype(BF16)
        place, ones = _fox_bias_lanes(n_heads)
        in_specs += [
            pl.BlockSpec((d, LANES), lambda i, j: (0, 0)),
            pl.BlockSpec((1, LANES), lambda i, j: (0, 0)),
            pl.BlockSpec((CUMSUM_ROWS, CUMSUM_ROWS), lambda i, j: (0, 0)),
            pl.BlockSpec((3, LANES, 2 * LANES), lambda i, j: (0, 0, 0)),
            pl.BlockSpec((1, 2 * LANES), lambda i, j: (0, 0)),
        ]
        args += [wf_pad, bf_pad, tri, place, ones]
        out_specs += [
            pl.BlockSpec((tm, LANES), lambda i, j: (i, 0)),
            pl.BlockSpec((tm, LANES), lambda i, j: (i, 0)),
        ]
        out_shape += [
            jax.ShapeDtypeStruct((m, LANES), BF16),
            jax.ShapeDtypeStruct((m, LANES), BF16),
        ]
        scratch.append(pltpu.VMEM((1, LANES), F32))

    kern = functools.partial(_proj_kernel, n_norm_tiles=n_norm_tiles, forget=with_forget,
                             tiles_per_batch=tpb, out_scale=out_scale)
    outs = pl.pallas_call(
        kern,
        grid=(m // tm, n // tn),
        in_specs=in_specs,
        out_specs=out_specs,
        out_shape=out_shape,
        scratch_shapes=scratch,
        compiler_params=_compiler_params(("arbitrary", "arbitrary")),
        name="proj",
    )(*args)
    return outs if with_forget else outs[0]


def _swiglu_kernel(x_ref, gn_ref, sh_ref, sc_ref, wg_ref, wu_ref, o_ref, h_scr):
    def gated(h):
        g = jnp.dot(h, wg_ref[...], preferred_element_type=F32)
        u = jnp.dot(h, wu_ref[...], preferred_element_type=F32)
        return (_silu(g) * u).astype(o_ref.dtype)

    @pl.when(pl.program_id(1) == 0)
    def _():
        for rows, h in _lhs_pieces(True, x_ref, gn_ref, sh_ref, sc_ref, h_scr):
            o_ref[rows, :] = gated(h)

    @pl.when(pl.program_id(1) > 0)
    def _():
        o_ref[...] = gated(h_scr[...])


def _swiglu_in(x2, seq, gnorm, shift, scale, w, layer):
    m, d = x2.shape
    f = w.shape[-1] // 2
    tm = PROJ_ROWS
    tf = _pick_tile(f, (512, 256, 128))
    assert seq % tm == 0
    tpb = seq // tm
    nf = f // tf
    return pl.pallas_call(
        _swiglu_kernel,
        grid=(m // tm, nf),
        in_specs=[
            pl.BlockSpec((tm, d), lambda i, j: (i, 0)),
            pl.BlockSpec((1, d), lambda i, j: (0, 0)),
            pl.BlockSpec((None, 1, d), lambda i, j: (i // tpb, 0, 0)),
            pl.BlockSpec((None, 1, d), lambda i, j: (i // tpb, 0, 0)),
            pl.BlockSpec((None, d, tf), lambda i, j: (layer, 0, j)),
            pl.BlockSpec((None, d, tf), lambda i, j: (layer, 0, nf + j)),
        ],
        out_specs=pl.BlockSpec((tm, tf), lambda i, j: (i, j)),
        out_shape=jax.ShapeDtypeStruct((m, f), BF16),
        scratch_shapes=[pltpu.VMEM((tm, d), BF16)],
        compiler_params=_compiler_params(("parallel", "arbitrary")),
        name="swiglu_in",
    )(x2, gnorm.reshape(1, d), shift, scale, w, w)


def _mmres_kernel(a_ref, w_ref, x_ref, gt_ref, o_ref):
    y = jnp.dot(a_ref[...], w_ref[...], preferred_element_type=F32)
    o_ref[...] = x_ref[...] + gt_ref[...] * y


def _mm_residual(a, w, layer, x2, seq, gate):
    m, k = a.shape
    n = w.shape[-1]
    tm = RESIDUAL_ROWS
    assert seq % tm == 0
    tpb = seq // tm
    return pl.pallas_call(
        _mmres_kernel,
        grid=(m // tm,),
        in_specs=[
            pl.BlockSpec((tm, k), lambda i: (i, 0)),
            pl.BlockSpec((None, k, n), lambda i: (layer, 0, 0), pipeline_mode=pl.Buffered(1)),
            pl.BlockSpec((tm, n), lambda i: (i, 0)),
            pl.BlockSpec((None, 1, n), lambda i: (i // tpb, 0, 0)),
        ],
        out_specs=pl.BlockSpec((tm, n), lambda i: (i, 0)),
        out_shape=jax.ShapeDtypeStruct((m, n), F32),
        compiler_params=_compiler_params(("parallel",)),
        name="mm_residual",
    )(a, w, x2, gate)


def _dilated_kernel(*refs, n_back, n_heads, dilation, has_prev, units):
    ins = refs[:3 * units]
    o_ref, lse_ref, o_scr = refs[3 * units:3 * units + 3]
    if has_prev:
        kp_scr, vp_scr = refs[3 * units + 3:]
    step_blk = pl.program_id(1)
    step_res = pl.program_id(2)
    width = n_heads * HEAD_DIM
    n_keys = (2 if has_prev else 1) * ATT_BLOCK
    qi = lax.broadcasted_iota(jnp.int32, (ATT_BLOCK, n_keys), 0)
    kj = lax.broadcasted_iota(jnp.int32, (ATT_BLOCK, n_keys), 1)
    dist = qi + (n_keys - ATT_BLOCK) - kj
    band = (dist >= 0) & (dist <= n_back)
    lane = lax.broadcasted_iota(jnp.int32, (ATT_BLOCK, LANES), 1)
    contract_last = (((1,), (1,)), ((), ()))
    head_cols = [slice(h * HEAD_DIM, (h + 1) * HEAD_DIM) for h in range(n_heads)]

    blocks = []
    for u in range(units):
        q_ref, kc_ref, vc_ref = ins[3 * u:3 * u + 3]
        if dilation == 1:
            blk, res = step_blk * units + u, 0
            rows = slice(u * ATT_BLOCK, (u + 1) * ATT_BLOCK)
        else:
            blk, res = step_blk, step_res * units + u
            rows = pl.ds(res, ATT_BLOCK, stride=dilation)
        q = q_ref[...].reshape(ATT_BLOCK, width)
        k = kc_ref[...].reshape(ATT_BLOCK, width)
        v = vc_ref[...].reshape(ATT_BLOCK, width)
        mask = band
        if has_prev:
            kp_ref = kp_scr.at[res]
            vp_ref = vp_scr.at[res]
            if dilation > 1 or u == 0:
                @pl.when(blk == 0)
                def _(kp_ref=kp_ref, vp_ref=vp_ref):
                    kp_ref[...] = jnp.zeros_like(kp_ref)
                    vp_ref[...] = jnp.zeros_like(vp_ref)
            k_prev = kp_ref[...]
            v_prev = vp_ref[...]
            kp_ref[...] = k
            vp_ref[...] = v
            k = jnp.concatenate([k_prev, k], axis=0)
            v = jnp.concatenate([v_prev, v], axis=0)
            mask = band & ((kj >= ATT_BLOCK) | (blk > 0))
        blocks.append((q, k, v, mask, rows))

    scores = [[lax.dot_general(q[:, cols], k[:, cols], contract_last,
                               preferred_element_type=F32) for cols in head_cols]
              for q, k, _, _, _ in blocks]
    probs = []
    for (_, _, _, mask, rows), unit_scores in zip(blocks, scores):
        unit_probs = []
        lse_tile = jnp.zeros((ATT_BLOCK, LANES), F32)
        for h, s2 in enumerate(unit_scores):
            s2 = jnp.where(mask, s2, NEG_INF)
            mx2 = jnp.max(s2, axis=-1, keepdims=True)
            p = jnp.exp2(s2 - mx2)
            den = jnp.sum(p, axis=-1, keepdims=True)
            unit_probs.append((p.astype(BF16), den))
            lse_tile = jnp.where(lane == h, mx2 * LN2 + jnp.log(den), lse_tile)
        lse_ref[rows, :] = lse_tile
        probs.append(unit_probs)
    for (_, _, v, _, rows), unit_probs in zip(blocks, probs):
        for h, (p, den) in enumerate(unit_probs):
            o = jnp.dot(p, v[:, head_cols[h]], preferred_element_type=F32)
            o_scr[h, rows, :] = o / den

    @pl.when(step_res == pl.num_programs(2) - 1)
    def _():
        for h in range(n_heads):
            o_ref[:, h * HEAD_DIM:(h + 1) * HEAD_DIM] = o_scr[h].astype(o_ref.dtype)


def _dilated_group(qkv, bsz, seq, group, n_groups, width, window, dilation):
    n_back = window // dilation
    sub_len = seq // dilation
    assert sub_len % ATT_BLOCK == 0
    nb = sub_len // ATT_BLOCK
    n_cols = qkv.shape[1]
    tpb = seq // DEINT_ROWS
    per_residue = DEINT_ROWS // dilation

    if per_residue >= ATT_BLOCK:
        bpt = per_residue // ATT_BLOCK
        view = qkv.reshape(bsz, tpb, dilation, bpt, ATT_BLOCK, n_cols)
        block = (None, None, None, None, ATT_BLOCK, width)

        def rows_index(b, n, r):
            return (b, n // bpt, r, n % bpt, 0)
    else:
        pieces = ATT_BLOCK // per_residue
        view = qkv.reshape(bsz, tpb, dilation, per_residue, n_cols)
        block = (None, pieces, None, per_residue, width)

        def rows_index(b, n, r):
            return (b, n, r, 0)

    units = DILATED_UNITS if (nb if dilation == 1 else dilation) % DILATED_UNITS == 0 else 1

    def spec(part, u):
        def index(b, n, r):
            if dilation == 1:
                rows = rows_index(b, n * units + u, r)
            else:
                rows = rows_index(b, n, r * units + u)
            return rows + (part * n_groups + group,)
        return pl.BlockSpec(block, index)

    if dilation == 1:
        grid = (bsz, nb // units, 1)
        span = ATT_BLOCK * units
    else:
        grid = (bsz, nb, dilation // units)
        span = ATT_BLOCK * dilation
    has_prev = nb > 1
    in_specs = [spec(part, u) for u in range(units) for part in range(3)]
    scratch = [pltpu.VMEM((width // HEAD_DIM, span, HEAD_DIM), F32)]
    if has_prev:
        scratch += [pltpu.VMEM((dilation, ATT_BLOCK, width), BF16)] * 2
    o, lse = pl.pallas_call(
        functools.partial(_dilated_kernel, n_back=n_back, n_heads=width // HEAD_DIM,
                          dilation=dilation, has_prev=has_prev, units=units),
        grid=grid,
        in_specs=in_specs,
        out_specs=[
            pl.BlockSpec((None, span, width), lambda b, n, r: (b, n, 0)),
            pl.BlockSpec((None, span, LANES), lambda b, n, r: (b, n, 0)),
        ],
        out_shape=[
            jax.ShapeDtypeStruct((bsz, seq, width), BF16),
            jax.ShapeDtypeStruct((bsz, seq, LANES), F32),
        ],
        scratch_shapes=scratch,
        compiler_params=_compiler_params(("parallel", "arbitrary", "arbitrary")),
        name="dilated_attention",
    )(*([view] * len(in_specs)))
    return o.reshape(bsz * seq, width), lse.reshape(bsz * seq, LANES)


def _mix_out_kernel(*refs, n_groups, n_heads, row_chunk):
    o_refs = refs[:n_groups]
    lse_refs = refs[n_groups:2 * n_groups]
    w_ref, x_ref, gt_ref, out_ref, lhs_scr = refs[2 * n_groups:]
    tm = x_ref.shape[0]
    w = w_ref[...]
    gate = gt_ref[...]

    for r in range(0, tm, row_chunk):
        rows = slice(r, r + row_chunk)
        lses = [ref[rows, :] for ref in lse_refs]
        mx = functools.reduce(jnp.maximum, lses)
        es = [jnp.exp(l - mx) for l in lses]
        inv = 1.0 / functools.reduce(lambda a, b: a + b, es)
        alphas = [e * inv for e in es]
        for h in range(n_heads):
            cols = slice(h * HEAD_DIM, (h + 1) * HEAD_DIM)
            mixed = None
            for g in range(n_groups):
                term = alphas[g][:, h:h + 1] * o_refs[g][rows, cols].astype(F32)
                mixed = term if mixed is None else mixed + term
            lhs_scr[rows, cols] = mixed.astype(BF16)
        y = jnp.dot(lhs_scr[rows, :], w, preferred_element_type=F32)
        out_ref[rows, :] = x_ref[rows, :] + gate * y


def _mix_out(o_list, lse_list, w, layer, x2, seq, gate):
    m, width = o_list[0].shape
    n = w.shape[-1]
    n_groups = len(o_list)
    tm = _pick_tile(seq, (512, 256, 128))
    tpb = seq // tm
    return pl.pallas_call(
        functools.partial(_mix_out_kernel, n_groups=n_groups, n_heads=width // HEAD_DIM,
                          row_chunk=128),
        grid=(m // tm,),
        in_specs=(
            [pl.BlockSpec((tm, width), lambda i: (i, 0))] * n_groups
            + [pl.BlockSpec((tm, LANES), lambda i: (i, 0))] * n_groups
            + [
                pl.BlockSpec((None, width, n), lambda i: (layer, 0, 0)),
                pl.BlockSpec((tm, n), lambda i: (i, 0)),
                pl.BlockSpec((None, 1, n), lambda i: (i // tpb, 0, 0)),
            ]
        ),
        out_specs=pl.BlockSpec((tm, n), lambda i: (i, 0)),
        out_shape=jax.ShapeDtypeStruct((m, n), F32),
        scratch_shapes=[pltpu.VMEM((tm, width), BF16)],
        compiler_params=_compiler_params(("parallel",)),
        name="mix_out",
    )(*o_list, *lse_list, w, x2, gate)


def _fox_kernel(q_ref, k_ref, v_ref, fq_ref, fk_ref, o_ref, *, blk, diag_rows, n_heads,
                heads_per_step):
    head0 = pl.program_id(1) * heads_per_step
    qt = pl.program_id(2)
    lane = lax.broadcasted_iota(jnp.int32, (1, LANES), 1)
    fq_all = fq_ref[...]
    contract_last = (((1,), (1,)), ((), ()))

    qs = []
    for hh in range(heads_per_step):
        off = (lane % (3 * n_heads)) - 3 * (head0 + hh)
        keep = jnp.where(off >= 0, jnp.where(off < 3, 1.0, 0.0), 0.0)
        keep = jnp.where(lane < 6 * n_heads, keep, 0.0).astype(BF16)
        qs.append(jnp.concatenate([q_ref[:, hh * HEAD_DIM:(hh + 1) * HEAD_DIM],
                                   fq_all * keep], axis=1))

    def update(kb, items, diagonal):
        scores = []
        for hh, row0, n_rows, n_keys, _ in items:
            rows = pl.ds(pl.multiple_of(kb * blk, blk), n_keys)
            cols = slice(hh * HEAD_DIM, (hh + 1) * HEAD_DIM)
            k = jnp.concatenate([k_ref[rows, cols], fk_ref[rows, :]], axis=1)
            s2 = lax.dot_general(qs[hh][row0:row0 + n_rows, :], k, contract_last,
                                 preferred_element_type=F32)
            if diagonal:
                qi = lax.broadcasted_iota(jnp.int32, (n_rows, n_keys), 0) + row0
                kj = lax.broadcasted_iota(jnp.int32, (n_rows, n_keys), 1)
                s2 = jnp.where(kj <= qi, s2, NEG_INF)
            scores.append((s2, rows, cols))
        probs = []
        for (s2, _, _), (_, _, _, _, (m_run, l_run, acc)) in zip(scores, items):
            m_new = jnp.maximum(m_run, jnp.max(s2, axis=-1, keepdims=True))
            alpha = jnp.exp2(m_run - m_new)
            p = jnp.exp2(s2 - m_new)
            l_new = alpha * l_run + jnp.sum(p, axis=-1, keepdims=True)
            probs.append((p.astype(BF16), m_new, l_new, alpha * acc))
        out = []
        for (p, m_new, l_new, acc_scaled), (_, rows, cols) in zip(probs, scores):
            acc_new = acc_scaled + jnp.dot(p, v_ref[rows, cols], preferred_element_type=F32)
            out.append((m_new, l_new, acc_new))
        return tuple(out)

    def below_diagonal(kb, carry):
        return update(kb, [(hh, 0, blk, blk, carry[hh]) for hh in range(heads_per_step)], False)

    init = tuple((jnp.full((blk, 1), NEG_INF, F32), jnp.zeros((blk, 1), F32),
                  jnp.zeros((blk, HEAD_DIM), F32)) for _ in range(heads_per_step))
    carry = lax.fori_loop(0, qt, below_diagonal, init)

    items = []
    for hh in range(heads_per_step):
        for row0 in range(0, blk, diag_rows):
            piece_carry = tuple(c[row0:row0 + diag_rows] for c in carry[hh])
            items.append((hh, row0, diag_rows, row0 + diag_rows, piece_carry))
    for (hh, row0, _, _, _), (_, l_fin, acc) in zip(items, update(qt, items, True)):
        o_ref[row0:row0 + diag_rows, hh * HEAD_DIM:(hh + 1) * HEAD_DIM] = (
            acc / l_fin).astype(o_ref.dtype)


def _fox_attention(q, kv, f_query, f_keys, bsz, seq, n_heads):
    blk = _pick_tile(seq, (FOX_BLOCK, 256, 128))
    nkb = seq // blk
    width = n_heads * HEAD_DIM
    hps = FOX_HEADS_PER_STEP
    assert n_heads % hps == 0
    n_hg = n_heads // hps
    qv = q.reshape(bsz, seq, width)
    kvv = kv.reshape(bsz, seq, 2 * width)
    o = pl.pallas_call(
        functools.partial(_fox_kernel, blk=blk, diag_rows=min(blk, FOX_DIAG_ROWS), n_heads=n_heads,
                          heads_per_step=hps),
        grid=(bsz, n_hg, nkb),
        in_specs=[
            pl.BlockSpec((None, blk, hps * HEAD_DIM), lambda b, h, t: (b, t, h)),
            pl.BlockSpec((None, seq, hps * HEAD_DIM), lambda b, h, t: (b, 0, h)),
            pl.BlockSpec((None, seq, hps * HEAD_DIM), lambda b, h, t: (b, 0, n_hg + h)),
            pl.BlockSpec((None, blk, LANES), lambda b, h, t: (b, t, 0)),
            pl.BlockSpec((None, seq, LANES), lambda b, h, t: (b, 0, 0)),
        ],
        out_specs=pl.BlockSpec((None, blk, hps * HEAD_DIM), lambda b, h, t: (b, t, h)),
        out_shape=jax.ShapeDtypeStruct((bsz, seq, width), BF16),
        compiler_params=_compiler_params(("parallel", "parallel", "arbitrary")),
        name="fox_attention",
    )(qv, kvv, kvv, f_query.reshape(bsz, seq, LANES), f_keys.reshape(bsz, seq, LANES))
    return o.reshape(bsz * seq, width)


def _rotary_lane_order():
    half = ROT_DIM // 2
    mid = HEAD_DIM // 2
    return np.concatenate([np.arange(0, half), np.arange(ROT_DIM, mid + half),
                           np.arange(half, ROT_DIM), np.arange(mid + half, HEAD_DIM)])


def _rope_tables(seq, dilations):
    half = ROT_DIM // 2
    mid = HEAD_DIM // 2
    inv = ROPE_THETA ** (-jnp.arange(0, ROT_DIM, 2, dtype=F32) / ROT_DIM)
    ang = jnp.arange(seq, dtype=F32)[:, None] * inv[None, :]
    cos, sin = jnp.cos(ang), jnp.sin(ang)
    ones = jnp.ones((seq, mid - half), F32)
    zeros = jnp.zeros((seq, mid - half), F32)
    cos_t = jnp.concatenate([cos, ones, cos, ones], axis=-1)
    sin_t = jnp.concatenate([-sin, zeros, sin, zeros], axis=-1)

    def deinterleave(t, d):
        t = t.reshape(seq // DEINT_ROWS, DEINT_ROWS // d, d, HEAD_DIM)
        return jnp.swapaxes(t, 1, 2).reshape(seq, HEAD_DIM)

    return (jnp.stack([deinterleave(cos_t, d) for d in dilations]),
            jnp.stack([deinterleave(sin_t, d) for d in dilations]))


def _split_mods(mods, parts):
    bsz, n = mods.shape
    d = n // parts
    return [mods[:, p * d:(p + 1) * d].reshape(bsz, 1, d) for p in range(parts)]


def kernel(x, c, w_ada, b_ada, g_norm_attn, g_norm_ffn, w_qkv_a, g_qk_a, w_o_a, w_ada_kv, b_ada_kv, g_norm_kv, w_kv, g_k_b, w_f, b_f, w_q_b, g_q_b, w_o_b, w_ffn_in, w_ffn_out):
    bsz, seq, d = x.shape
    depth = w_ada.shape[0]
    n_a = w_qkv_a.shape[0]
    n_groups = g_qk_a.shape[2]
    width_a = w_o_a.shape[1]
    n_heads_b = w_f.shape[1]
    assert n_groups == len(DIL_CONFIGS) and n_heads_b <= LANES and width_a == PROJ_COLS
    dilations = tuple(dl for _, dl in DIL_CONFIGS)

    x2 = x.reshape(bsz * seq, d)
    rope_tables = _rope_tables(seq, dilations)

    w_qkv = _qkv_weights(w_qkv_a, n_groups)
    g_qk = g_qk_a[..., _rotary_lane_order()]
    w_o_a = w_o_a.astype(BF16)
    w_kv = w_kv.astype(BF16)[None]
    w_q_b = w_q_b.astype(BF16)
    w_o_b = w_o_b.astype(BF16)
    w_ffn_in = w_ffn_in.astype(BF16)
    w_ffn_out = w_ffn_out.astype(BF16)
    wf_pad = jnp.pad(w_f, ((0, 0), (0, LANES - n_heads_b))).astype(BF16)
    bf_pad = jnp.pad(b_f, (0, LANES - n_heads_b)).reshape(1, LANES)

    kv = f_keys = f_query = None
    for layer in range(depth):
        sh_a, sc_a, gt_a, sh_f, sc_f, gt_f = _split_mods(_mods(c, w_ada, b_ada, layer), 6)
        if layer < n_a:
            gains = g_qk[layer].reshape(2 * n_groups, 1, HEAD_DIM)
            h_var = _deint_lhs(x2, seq, g_norm_attn[layer], sh_a, sc_a, dilations)
            qkv = _qkv_proj(h_var, seq, w_qkv, layer, gains, rope_tables)
            outs = [_dilated_group(qkv, bsz, seq, g, n_groups, width_a, window, dilation)
                    for g, (window, dilation) in enumerate(DIL_CONFIGS)]
            x2 = _mix_out([o for o, _ in outs], [l for _, l in outs], w_o_a, layer, x2, seq, gt_a)
        else:
            i = layer - n_a
            q = _proj(x2, seq, g_norm_attn[layer], sh_a, sc_a, w_q_b, i,
                      g_q_b[i].reshape(1, HEAD_DIM),
                      n_norm_tiles=w_q_b.shape[-1] // PROJ_COLS,
                      out_scale=HEAD_DIM ** -0.5 * LOG2E)
            o = _fox_attention(q, kv, f_query, f_keys, bsz, seq, n_heads_b)
            x2 = _mm_residual(o, w_o_b, i, x2, seq, gt_a)
        a = _swiglu_in(x2, seq, g_norm_ffn[layer], sh_f, sc_f, w_ffn_in, layer)
        x2 = _mm_residual(a, w_ffn_out, layer, x2, seq, gt_f)
        if layer == n_a - 1:
            sh_kv, sc_kv = _split_mods(_mods(c, w_ada_kv[None], b_ada_kv[None], 0), 2)
            kv, f_keys, f_query = _proj(
                x2, seq, g_norm_kv, sh_kv, sc_kv, w_kv, 0, g_k_b.reshape(1, HEAD_DIM),
                n_norm_tiles=w_kv.shape[-1] // 2 // PROJ_COLS, forget=(wf_pad, bf_pad, n_heads_b))
    return x2.reshape(bsz, seq, d)
```
